```python
import jax, jax.numpy as jnp
from jax import lax
import numpy as np

D_MODEL = 2048
BATCH = 4
SEQ = 2048
DEPTH = 2

D_MIX = D_MODEL
GM_HEADS = 8
GM_HEAD_DIM = D_MIX // 2 // GM_HEADS
GM_WIDTH = GM_HEADS * GM_HEAD_DIM
GM_CHUNK = 128
SSD_WIDTH = D_MIX - GM_WIDTH
SSD_HEAD_DIM = 64
SSD_HEADS = SSD_WIDTH // SSD_HEAD_DIM
SSD_GROUPS = 2
SSD_STATE = 128
SSD_CONV = 4
SSD_CHUNK = 128
SSD_CONV_DIM = SSD_WIDTH + 2 * SSD_GROUPS * SSD_STATE
IN_PROJ_DIM = 2 * GM_WIDTH + SSD_WIDTH + SSD_CONV_DIM + SSD_HEADS
N_EXPERTS = 64
N_EXPERT_GROUPS = 8
EXPERTS_PER_GROUP = N_EXPERTS // N_EXPERT_GROUPS
TOP_K = 2
D_EXPERT = 512
MOE_BLOCK = 128
EPS = 1e-6

kernel_name = 'hybrid_gmlp_ssd_moe_adaln'


def rms_norm(x, w):
    xf = x.astype(jnp.float32)
    y = xf * lax.rsqrt(jnp.mean(xf * xf, -1, keepdims=True) + EPS)
    return (y * w.astype(jnp.float32)).astype(x.dtype)


def layer_norm(x, w, b):
    xf = x.astype(jnp.float32)
    mu = jnp.mean(xf, -1, keepdims=True)
    var = jnp.mean(jnp.square(xf - mu), -1, keepdims=True)
    y = (xf - mu) * lax.rsqrt(var + EPS) * w.astype(jnp.float32) + b.astype(jnp.float32)
    return y.astype(x.dtype)


def gmlp_spatial_gating(u, v, ln_w, ln_b, w_s, b_s):
    bsz, seq, _ = u.shape
    nc = seq // GM_CHUNK
    u = jax.nn.gelu(u, approximate=False)
    v = jax.nn.gelu(v, approximate=False).reshape(bsz, seq, GM_HEADS, GM_HEAD_DIM)
    v = layer_norm(v, ln_w.reshape(GM_HEADS, GM_HEAD_DIM), ln_b.reshape(GM_HEADS, GM_HEAD_DIM))
    v = v.reshape(bsz, nc, GM_CHUNK, GM_HEADS, GM_HEAD_DIM)
    causal = jnp.tril(jnp.ones((GM_CHUNK, GM_CHUNK), dtype=bool))
    w = jnp.where(causal[None], w_s, 0).astype(v.dtype)
    s = jnp.einsum('hts,bcshp->bcthp', w, v) + b_s.T.astype(v.dtype)[None, None, :, :, None]
    return u * s.reshape(bsz, seq, GM_WIDTH)


def causal_depthwise_conv(x, w, b):
    k = w.shape[0]
    y = lax.conv_general_dilated(x, w[:, None, :].astype(x.dtype), window_strides=(1,),
                                 padding=[(k - 1, 0)], dimension_numbers=('NWC', 'WIO', 'NWC'),
                                 feature_group_count=x.shape[-1])
    return y + b.astype(x.dtype)


def segsum_exp(a):
    cs = jnp.cumsum(a, -1)
    diff = cs[..., :, None] - cs[..., None, :]
    mask = jnp.tril(jnp.ones((a.shape[-1], a.shape[-1]), dtype=bool))
    return jnp.where(mask, jnp.exp(jnp.where(mask, diff, 0.0)), 0.0)


def ssd_scan(x, dt, a, bm, cm):
    out_dtype = x.dtype
    bsz, seq = x.shape[:2]
    nc = seq // SSD_CHUNK
    hpg = SSD_HEADS // SSD_GROUPS
    x = x.astype(jnp.float32); dt = dt.astype(jnp.float32)
    a = a.astype(jnp.float32)
    xd = (x * dt[..., None]).reshape(bsz, nc, SSD_CHUNK, SSD_GROUPS, hpg, SSD_HEAD_DIM)
    ad = (dt * a).reshape(bsz, nc, SSD_CHUNK, SSD_GROUPS, hpg).transpose(0, 1, 3, 4, 2)
    bm = bm.astype(jnp.float32).reshape(bsz, nc, SSD_CHUNK, SSD_GROUPS, SSD_STATE)
    cm = cm.astype(jnp.float32).reshape(bsz, nc, SSD_CHUNK, SSD_GROUPS, SSD_STATE)
    a_cs = jnp.cumsum(ad, -1)
    decay = segsum_exp(ad)
    cb = jnp.einsum('bclgn,bcsgn->bcgls', cm, bm)
    y_diag = jnp.einsum('bcgls,bcghls,bcsghp->bclghp', cb, decay, xd)
    decay_states = jnp.exp(a_cs[..., -1:] - a_cs)
    states = jnp.einsum('bclgn,bcghl,bclghp->bcghpn', bm, decay_states, xd)
    chunk_decay = jnp.exp(a_cs[..., -1])

    def step(carry, inp):
        st, dec = inp
        return carry * dec[..., None, None] + st, carry

    init = jnp.zeros((bsz, SSD_GROUPS, hpg, SSD_HEAD_DIM, SSD_STATE), jnp.float32)
    _, prev = lax.scan(step, init, (states.swapaxes(0, 1), chunk_decay.swapaxes(0, 1)))
    prev = prev.swapaxes(0, 1)
    y_off = jnp.einsum('bclgn,bcghpn,bcghl->bclghp', cm, prev, jnp.exp(a_cs))
    y = (y_diag + y_off).reshape(bsz, seq, SSD_HEADS, SSD_HEAD_DIM)
    return y.astype(out_dtype)


def gated_rms_norm(y, z, w):
    g = (y * jax.nn.silu(z)).astype(jnp.float32)
    shp = g.shape
    g = g.reshape(*shp[:-1], SSD_GROUPS, shp[-1] // SSD_GROUPS)
    g = g * lax.rsqrt(jnp.mean(g * g, -1, keepdims=True) + EPS)
    return (g.reshape(shp) * w.astype(jnp.float32)).astype(y.dtype)


def hybrid_mixer(h, w_in, w_out, gm_ln_w, gm_ln_b, gm_ws, gm_bs,
                 conv_w, conv_b, dt_bias, a_log, d_skip, ssd_norm_w):
    bsz, seq, _ = h.shape
    proj = h @ w_in
    c1 = GM_WIDTH
    c2 = 2 * GM_WIDTH
    c3 = c2 + SSD_WIDTH
    c4 = c3 + SSD_CONV_DIM
    u, v, z, xbc, dt = jnp.split(proj, [c1, c2, c3, c4], axis=-1)
    y_gm = gmlp_spatial_gating(u, v, gm_ln_w, gm_ln_b, gm_ws, gm_bs)
    xbc = jax.nn.silu(causal_depthwise_conv(xbc, conv_w, conv_b))
    xs, bm, cm = jnp.split(xbc, [SSD_WIDTH, SSD_WIDTH + SSD_GROUPS * SSD_STATE], axis=-1)
    dt = jax.nn.softplus((dt + dt_bias).astype(jnp.float32))
    a = -jnp.exp(a_log.astype(jnp.float32))
    xs = xs.reshape(bsz, seq, SSD_HEADS, SSD_HEAD_DIM)
    y = ssd_scan(xs, dt, a, bm.reshape(bsz, seq, SSD_GROUPS, SSD_STATE),
                 cm.reshape(bsz, seq, SSD_GROUPS, SSD_STATE))
    y = (y + d_skip[:, None].astype(y.dtype) * xs).reshape(bsz, seq, SSD_WIDTH)
    y_ssd = gated_rms_norm(y, z, ssd_norm_w)
    return jnp.concatenate([y_gm, y_ssd], axis=-1) @ w_out


def route(hf, router_w, router_b):
    t = hf.shape[0]
    scores = jax.nn.sigmoid((hf @ router_w).astype(jnp.float32))
    biased = scores + router_b.astype(jnp.float32)
    grp = biased.reshape(t, N_EXPERT_GROUPS, EXPERTS_PER_GROUP)
    grp_score = lax.top_k(grp, TOP_K)[0].sum(-1)
    g_idx = jnp.argmax(grp_score, -1)
    in_grp = jnp.take_along_axis(grp, g_idx[:, None, None], axis=1)[:, 0]
    _, local = lax.top_k(in_grp, TOP_K)
    e_idx = g_idx[:, None] * EXPERTS_PER_GROUP + local
    w = jnp.take_along_axis(scores, e_idx, axis=1)
    w = w / jnp.sum(w, -1, keepdims=True)
    return e_idx, w


def moe_ffn(h, router_w, router_b, w_gate, w_up, w_down):
    bsz, seq, d = h.shape
    t = bsz * seq
    hf = h.reshape(t, d)
    e_idx, gate_w = route(hf, router_w, router_b)
    n_slots = t * TOP_K
    e_flat = e_idx.reshape(-1)
    tok_flat = jnp.arange(n_slots, dtype=jnp.int32) // TOP_K
    w_flat = gate_w.reshape(-1)
    order = jnp.argsort(e_flat)
    e_sorted = e_flat[order]
    tok_sorted = tok_flat[order]
    w_sorted = w_flat[order]
    counts = jnp.zeros((N_EXPERTS,), jnp.int32).at[e_flat].add(1)
    padded = (counts + MOE_BLOCK - 1) // MOE_BLOCK * MOE_BLOCK
    start = jnp.cumsum(counts) - counts
    pend = jnp.cumsum(padded)
    pstart = pend - padded
    dest = pstart[e_sorted] + jnp.arange(n_slots, dtype=jnp.int32) - start[e_sorted]
    n_blocks = -(-n_slots // MOE_BLOCK) + N_EXPERTS
    rows = jnp.zeros((n_blocks * MOE_BLOCK, d), h.dtype).at[dest].set(hf[tok_sorted])
    block_expert = jnp.minimum(
        jnp.searchsorted(pend, jnp.arange(n_blocks, dtype=jnp.int32) * MOE_BLOCK, side='right'),
        N_EXPERTS - 1)

    def expert_block(args):
        xb, e = args
        return (jax.nn.silu(xb @ w_gate[e]) * (xb @ w_up[e])) @ w_down[e]

    out = lax.map(expert_block, (rows.reshape(n_blocks, MOE_BLOCK, d), block_expert))
    out = out.reshape(n_blocks * MOE_BLOCK, d)
    y = jnp.zeros((t, d), h.dtype).at[tok_sorted].add(out[dest] * w_sorted[:, None].astype(h.dtype))
    return y.reshape(bsz, seq, d)


def setup_inputs(seed: int = 0) -> dict:
    key = jax.random.key(seed)
    ks = jax.random.split(key, 24)
    f = jnp.float32
    nrm = lambda k, shp, s: jax.random.normal(k, shp, f) * s
    dt0 = jnp.exp(jax.random.uniform(ks[12], (DEPTH, SSD_HEADS), f, np.log(1e-3), np.log(1e-1)))
    return {
        'x': nrm(ks[0], (BATCH, SEQ, D_MODEL), 1.0),
        'c': nrm(ks[1], (BATCH, D_MODEL), 1.0),
        'ada_w': nrm(ks[2], (DEPTH, D_MODEL, 6 * D_MODEL), 0.5 * D_MODEL ** -0.5),
        'ada_b': nrm(ks[3], (DEPTH, 6 * D_MODEL), 0.02),
        'norm1_w': 1.0 + nrm(ks[4], (DEPTH, D_MODEL), 0.02),
        'w_in': nrm(ks[5], (DEPTH, D_MODEL, IN_PROJ_DIM), D_MODEL ** -0.5),
        'gm_ln_w': 1.0 + nrm(ks[6], (DEPTH, GM_WIDTH), 0.02),
        'gm_ln_b': nrm(ks[7], (DEPTH, GM_WIDTH), 0.02),
        'gm_ws': nrm(ks[8], (DEPTH, GM_HEADS, GM_CHUNK, GM_CHUNK), 0.5 * GM_CHUNK ** -0.5),
        'gm_bs': 1.0 + nrm(ks[9], (DEPTH, GM_HEADS, GM_CHUNK), 0.1),
        'conv_w': nrm(ks[10], (DEPTH, SSD_CONV, SSD_CONV_DIM), SSD_CONV ** -0.5),
        'conv_b': nrm(ks[11], (DEPTH, SSD_CONV_DIM), 0.02),
        'dt_bias': dt0 + jnp.log(-jnp.expm1(-dt0)),
        'a_log': jnp.log(jax.random.uniform(ks[13], (DEPTH, SSD_HEADS), f, 1.0, 16.0)),
        'd_skip': 1.0 + nrm(ks[14], (DEPTH, SSD_HEADS), 0.1),
        'ssd_norm_w': 1.0 + nrm(ks[15], (DEPTH, SSD_WIDTH), 0.02),
        'w_out': nrm(ks[16], (DEPTH, D_MIX, D_MODEL), D_MIX ** -0.5),
        'norm2_w': 1.0 + nrm(ks[17], (DEPTH, D_MODEL), 0.02),
        'router_w': nrm(ks[18], (D_MODEL, N_EXPERTS), D_MODEL ** -0.5),
        'router_b': nrm(ks[19], (N_EXPERTS,), 0.01),
        'exp_w_gate': nrm(ks[20], (DEPTH, N_EXPERTS, D_MODEL, D_EXPERT), D_MODEL ** -0.5),
        'exp_w_up': nrm(ks[21], (DEPTH, N_EXPERTS, D_MODEL, D_EXPERT), D_MODEL ** -0.5),
        'exp_w_down': nrm(ks[22], (DEPTH, N_EXPERTS, D_EXPERT, D_MODEL), D_EXPERT ** -0.5),
        'final_norm_w': 1.0 + nrm(ks[23], (D_MODEL,), 0.02),
    }


def reference(x, c, ada_w, ada_b, norm1_w, w_in, gm_ln_w, gm_ln_b, gm_ws, gm_bs,
              conv_w, conv_b, dt_bias, a_log, d_skip, ssd_norm_w, w_out, norm2_w,
              router_w, router_b, exp_w_gate, exp_w_up, exp_w_down, final_norm_w):
    sc = jax.nn.silu(c)
    for l in range(DEPTH):
        ada = sc @ ada_w[l] + ada_b[l]
        sh1, s1, g1, sh2, s2, g2 = [a[:, None, :] for a in jnp.split(ada, 6, axis=-1)]
        h = rms_norm(x, norm1_w[l]) * (1 + s1) + sh1
        mix = hybrid_mixer(h, w_in[l], w_out[l], gm_ln_w[l], gm_ln_b[l], gm_ws[l], gm_bs[l],
                           conv_w[l], conv_b[l], dt_bias[l], a_log[l], d_skip[l], ssd_norm_w[l])
        x = x + g1 * mix
        h = rms_norm(x, norm2_w[l]) * (1 + s2) + sh2
        x = x + g2 * moe_ffn(h, router_w, router_b, exp_w_gate[l], exp_w_up[l], exp_w_down[l])
    return rms_norm(x, final_norm_w)
```

```python
import functools

import jax
import jax.numpy as jnp
from jax import lax
from jax.experimental import pallas as pl
from jax.experimental.pallas import tpu as pltpu

F32 = jnp.float32
BF16 = jnp.bfloat16
I32 = jnp.int32

EPS = 1e-6
LANES = 128
CHUNK = 128
GM_HEADS = 8
GM_WIDTH = 1024
SSD_WIDTH = 1024
SSD_HEADS = 16
SSD_GROUPS = 2
SSD_STATE = 128
SSD_CONV = 4
CONV_DIM = SSD_WIDTH + 2 * SSD_GROUPS * SSD_STATE
MAIN_PROJ = 2 * GM_WIDTH + SSD_WIDTH + CONV_DIM
N_EXPERTS = 64
EXPERTS_PER_GROUP = 8
N_EXPERT_GROUPS = 8
MOE_BLOCK = 128
HALO = 8
VMEM_LIMIT = 56 * 1024 * 1024


def _cparams(sem, vmem=VMEM_LIMIT):
    return pltpu.CompilerParams(dimension_semantics=sem, vmem_limit_bytes=vmem)


def _silu(x):
    return x * jax.nn.sigmoid(x)


def _gelu(x):
    return 0.5 * x * (1.0 + lax.erf(x * 0.7071067811865476))


def _softplus(x):
    return jnp.maximum(x, 0.0) + jnp.log1p(jnp.exp(-jnp.abs(x)))


def _ada_kernel(c_ref, w_ref, b_ref, o_ref):
    sc = _silu(c_ref[...])
    o_ref[0] = jnp.dot(sc.astype(BF16), w_ref[0].astype(BF16), preferred_element_type=F32) + b_ref[0]


def _ada(c_pad, ada_w, ada_b):
    n_layers, d, n = ada_w.shape
    tn = 1024
    return pl.pallas_call(
        _ada_kernel,
        grid=(n_layers, n // tn),
        in_specs=[
            pl.BlockSpec((8, d), lambda l, j: (0, 0)),
            pl.BlockSpec((1, d, tn), lambda l, j: (l, 0, j)),
            pl.BlockSpec((1, 1, tn), lambda l, j: (l, 0, j)),
        ],
        out_specs=pl.BlockSpec((1, 8, tn), lambda l, j: (l, 0, j)),
        out_shape=jax.ShapeDtypeStruct((n_layers, 8, n), F32),
        compiler_params=_cparams(("parallel", "parallel")),
        name="ada",
    )(c_pad, ada_w, ada_b.reshape(n_layers, 1, n))


def _inproj_kernel(x_ref, nw_ref, s_ref, sh_ref, w_ref, wdt_ref, o_ref, dt_ref, h_scr):
    @pl.when(pl.program_id(1) == 0)
    def _():
        x = x_ref[...]
        ms = jnp.mean(x * x, axis=-1, keepdims=True)
        y = x * lax.rsqrt(ms + EPS) * nw_ref[...]
        h = (y * (1.0 + s_ref[...]) + sh_ref[...]).astype(BF16)
        h_scr[...] = h
        dt_ref[...] = jnp.dot(h, wdt_ref[...], preferred_element_type=F32)

    o_ref[...] = jnp.dot(h_scr[...], w_ref[...], preferred_element_type=F32)


def _inproj(xf, norm_w, mod, w_main, w_dt, seq):
    t, d = xf.shape
    n = w_main.shape[1]
    tm = min(1024, seq)
    tn = 512
    per_batch = seq // tm
    return pl.pallas_call(
        _inproj_kernel,
        grid=(t // tm, n // tn),
        in_specs=[
            pl.BlockSpec((tm, d), lambda i, j: (i, 0)),
            pl.BlockSpec((1, d), lambda i, j: (0, 0)),
            pl.BlockSpec((None, None, 1, d), lambda i, j: (i // per_batch, 1, 0, 0)),
            pl.BlockSpec((None, None, 1, d), lambda i, j: (i // per_batch, 0, 0, 0)),
            pl.BlockSpec((d, tn), lambda i, j: (0, j)),
            pl.BlockSpec((d, LANES), lambda i, j: (0, 0)),
        ],
        out_specs=[
            pl.BlockSpec((tm, tn), lambda i, j: (i, j)),
            pl.BlockSpec((tm, LANES), lambda i, j: (i, 0)),
        ],
        out_shape=[
            jax.ShapeDtypeStruct((t, n), F32),
            jax.ShapeDtypeStruct((t, LANES), F32),
        ],
        scratch_shapes=[pltpu.VMEM((tm, d), BF16)],
        compiler_params=_cparams(("parallel", "arbitrary")),
        name="inproj",
    )(xf, norm_w.reshape(1, d), mod, mod, w_main, w_dt)


def _mixer_kernel(u_ref, v_ref, z_ref, xbc_ref, dt_ref,
                  lnw_ref, lnb_ref, ws_ref, bst_ref, cw_ref, cb_ref, dtb_ref, alog_ref,
                  dsk_ref, nw_ref, y_ref, buf_scr, xa_scr, state_scr, ys_scr):
    @pl.when(pl.program_id(1) == 0)
    def _():
        buf_scr[0:HALO, :] = jnp.zeros((HALO, CONV_DIM), F32)
        state_scr[...] = jnp.zeros(state_scr.shape, F32)

    row = lax.broadcasted_iota(I32, (CHUNK, CHUNK), 0)
    col = lax.broadcasted_iota(I32, (CHUNK, CHUNK), 1)
    tril = row >= col
    lane_lo = col < (LANES // 2)
    lane_lo_row = lane_lo[0:1, :]

    for h in range(GM_HEADS):
        sl = slice(h * LANES, (h + 1) * LANES)
        gu = _gelu(u_ref[:, sl])
        gv = _gelu(v_ref[:, sl])
        mu = jnp.mean(gv, axis=-1, keepdims=True)
        dv = gv - mu
        var = jnp.mean(dv * dv, axis=-1, keepdims=True)
        vn = dv * lax.rsqrt(var + EPS) * lnw_ref[:, sl] + lnb_ref[:, sl]
        w = jnp.where(tril, ws_ref[h], 0.0).astype(BF16)
        s = jnp.dot(w, vn.astype(BF16), preferred_element_type=F32) + bst_ref[:, h:h + 1]
        y_ref[:, sl] = (gu * s).astype(y_ref.dtype)

    buf_scr[HALO:HALO + CHUNK, :] = xbc_ref[...]
    for cb in range(CONV_DIM // 256):
        cs_ = slice(cb * 256, (cb + 1) * 256)
        acc = cb_ref[:, cs_] + cw_ref[0:1, cs_] * buf_scr[HALO - 3:HALO - 3 + CHUNK, cs_]
        for k in range(1, SSD_CONV):
            acc = acc + cw_ref[k:k + 1, cs_] * buf_scr[HALO - 3 + k:HALO - 3 + k + CHUNK, cs_]
        xa_scr[:, cs_] = _silu(acc)
    buf_scr[0:HALO, :] = buf_scr[CHUNK:CHUNK + HALO, :]

    dt = _softplus(dt_ref[...] + dtb_ref[...])
    a = -jnp.exp(alog_ref[...])
    ad = dt * a
    cs = jnp.dot(tril.astype(F32), ad, precision=lax.Precision.HIGHEST, preferred_element_type=F32)
    cs_t = cs.T
    last = cs[CHUNK - 1:CHUNK, :]
    ds = jnp.exp(last - cs)
    ecs = jnp.exp(cs)
    cd = jnp.exp(last)

    def pair_expand(m, p):
        return jnp.where(lane_lo[0:m.shape[0], :], m[:, 2 * p:2 * p + 1], m[:, 2 * p + 1:2 * p + 2])

    pairs_per_group = SSD_HEADS // SSD_GROUPS // 2
    gw = SSD_WIDTH // SSD_GROUPS
    for g in range(SSD_GROUPS):
        bm_g = xa_scr[:, SSD_WIDTH + g * SSD_STATE:SSD_WIDTH + (g + 1) * SSD_STATE]
        cm_g = xa_scr[:, SSD_WIDTH + (SSD_GROUPS + g) * SSD_STATE:SSD_WIDTH + (SSD_GROUPS + g + 1) * SSD_STATE]
        cmb = cm_g.astype(BF16)
        bmb = bm_g.astype(BF16)
        cbm = lax.dot_general(cmb, bmb, (((1,), (1,)), ((), ())), preferred_element_type=F32)
        bm_t = bm_g.T.astype(BF16)
        st_prev = state_scr[:, g * gw:(g + 1) * gw]
        y_off = jnp.dot(cmb, st_prev.astype(BF16), preferred_element_type=F32)
        xdds = []
        cds = []
        for q in range(pairs_per_group):
            p = g * pairs_per_group + q
            sl = slice(p * LANES, (p + 1) * LANES)
            xs_p = xa_scr[:, sl]
            xd = xs_p * pair_expand(dt, p)
            xdb = xd.astype(BF16)
            ys = []
            for hh in (2 * p, 2 * p + 1):
                diff = cs[:, hh:hh + 1] - cs_t[hh:hh + 1, :]
                lm = jnp.where(tril, jnp.exp(jnp.where(tril, diff, 0.0)), 0.0)
                wmat = (cbm * lm).astype(BF16)
                ys.append(jnp.dot(wmat, xdb, preferred_element_type=F32))
            y_diag = jnp.where(lane_lo, ys[0], ys[1])
            y = y_diag + y_off[:, q * LANES:(q + 1) * LANES] * pair_expand(ecs, p)
            ys_scr[:, sl] = y + dsk_ref[:, sl] * xs_p
            xdds.append((xd * pair_expand(ds, p)).astype(BF16))
            cds.append(jnp.where(lane_lo_row, cd[:, 2 * p:2 * p + 1], cd[:, 2 * p + 1:2 * p + 2]))
        st_new = jnp.dot(bm_t, jnp.concatenate(xdds, axis=1), preferred_element_type=F32)
        state_scr[:, g * gw:(g + 1) * gw] = st_prev * jnp.concatenate(cds, axis=1) + st_new

    for g in range(SSD_GROUPS):
        sl = slice(g * gw, (g + 1) * gw)
        gg = ys_scr[:, sl] * _silu(z_ref[:, sl])
        ms = jnp.mean(gg * gg, axis=-1, keepdims=True)
        y_ref[:, GM_WIDTH + g * gw:GM_WIDTH + (g + 1) * gw] = (
            gg * lax.rsqrt(ms + EPS) * nw_ref[:, sl]).astype(y_ref.dtype)


def _mixer(proj, dt_raw, p, batch, seq):
    t = proj.shape[0]
    nc = seq // CHUNK
    rows = lambda b, c: b * nc + c
    full = lambda shape: pl.BlockSpec(shape, lambda b, c: (0,) * len(shape))
    return pl.pallas_call(
        _mixer_kernel,
        grid=(batch, nc),
        in_specs=[
            pl.BlockSpec((CHUNK, GM_WIDTH), lambda b, c: (rows(b, c), 0)),
            pl.BlockSpec((CHUNK, GM_WIDTH), lambda b, c: (rows(b, c), 1)),
            pl.BlockSpec((CHUNK, SSD_WIDTH), lambda b, c: (rows(b, c), 2)),
            pl.BlockSpec((CHUNK, CONV_DIM), lambda b, c: (rows(b, c), 2)),
            pl.BlockSpec((CHUNK, LANES), lambda b, c: (rows(b, c), 0)),
            full((1, GM_WIDTH)), full((1, GM_WIDTH)),
            full((GM_HEADS, CHUNK, CHUNK)), full((CHUNK, GM_HEADS)),
            full((SSD_CONV, CONV_DIM)), full((1, CONV_DIM)),
            full((1, LANES)), full((1, LANES)),
            full((1, SSD_WIDTH)), full((1, SSD_WIDTH)),
        ],
        out_specs=pl.BlockSpec((CHUNK, GM_WIDTH + SSD_WIDTH), lambda b, c: (rows(b, c), 0)),
        out_shape=jax.ShapeDtypeStruct((t, GM_WIDTH + SSD_WIDTH), BF16),
        scratch_shapes=[
            pltpu.VMEM((HALO + CHUNK, CONV_DIM), F32),
            pltpu.VMEM((CHUNK, CONV_DIM), F32),
            pltpu.VMEM((SSD_STATE, SSD_WIDTH), F32),
            pltpu.VMEM((CHUNK, SSD_WIDTH), F32),
        ],
        compiler_params=_cparams(("parallel", "arbitrary")),
        name="mixer",
    )(proj, proj, proj, proj, dt_raw,
      p["lnw"], p["lnb"], p["ws"], p["bst"], p["cw"], p["cb"], p["dtb"], p["alog"], p["dsk"], p["nw"])


def _first_max(vals, axis_iota, n):
    m = jnp.max(vals, axis=0, keepdims=True)
    idx = jnp.min(jnp.where(vals == m, axis_iota, n), axis=0, keepdims=True)
    return m, idx


def _post_kernel(y_ref, wout_ref, x_ref, g1_ref, n2w_ref, s2_ref, sh2_ref, rwt_ref, rb_ref,
                 xo_ref, h2_ref, eidx_ref, gate_ref, rank_ref, cnt_ref, carry_scr):
    @pl.when(pl.program_id(0) == 0)
    def _():
        carry_scr[...] = jnp.zeros(carry_scr.shape, F32)

    tm = x_ref.shape[0]
    mix = jnp.dot(y_ref[...], wout_ref[...], preferred_element_type=F32)
    x = x_ref[...] + g1_ref[...] * mix
    xo_ref[...] = x
    ms = jnp.mean(x * x, axis=-1, keepdims=True)
    h = x * lax.rsqrt(ms + EPS) * n2w_ref[...] * (1.0 + s2_ref[...]) + sh2_ref[...]
    h2_ref[...] = h

    logits_t = lax.dot_general(rwt_ref[...], h.astype(BF16), (((1,), (1,)), ((), ())),
                               preferred_element_type=F32)
    scores = jax.nn.sigmoid(logits_t)
    biased = scores + rb_ref[...]

    sub = lax.broadcasted_iota(I32, (EXPERTS_PER_GROUP, tm), 0)
    neg = jnp.float32(-jnp.inf)
    best = None
    for g in range(N_EXPERT_GROUPS):
        grp = biased[g * EXPERTS_PER_GROUP:(g + 1) * EXPERTS_PER_GROUP, :]
        m1, i1 = _first_max(grp, sub, EXPERTS_PER_GROUP)
        m2, i2 = _first_max(jnp.where(sub == i1, neg, grp), sub, EXPERTS_PER_GROUP)
        gs = m1 + m2
        if best is None:
            best, bi, l1, l2 = gs, jnp.zeros((1, tm), I32), i1, i2
        else:
            upd = gs > best
            best = jnp.where(upd, gs, best)
            bi = jnp.where(upd, g, bi)
            l1 = jnp.where(upd, i1, l1)
            l2 = jnp.where(upd, i2, l2)
    e0 = bi * EXPERTS_PER_GROUP + l1
    e1 = bi * EXPERTS_PER_GROUP + l2

    eio = lax.broadcasted_iota(I32, (N_EXPERTS, tm), 0)
    oh0 = eio == e0
    oh1 = eio == e1
    s0 = jnp.sum(jnp.where(oh0, scores, 0.0), axis=0, keepdims=True)
    s1 = jnp.sum(jnp.where(oh1, scores, 0.0), axis=0, keepdims=True)
    tot = s0 + s1
    eidx_ref[0:1, :] = e0
    eidx_ref[1:2, :] = e1
    gate_ref[0:1, :] = s0 / tot
    gate_ref[1:2, :] = s1 / tot

    ohs = oh0.astype(F32) + oh1.astype(F32)
    tr = lax.broadcasted_iota(I32, (tm, tm), 0)
    tc = lax.broadcasted_iota(I32, (tm, tm), 1)
    before = (tr < tc).astype(BF16)
    prefix = jnp.dot(ohs.astype(BF16), before, preferred_element_type=F32)
    base = carry_scr[:, 0:1] + prefix
    rank_ref[0:1, :] = jnp.sum(jnp.where(oh0, base, 0.0), axis=0, keepdims=True).astype(I32)
    rank_ref[1:2, :] = jnp.sum(jnp.where(oh1, base, 0.0), axis=0, keepdims=True).astype(I32)
    carry_scr[...] = carry_scr[...] + jnp.sum(ohs, axis=1, keepdims=True)
    cnt_ref[...] = carry_scr[...]


def _post(y_mix, w_out, xf, mod, norm2_w, rw_t, rb, seq):
    t, d = xf.shape
    dm = y_mix.shape[1]
    tm = min(256, seq)
    per_batch = seq // tm
    modspec = lambda k: pl.BlockSpec((None, None, 1, d), lambda i: (i // per_batch, k, 0, 0))
    tok = pl.BlockSpec((2, tm), lambda i: (0, i))
    return pl.pallas_call(
        _post_kernel,
        grid=(t // tm,),
        in_specs=[
            pl.BlockSpec((tm, dm), lambda i: (i, 0)),
            pl.BlockSpec((dm, d), lambda i: (0, 0)),
            pl.BlockSpec((tm, d), lambda i: (i, 0)),
            modspec(2),
            pl.BlockSpec((1, d), lambda i: (0, 0)),
            modspec(4),
            modspec(3),
            pl.BlockSpec((N_EXPERTS, d), lambda i: (0, 0)),
            pl.BlockSpec((N_EXPERTS, 1), lambda i: (0, 0)),
        ],
        out_specs=[
            pl.BlockSpec((tm, d), lambda i: (i, 0)),
            pl.BlockSpec((tm, d), lambda i: (i, 0)),
            tok, tok, tok,
            pl.BlockSpec((N_EXPERTS, LANES), lambda i: (0, 0)),
        ],
        out_shape=[
            jax.ShapeDtypeStruct((t, d), F32),
            jax.ShapeDtypeStruct((t, d), F32),
            jax.ShapeDtypeStruct((2, t), I32),
            jax.ShapeDtypeStruct((2, t), F32),
            jax.ShapeDtypeStruct((2, t), I32),
            jax.ShapeDtypeStruct((N_EXPERTS, LANES), F32),
        ],
        scratch_shapes=[pltpu.VMEM((N_EXPERTS, LANES), F32)],
        compiler_params=_cparams(("arbitrary",)),
        name="post",
    )(y_mix, w_out, xf, mod, norm2_w.reshape(1, d), mod, mod, rw_t, rb.reshape(N_EXPERTS, 1))


META_ROWS = 8
META_LANES = 256
ROW_BLK_E, ROW_CNT, ROW_PSTART, ROW_NUSED = 0, 1, 2, 3


def _col_to_row(colv):
    n = colv.shape[0]
    r = lax.broadcasted_iota(I32, (n, n), 0)
    c = lax.broadcasted_iota(I32, (n, n), 1)
    return jnp.sum(jnp.where(r == c, colv, 0.0), axis=0, keepdims=True)


def _meta_kernel(cnt_ref, eidx_ref, rank_ref, dest_ref, meta_ref):
    t = eidx_ref.shape[1]
    cnt = cnt_ref[...]
    nblk = jnp.floor((cnt + (MOE_BLOCK - 1)) * (1.0 / MOE_BLOCK))
    r = lax.broadcasted_iota(I32, (N_EXPERTS, N_EXPERTS), 0)
    c = lax.broadcasted_iota(I32, (N_EXPERTS, N_EXPERTS), 1)
    lower = (c < r).astype(BF16)
    pstart = jnp.dot(lower, nblk.astype(BF16), preferred_element_type=F32)
    pend = pstart + nblk

    chunk = min(1024, t)
    for j in range(t // chunk):
        sl = slice(j * chunk, (j + 1) * chunk)
        eio = lax.broadcasted_iota(I32, (N_EXPERTS, chunk), 0)
        for k in range(2):
            oh = eio == eidx_ref[k:k + 1, sl]
            ps = jnp.sum(jnp.where(oh, pstart[:, 0:1], 0.0), axis=0, keepdims=True)
            dest_ref[k:k + 1, sl] = (ps * MOE_BLOCK).astype(I32) + rank_ref[k:k + 1, sl]

    bl = lax.broadcasted_iota(I32, (N_EXPERTS, META_LANES), 1).astype(F32)
    raw = jnp.sum((pend[:, 0:1] <= bl).astype(F32), axis=0, keepdims=True)
    raw = jnp.minimum(raw, N_EXPERTS - 1.0)
    nused = pend[N_EXPERTS - 1:N_EXPERTS, 0:1]
    used = bl[0:1, :] < nused
    last_e = jnp.max(jnp.where(used, raw, 0.0), axis=1, keepdims=True)
    meta_ref[...] = jnp.zeros(meta_ref.shape, I32)
    meta_ref[ROW_BLK_E:ROW_BLK_E + 1, :] = jnp.where(used, raw, last_e).astype(I32)
    meta_ref[ROW_CNT:ROW_CNT + 1, 0:N_EXPERTS] = _col_to_row(cnt[:, 0:1]).astype(I32)
    meta_ref[ROW_PSTART:ROW_PSTART + 1, 0:N_EXPERTS] = (_col_to_row(pstart[:, 0:1]) * MOE_BLOCK).astype(I32)
    meta_ref[ROW_NUSED:ROW_NUSED + 1, :] = jnp.broadcast_to(nused, (1, META_LANES)).astype(I32)


def _meta(cnt, eidx, rank):
    t = eidx.shape[1]
    full = lambda shape: pl.BlockSpec(shape, lambda: (0,) * len(shape))
    return pl.pallas_call(
        _meta_kernel,
        in_specs=[full((N_EXPERTS, LANES)), full((2, t)), full((2, t))],
        out_specs=[full((2, t)), full((META_ROWS, META_LANES))],
        out_shape=[jax.ShapeDtypeStruct((2, t), I32), jax.ShapeDtypeStruct((META_ROWS, META_LANES), I32)],
        name="meta",
    )(cnt, eidx, rank)


SUBLANES = 8
PAD_RUNS = (64, 32, 16, 8)


def _scatter_kernel(cnt_sm, ps_sm, dest_ref, h2_ref, rows_ref, zbuf, sem, zsem):
    tm = h2_ref.shape[0]

    def row_copy(tok, k):
        return pltpu.make_async_copy(h2_ref.at[pl.ds(tok, 1)], rows_ref.at[pl.ds(dest_ref[k, tok], 1)], sem)

    def issue(tok, carry):
        row_copy(tok, 0).start()
        row_copy(tok, 1).start()
        return carry

    def drain(tok, carry):
        row_copy(tok, 0).wait()
        row_copy(tok, 1).wait()
        return carry

    lax.fori_loop(0, tm, issue, 0)
    lax.fori_loop(0, tm, drain, 0)

    @pl.when(pl.program_id(0) == pl.num_programs(0) - 1)
    def _():
        zbuf[...] = jnp.zeros(zbuf.shape, zbuf.dtype)

        def pad_copies(e, do_start):
            def go(cp):
                if do_start:
                    cp.start()
                else:
                    cp.wait()

            cnt = cnt_sm[e]
            npad = (-cnt) & (MOE_BLOCK - 1)
            off = ps_sm[e] + cnt
            nhead = npad & (SUBLANES - 1)
            for r in range(SUBLANES - 1):
                @pl.when(r < nhead)
                def _():
                    go(pltpu.make_async_copy(zbuf.at[pl.ds(0, 1)], rows_ref.at[pl.ds(off + r, 1)], zsem))
            off = off + nhead
            for bit in PAD_RUNS:
                @pl.when((npad & bit) != 0)
                def _():
                    go(pltpu.make_async_copy(zbuf.at[pl.ds(0, bit)],
                                             rows_ref.at[pl.ds(pl.multiple_of(off, SUBLANES), bit)], zsem))
                off = off + (npad & bit)

        def start_e(e, carry):
            pad_copies(e, True)
            return carry

        def wait_e(e, carry):
            pad_copies(e, False)
            return carry

        lax.fori_loop(0, N_EXPERTS, start_e, 0)
        lax.fori_loop(0, N_EXPERTS, wait_e, 0)

        last = N_EXPERTS - 1
        used_rows = ps_sm[last] + ((cnt_sm[last] + (MOE_BLOCK - 1)) & -MOE_BLOCK)
        half = MOE_BLOCK // 2

        def tail_copy(j):
            return pltpu.make_async_copy(
                zbuf, rows_ref.at[pl.ds(pl.multiple_of(used_rows + j * half, SUBLANES), half)], zsem)

        n_tail = (rows_ref.shape[0] - used_rows) // half
        lax.fori_loop(0, n_tail, lambda j, c: (tail_copy(j).start(), c)[1], 0)
        lax.fori_loop(0, n_tail, lambda j, c: (tail_copy(j).wait(), c)[1], 0)


def _scatter(cnt_row, ps_row, dest, h2, n_rows):
    t, d = h2.shape
    tm = min(256, t)
    return pl.pallas_call(
        _scatter_kernel,
        grid_spec=pltpu.PrefetchScalarGridSpec(
            num_scalar_prefetch=2,
            grid=(t // tm,),
            in_specs=[
                pl.BlockSpec((2, tm), lambda i, *_: (0, i), memory_space=pltpu.SMEM),
                pl.BlockSpec((tm, d), lambda i, *_: (i, 0)),
            ],
            out_specs=pl.BlockSpec(memory_space=pl.ANY),
            scratch_shapes=[
                pltpu.VMEM((MOE_BLOCK // 2, d), h2.dtype),
                pltpu.SemaphoreType.DMA(()),
                pltpu.SemaphoreType.DMA(()),
            ],
        ),
        out_shape=jax.ShapeDtypeStruct((n_rows, d), h2.dtype),
        compiler_params=_cparams(("arbitrary",)),
        name="scatter",
    )(cnt_row, ps_row, dest, h2)


def _expert_kernel(be_sm, nu_sm, rows_ref, wg_ref, wu_ref, wd_ref, out_ref, wg_scr, wu_scr, wd_scr):
    b = pl.program_id(0)

    @pl.when(b < nu_sm[0])
    def _():
        prev = be_sm[jnp.maximum(b - 1, 0)]

        @pl.when((b == 0) | (be_sm[b] != prev))
        def _():
            wg_scr[...] = wg_ref[0].astype(BF16)
            wu_scr[...] = wu_ref[0].astype(BF16)
            wd_scr[...] = wd_ref[0].astype(BF16)

        xb = rows_ref[...].astype(BF16)
        gate = jnp.dot(xb, wg_scr[...], preferred_element_type=F32)
        up = jnp.dot(xb, wu_scr[...], preferred_element_type=F32)
        act = (_silu(gate) * up).astype(BF16)
        out_ref[...] = jnp.dot(act, wd_scr[...], preferred_element_type=F32)

    @pl.when(b >= nu_sm[0])
    def _():
        out_ref[...] = jnp.zeros(out_ref.shape, out_ref.dtype)


def _experts(blk_e, nused, rows, w_gate, w_up, w_down):
    n_rows, d = rows.shape
    de = w_gate.shape[2]
    n_blocks = n_rows // MOE_BLOCK
    blk = lambda b, be, nu: (jnp.minimum(b, nu[0] - 1), 0)
    return pl.pallas_call(
        _expert_kernel,
        grid_spec=pltpu.PrefetchScalarGridSpec(
            num_scalar_prefetch=2,
            grid=(n_blocks,),
            in_specs=[
                pl.BlockSpec((MOE_BLOCK, d), blk),
                pl.BlockSpec((1, d, de), lambda b, be, nu: (be[b], 0, 0)),
                pl.BlockSpec((1, d, de), lambda b, be, nu: (be[b], 0, 0)),
                pl.BlockSpec((1, de, d), lambda b, be, nu: (be[b], 0, 0)),
            ],
            out_specs=pl.BlockSpec((MOE_BLOCK, d), lambda b, be, nu: (b, 0)),
            scratch_shapes=[
                pltpu.VMEM((d, de), BF16),
                pltpu.VMEM((d, de), BF16),
                pltpu.VMEM((de, d), BF16),
            ],
        ),
        out_shape=jax.ShapeDtypeStruct((n_rows, d), F32),
        compiler_params=_cparams(("arbitrary",)),
        name="experts",
    )(blk_e, nused, rows, w_gate, w_up, w_down)


def _combine_kernel(dest_ref, gate_ref, x_ref, g2_ref, fw_ref, os_ref, out_ref, gbuf, sem, *, final):
    tm = x_ref.shape[0]

    def row_copy(tok, k):
        return pltpu.make_async_copy(os_ref.at[pl.ds(dest_ref[k, tok], 1)], gbuf.at[k, pl.ds(tok, 1)], sem)

    def issue(tok, carry):
        row_copy(tok, 0).start()
        row_copy(tok, 1).start()
        return carry

    def drain(tok, carry):
        row_copy(tok, 0).wait()
        row_copy(tok, 1).wait()
        return carry

    lax.fori_loop(0, tm, issue, 0)
    lax.fori_loop(0, tm, drain, 0)

    r = lax.broadcasted_iota(I32, (tm, tm), 0)
    c = lax.broadcasted_iota(I32, (tm, tm), 1)
    eye = r == c
    w0 = jnp.sum(jnp.where(eye, gate_ref[0:1, :], 0.0), axis=1, keepdims=True)
    w1 = jnp.sum(jnp.where(eye, gate_ref[1:2, :], 0.0), axis=1, keepdims=True)
    y = gbuf[0] * w0 + gbuf[1] * w1
    x = x_ref[...] + g2_ref[...] * y
    if final:
        ms = jnp.mean(x * x, axis=-1, keepdims=True)
        x = x * lax.rsqrt(ms + EPS) * fw_ref[...]
    out_ref[...] = x


def _combine(dest, gate, xf, mod, final_w, out_sorted, seq, final):
    t, d = xf.shape
    tm = min(256, seq)
    per_batch = seq // tm
    return pl.pallas_call(
        functools.partial(_combine_kernel, final=final),
        grid=(t // tm,),
        in_specs=[
            pl.BlockSpec((2, tm), lambda i: (0, i), memory_space=pltpu.SMEM),
            pl.BlockSpec((2, tm), lambda i: (0, i)),
            pl.BlockSpec((tm, d), lambda i: (i, 0)),
            pl.BlockSpec((None, None, 1, d), lambda i: (i // per_batch, 5, 0, 0)),
            pl.BlockSpec((1, d), lambda i: (0, 0)),
            pl.BlockSpec(memory_space=pl.ANY),
        ],
        out_specs=pl.BlockSpec((tm, d), lambda i: (i, 0)),
        out_shape=jax.ShapeDtypeStruct((t, d), F32),
        scratch_shapes=[pltpu.VMEM((2, tm, d), F32), pltpu.SemaphoreType.DMA(())],
        compiler_params=_cparams(("arbitrary",)),
        name="combine",
    )(dest, gate, xf, mod, final_w.reshape(1, d), out_sorted)


def _pad_lanes(v, n=LANES):
    return jnp.pad(v, (0, n - v.shape[0])).reshape(1, n)


def kernel(x, c, ada_w, ada_b, norm1_w, w_in, gm_ln_w, gm_ln_b, gm_ws, gm_bs, conv_w, conv_b, dt_bias, a_log,
           d_skip, ssd_norm_w, w_out, norm2_w, router_w, router_b, exp_w_gate, exp_w_up, exp_w_down,
           final_norm_w):
    batch, seq, d = x.shape
    t = batch * seq
    depth = ada_w.shape[0]
    assert batch <= 8 and seq % CHUNK == 0 and w_in.shape[2] == MAIN_PROJ + SSD_HEADS
    n_rows = (-(-(t * 2) // MOE_BLOCK) + N_EXPERTS) * MOE_BLOCK
    assert n_rows // MOE_BLOCK <= META_LANES

    ada = _ada(jnp.pad(c, ((0, 8 - batch), (0, 0))), ada_w, ada_b)
    rw_t = router_w.T.astype(BF16)
    xf = x.reshape(t, d)
    for l in range(depth):
        mod = ada[l, :batch].reshape(batch, 6, 1, d)
        w_main = w_in[l, :, :MAIN_PROJ].astype(BF16)
        w_dt = jnp.pad(w_in[l, :, MAIN_PROJ:], ((0, 0), (0, LANES - SSD_HEADS))).astype(BF16)
        proj, dt_raw = _inproj(xf, norm1_w[l], mod, w_main, w_dt, seq)
        mixer_params = dict(
            lnw=gm_ln_w[l].reshape(1, GM_WIDTH), lnb=gm_ln_b[l].reshape(1, GM_WIDTH),
            ws=gm_ws[l], bst=gm_bs[l].T,
            cw=conv_w[l], cb=conv_b[l].reshape(1, CONV_DIM),
            dtb=_pad_lanes(dt_bias[l]), alog=_pad_lanes(a_log[l]),
            dsk=jnp.repeat(d_skip[l], SSD_WIDTH // SSD_HEADS).reshape(1, SSD_WIDTH),
            nw=ssd_norm_w[l].reshape(1, SSD_WIDTH))
        y_mix = _mixer(proj, dt_raw, mixer_params, batch, seq)
        xf, h2, eidx, gate, rank, cnt = _post(y_mix, w_out[l].astype(BF16), xf, mod, norm2_w[l], rw_t,
                                              router_b, seq)
        dest, meta = _meta(cnt, eidx, rank)
        rows = _scatter(meta[ROW_CNT, :N_EXPERTS], meta[ROW_PSTART, :N_EXPERTS], dest, h2, n_rows)
        out_sorted = _experts(meta[ROW_BLK_E, :n_rows // MOE_BLOCK], meta[ROW_NUSED, :1], rows,
                              exp_w_gate[l], exp_w_up[l], exp_w_down[l])
        xf = _combine(dest, gate, xf, mod, final_norm_w, out_sorted, seq, final=(l == depth - 1))
    return xf.reshape(batch, seq, d)
```

```python
import functools

import jax
import jax.numpy as jnp
from jax import lax
from jax.experimental import pallas as pl
from jax.experimental.pallas import tpu as pltpu

F32 = jnp.float32
BF16 = jnp.bfloat16
I32 = jnp.int32

EPS = 1e-6
LANES = 128
CHUNK = 128
GM_HEADS = 8
GM_WIDTH = 1024
SSD_WIDTH = 1024
SSD_HEADS = 16
SSD_GROUPS = 2
SSD_STATE = 128
SSD_CONV = 4
CONV_DIM = SSD_WIDTH + 2 * SSD_GROUPS * SSD_STATE
MAIN_PROJ = 2 * GM_WIDTH + SSD_WIDTH + CONV_DIM
N_EXPERTS = 64
EXPERTS_PER_GROUP = 8
N_EXPERT_GROUPS = 8
MOE_BLOCK = 128
HALO = 8
VMEM_LIMIT = 56 * 1024 * 1024


def _cparams(sem, vmem=VMEM_LIMIT):
    return pltpu.CompilerParams(dimension_semantics=sem, vmem_limit_bytes=vmem)


def _silu(x):
    return x * jax.nn.sigmoid(x)


def _gelu(x):
    return 0.5 * x * (1.0 + lax.erf(x * 0.7071067811865476))


def _softplus(x):
    return jnp.maximum(x, 0.0) + jnp.log1p(jnp.exp(-jnp.abs(x)))


def _ada_kernel(c_ref, w_ref, b_ref, o_ref):
    sc = _silu(c_ref[...])
    o_ref[0] = jnp.dot(sc.astype(BF16), w_ref[0].astype(BF16), preferred_element_type=F32) + b_ref[0]


def _ada(c_pad, ada_w, ada_b):
    n_layers, d, n = ada_w.shape
    tn = 1024
    return pl.pallas_call(
        _ada_kernel,
        grid=(n_layers, n // tn),
        in_specs=[
            pl.BlockSpec((8, d), lambda l, j: (0, 0)),
            pl.BlockSpec((1, d, tn), lambda l, j: (l, 0, j)),
            pl.BlockSpec((1, 1, tn), lambda l, j: (l, 0, j)),
        ],
        out_specs=pl.BlockSpec((1, 8, tn), lambda l, j: (l, 0, j)),
        out_shape=jax.ShapeDtypeStruct((n_layers, 8, n), F32),
        compiler_params=_cparams(("parallel", "parallel")),
        name="ada",
    )(c_pad, ada_w, ada_b.reshape(n_layers, 1, n))


def _inproj_kernel(x_ref, nw_ref, s_ref, sh_ref, w_ref, wdt_ref, o_ref, dt_ref, h_scr):
    @pl.when(pl.program_id(1) == 0)
    def _():
        x = x_ref[...]
        ms = jnp.mean(x * x, axis=-1, keepdims=True)
        y = x * lax.rsqrt(ms + EPS) * nw_ref[...]
        h = (y * (1.0 + s_ref[...]) + sh_ref[...]).astype(BF16)
        h_scr[...] = h
        dt_ref[...] = jnp.dot(h, wdt_ref[...], preferred_element_type=F32)

    o_ref[...] = jnp.dot(h_scr[...], w_ref[...], preferred_element_type=F32)


def _inproj(xf, norm_w, mod, w_main, w_dt, seq):
    t, d = xf.shape
    n = w_main.shape[1]
    tm = min(1024, seq)
    tn = 512
    per_batch = seq // tm
    return pl.pallas_call(
        _inproj_kernel,
        grid=(t // tm, n // tn),
        in_specs=[
            pl.BlockSpec((tm, d), lambda i, j: (i, 0)),
            pl.BlockSpec((1, d), lambda i, j: (0, 0)),
            pl.BlockSpec((None, None, 1, d), lambda i, j: (i // per_batch, 1, 0, 0)),
            pl.BlockSpec((None, None, 1, d), lambda i, j: (i // per_batch, 0, 0, 0)),
            pl.BlockSpec((d, tn), lambda i, j: (0, j)),
            pl.BlockSpec((d, LANES), lambda i, j: (0, 0)),
        ],
        out_specs=[
            pl.BlockSpec((tm, tn), lambda i, j: (i, j)),
            pl.BlockSpec((tm, LANES), lambda i, j: (i, 0)),
        ],
        out_shape=[
            jax.ShapeDtypeStruct((t, n), F32),
            jax.ShapeDtypeStruct((t, LANES), F32),
        ],
        scratch_shapes=[pltpu.VMEM((tm, d), BF16)],
        compiler_params=_cparams(("parallel", "arbitrary")),
        name="inproj",
    )(xf, norm_w.reshape(1, d), mod, mod, w_main, w_dt)


def _mixer_kernel(u_ref, v_ref, z_ref, xbc_ref, dt_ref,
                  lnw_ref, lnb_ref, ws_ref, bst_ref, cw_ref, cb_ref, dtb_ref, alog_ref,
                  dsk_ref, nw_ref, y_ref, buf_scr, xa_scr, state_scr, ys_scr):
    @pl.when(pl.program_id(1) == 0)
    def _():
        buf_scr[0:HALO, :] = jnp.zeros((HALO, CONV_DIM), F32)
        state_scr[...] = jnp.zeros(state_scr.shape, F32)

    row = lax.broadcasted_iota(I32, (CHUNK, CHUNK), 0)
    col = lax.broadcasted_iota(I32, (CHUNK, CHUNK), 1)
    tril = row >= col
    lane_lo = col < (LANES // 2)
    lane_lo_row = lane_lo[0:1, :]

    for h in range(GM_HEADS):
        sl = slice(h * LANES, (h + 1) * LANES)
        gu = _gelu(u_ref[:, sl])
        gv = _gelu(v_ref[:, sl])
        mu = jnp.mean(gv, axis=-1, keepdims=True)
        dv = gv - mu
        var = jnp.mean(dv * dv, axis=-1, keepdims=True)
        vn = dv * lax.rsqrt(var + EPS) * lnw_ref[:, sl] + lnb_ref[:, sl]
        w = jnp.where(tril, ws_ref[h], 0.0).astype(BF16)
        s = jnp.dot(w, vn.astype(BF16), preferred_element_type=F32) + bst_ref[:, h:h + 1]
        y_ref[:, sl] = (gu * s).astype(y_ref.dtype)

    buf_scr[HALO:HALO + CHUNK, :] = xbc_ref[...]
    for cb in range(CONV_DIM // 256):
        cs_ = slice(cb * 256, (cb + 1) * 256)
        acc = cb_ref[:, cs_] + cw_ref[0:1, cs_] * buf_scr[HALO - 3:HALO - 3 + CHUNK, cs_]
        for k in range(1, SSD_CONV):
            acc = acc + cw_ref[k:k + 1, cs_] * buf_scr[HALO - 3 + k:HALO - 3 + k + CHUNK, cs_]
        xa_scr[:, cs_] = _silu(acc)
    buf_scr[0:HALO, :] = buf_scr[CHUNK:CHUNK + HALO, :]

    dt = _softplus(dt_ref[...] + dtb_ref[...])
    a = -jnp.exp(alog_ref[...])
    ad = dt * a
    cs = jnp.dot(tril.astype(F32), ad, precision=lax.Precision.HIGHEST, preferred_element_type=F32)
    cs_t = cs.T
    last = cs[CHUNK - 1:CHUNK, :]
    ds = jnp.exp(last - cs)
    ecs = jnp.exp(cs)
    cd = jnp.exp(last)

    def pair_expand(m, p):
        return jnp.where(lane_lo[0:m.shape[0], :], m[:, 2 * p:2 * p + 1], m[:, 2 * p + 1:2 * p + 2])

    pairs_per_group = SSD_HEADS // SSD_GROUPS // 2
    gw = SSD_WIDTH // SSD_GROUPS
    for g in range(SSD_GROUPS):
        bm_g = xa_scr[:, SSD_WIDTH + g * SSD_STATE:SSD_WIDTH + (g + 1) * SSD_STATE]
        cm_g = xa_scr[:, SSD_WIDTH + (SSD_GROUPS + g) * SSD_STATE:SSD_WIDTH + (SSD_GROUPS + g + 1) * SSD_STATE]
        cmb = cm_g.astype(BF16)
        bmb = bm_g.astype(BF16)
        cbm = lax.dot_general(cmb, bmb, (((1,), (1,)), ((), ())), preferred_element_type=F32)
        bm_t = bm_g.T.astype(BF16)
        st_prev = state_scr[:, g * gw:(g + 1) * gw]
        y_off = jnp.dot(cmb, st_prev.astype(BF16), preferred_element_type=F32)
        xdds = []
        cds = []
        for q in range(pairs_per_group):
            p = g * pairs_per_group + q
            sl = slice(p * LANES, (p + 1) * LANES)
            xs_p = xa_scr[:, sl]
            xd = xs_p * pair_expand(dt, p)
            xdb = xd.astype(BF16)
            ys = []
            for hh in (2 * p, 2 * p + 1):
                diff = cs[:, hh:hh + 1] - cs_t[hh:hh + 1, :]
                lm = jnp.where(tril, jnp.exp(jnp.where(tril, diff, 0.0)), 0.0)
                wmat = (cbm * lm).astype(BF16)
                ys.append(jnp.dot(wmat, xdb, preferred_element_type=F32))
            y_diag = jnp.where(lane_lo, ys[0], ys[1])
            y = y_diag + y_off[:, q * LANES:(q + 1) * LANES] * pair_expand(ecs, p)
            ys_scr[:, sl] = y + dsk_ref[:, sl] * xs_p
            xdds.append((xd * pair_expand(ds, p)).astype(BF16))
            cds.append(jnp.where(lane_lo_row, cd[:, 2 * p:2 * p + 1], cd[:, 2 * p + 1:2 * p + 2]))
        st_new = jnp.dot(bm_t, jnp.concatenate(xdds, axis=1), preferred_element_type=F32)
        state_scr[:, g * gw:(g + 1) * gw] = st_prev * jnp.concatenate(cds, axis=1) + st_new

    for g in range(SSD_GROUPS):
        sl = slice(g * gw, (g + 1) * gw)
        gg = ys_scr[:, sl] * _silu(z_ref[:, sl])
        ms = jnp.mean(gg * gg, axis=-1, keepdims=True)
        y_ref[:, GM_WIDTH + g * gw:GM_WIDTH + (g + 1) * gw] = (
            gg * lax.rsqrt(ms + EPS) * nw_ref[:, sl]).astype(y_ref.dtype)


def _mixer(proj, dt_raw, p, batch, seq):
    t = proj.shape[0]
    nc = seq // CHUNK
    rows = lambda b, c: b * nc + c
    full = lambda shape: pl.BlockSpec(shape, lambda b, c: (0,) * len(shape))
    return pl.pallas_call(
        _mixer_kernel,
        grid=(batch, nc),
        in_specs=[
            pl.BlockSpec((CHUNK, GM_WIDTH), lambda b, c: (rows(b, c), 0)),
            pl.BlockSpec((CHUNK, GM_WIDTH), lambda b, c: (rows(b, c), 1)),
            pl.BlockSpec((CHUNK, SSD_WIDTH), lambda b, c: (rows(b, c), 2)),
            pl.BlockSpec((CHUNK, CONV_DIM), lambda b, c: (rows(b, c), 2)),
            pl.BlockSpec((CHUNK, LANES), lambda b, c: (rows(b, c), 0)),
            full((1, GM_WIDTH)), full((1, GM_WIDTH)),
            full((GM_HEADS, CHUNK, CHUNK)), full((CHUNK, GM_HEADS)),
            full((SSD_CONV, CONV_DIM)), full((1, CONV_DIM)),
            full((1, LANES)), full((1, LANES)),
            full((1, SSD_WIDTH)), full((1, SSD_WIDTH)),
        ],
        out_specs=pl.BlockSpec((CHUNK, GM_WIDTH + SSD_WIDTH), lambda b, c: (rows(b, c), 0)),
        out_shape=jax.ShapeDtypeStruct((t, GM_WIDTH + SSD_WIDTH), BF16),
        scratch_shapes=[
            pltpu.VMEM((HALO + CHUNK, CONV_DIM), F32),
            pltpu.VMEM((CHUNK, CONV_DIM), F32),
            pltpu.VMEM((SSD_STATE, SSD_WIDTH), F32),
            pltpu.VMEM((CHUNK, SSD_WIDTH), F32),
        ],
        compiler_params=_cparams(("parallel", "arbitrary")),
        name="mixer",
    )(proj, proj, proj, proj, dt_raw,
      p["lnw"], p["lnb"], p["ws"], p["bst"], p["cw"], p["cb"], p["dtb"], p["alog"], p["dsk"], p["nw"])


def _first_max(vals, axis_iota, n):
    m = jnp.max(vals, axis=0, keepdims=True)
    idx = jnp.min(jnp.where(vals == m, axis_iota, n), axis=0, keepdims=True)
    return m, idx


def _post_kernel(y_ref, wout_ref, x_ref, g1_ref, n2w_ref, s2_ref, sh2_ref, rwt_ref, rb_ref,
                 xo_ref, h2_ref, eidx_ref, gate_ref, rank_ref, cnt_ref, carry_scr):
    @pl.when(pl.program_id(0) == 0)
    def _():
        carry_scr[...] = jnp.zeros(carry_scr.shape, F32)

    tm = x_ref.shape[0]
    mix = jnp.dot(y_ref[...], wout_ref[...], preferred_element_type=F32)
    x = x_ref[...] + g1_ref[...] * mix
    xo_ref[...] = x
    ms = jnp.mean(x * x, axis=-1, keepdims=True)
    h = x * lax.rsqrt(ms + EPS) * n2w_ref[...] * (1.0 + s2_ref[...]) + sh2_ref[...]
    h2_ref[...] = h

    logits_t = lax.dot_general(rwt_ref[...], h.astype(BF16), (((1,), (1,)), ((), ())),
                               preferred_element_type=F32)
    scores = jax.nn.sigmoid(logits_t)
    biased = scores + rb_ref[...]

    sub = lax.broadcasted_iota(I32, (EXPERTS_PER_GROUP, tm), 0)
    neg = jnp.float32(-jnp.inf)
    best = None
    for g in range(N_EXPERT_GROUPS):
        grp = biased[g * EXPERTS_PER_GROUP:(g + 1) * EXPERTS_PER_GROUP, :]
        m1, i1 = _first_max(grp, sub, EXPERTS_PER_GROUP)
        m2, i2 = _first_max(jnp.where(sub == i1, neg, grp), sub, EXPERTS_PER_GROUP)
        gs = m1 + m2
        if best is None:
            best, bi, l1, l2 = gs, jnp.zeros((1, tm), I32), i1, i2
        else:
            upd = gs > best
            best = jnp.where(upd, gs, best)
            bi = jnp.where(upd, g, bi)
            l1 = jnp.where(upd, i1, l1)
            l2 = jnp.where(upd, i2, l2)
    e0 = bi * EXPERTS_PER_GROUP + l1
    e1 = bi * EXPERTS_PER_GROUP + l2

    eio = lax.broadcasted_iota(I32, (N_EXPERTS, tm), 0)
    oh0 = eio == e0
    oh1 = eio == e1
    s0 = jnp.sum(jnp.where(oh0, scores, 0.0), axis=0, keepdims=True)
    s1 = jnp.sum(jnp.where(oh1, scores, 0.0), axis=0, keepdims=True)
    tot = s0 + s1
    eidx_ref[0:1, :] = e0
    eidx_ref[1:2, :] = e1
    gate_ref[0:1, :] = s0 / tot
    gate_ref[1:2, :] = s1 / tot

    ohs = oh0.astype(F32) + oh1.astype(F32)
    tr = lax.broadcasted_iota(I32, (tm, tm), 0)
    tc = lax.broadcasted_iota(I32, (tm, tm), 1)
    before = (tr < tc).astype(BF16)
    prefix = jnp.dot(ohs.astype(BF16), before, preferred_element_type=F32)
    base = carry_scr[:, 0:1] + prefix
    rank_ref[0:1, :] = jnp.sum(jnp.where(oh0, base, 0.0), axis=0, keepdims=True).astype(I32)
    rank_ref[1:2, :] = jnp.sum(jnp.where(oh1, base, 0.0), axis=0, keepdims=True).astype(I32)
    carry_scr[...] = carry_scr[...] + jnp.sum(ohs, axis=1, keepdims=True)
    cnt_ref[...] = carry_scr[...]


def _post(y_mix, w_out, xf, mod, norm2_w, rw_t, rb, seq):
    t, d = xf.shape
    dm = y_mix.shape[1]
    tm = min(256, seq)
    per_batch = seq // tm
    modspec = lambda k: pl.BlockSpec((None, None, 1, d), lambda i: (i // per_batch, k, 0, 0))
    tok = pl.BlockSpec((2, tm), lambda i: (0, i))
    return pl.pallas_call(
        _post_kernel,
        grid=(t // tm,),
        in_specs=[
            pl.BlockSpec((tm, dm), lambda i: (i, 0)),
            pl.BlockSpec((dm, d), lambda i: (0, 0)),
            pl.BlockSpec((tm, d), lambda i: (i, 0)),
            modspec(2),
            pl.BlockSpec((1, d), lambda i: (0, 0)),
            modspec(4),
            modspec(3),
            pl.BlockSpec((N_EXPERTS, d), lambda i: (0, 0)),
            pl.BlockSpec((N_EXPERTS, 1), lambda i: (0, 0)),
        ],
        out_specs=[
            pl.BlockSpec((tm, d), lambda i: (i, 0)),
            pl.BlockSpec((tm, d), lambda i: (i, 0)),
            tok, tok, tok,
            pl.BlockSpec((N_EXPERTS, LANES), lambda i: (0, 0)),
        ],
        out_shape=[
            jax.ShapeDtypeStruct((t, d), F32),
            jax.ShapeDtypeStruct((t, d), F32),
            jax.ShapeDtypeStruct((2, t), I32),
            jax.ShapeDtypeStruct((2, t), F32),
            jax.ShapeDtypeStruct((2, t), I32),
            jax.ShapeDtypeStruct((N_EXPERTS, LANES), F32),
        ],
        scratch_shapes=[pltpu.VMEM((N_EXPERTS, LANES), F32)],
        compiler_params=_cparams(("arbitrary",)),
        name="post",
    )(y_mix, w_out, xf, mod, norm2_w.reshape(1, d), mod, mod, rw_t, rb.reshape(N_EXPERTS, 1))


META_ROWS = 8
META_LANES = 256
ROW_BLK_E, ROW_CNT, ROW_PSTART, ROW_NUSED = 0, 1, 2, 3


def _col_to_row(colv):
    n = colv.shape[0]
    r = lax.broadcasted_iota(I32, (n, n), 0)
    c = lax.broadcasted_iota(I32, (n, n), 1)
    return jnp.sum(jnp.where(r == c, colv, 0.0), axis=0, keepdims=True)


def _meta_kernel(cnt_ref, eidx_ref, rank_ref, dest_ref, meta_ref):
    t = eidx_ref.shape[1]
    cnt = cnt_ref[...]
    nblk = jnp.floor((cnt + (MOE_BLOCK - 1)) * (1.0 / MOE_BLOCK))
    r = lax.broadcasted_iota(I32, (N_EXPERTS, N_EXPERTS), 0)
    c = lax.broadcasted_iota(I32, (N_EXPERTS, N_EXPERTS), 1)
    lower = (c < r).astype(BF16)
    pstart = jnp.dot(lower, nblk.astype(BF16), preferred_element_type=F32)
    pend = pstart + nblk

    chunk = min(1024, t)
    for j in range(t // chunk):
        sl = slice(j * chunk, (j + 1) * chunk)
        eio = lax.broadcasted_iota(I32, (N_EXPERTS, chunk), 0)
        for k in range(2):
            oh = eio == eidx_ref[k:k + 1, sl]
            ps = jnp.sum(jnp.where(oh, pstart[:, 0:1], 0.0), axis=0, keepdims=True)
            dest_ref[k:k + 1, sl] = (ps * MOE_BLOCK).astype(I32) + rank_ref[k:k + 1, sl]

    bl = lax.broadcasted_iota(I32, (N_EXPERTS, META_LANES), 1).astype(F32)
    raw = jnp.sum((pend[:, 0:1] <= bl).astype(F32), axis=0, keepdims=True)
    raw = jnp.minimum(raw, N_EXPERTS - 1.0)
    nused = pend[N_EXPERTS - 1:N_EXPERTS, 0:1]
    used = bl[0:1, :] < nused
    last_e = jnp.max(jnp.where(used, raw, 0.0), axis=1, keepdims=True)
    meta_ref[...] = jnp.zeros(meta_ref.shape, I32)
    meta_ref[ROW_BLK_E:ROW_BLK_E + 1, :] = jnp.where(used, raw, last_e).astype(I32)
    meta_ref[ROW_CNT:ROW_CNT + 1, 0:N_EXPERTS] = _col_to_row(cnt[:, 0:1]).astype(I32)
    meta_ref[ROW_PSTART:ROW_PSTART + 1, 0:N_EXPERTS] = (_col_to_row(pstart[:, 0:1]) * MOE_BLOCK).astype(I32)
    meta_ref[ROW_NUSED:ROW_NUSED + 1, :] = jnp.broadcast_to(nused, (1, META_LANES)).astype(I32)


def _meta(cnt, eidx, rank):
    t = eidx.shape[1]
    full = lambda shape: pl.BlockSpec(shape, lambda: (0,) * len(shape))
    return pl.pallas_call(
        _meta_kernel,
        in_specs=[full((N_EXPERTS, LANES)), full((2, t)), full((2, t))],
        out_specs=[full((2, t)), full((META_ROWS, META_LANES))],
        out_shape=[jax.ShapeDtypeStruct((2, t), I32), jax.ShapeDtypeStruct((META_ROWS, META_LANES), I32)],
        name="meta",
    )(cnt, eidx, rank)


SUBLANES = 8
PAD_RUNS = (64, 32, 16, 8)


def _scatter_kernel(cnt_sm, ps_sm, dest_ref, h2_ref, rows_ref, zbuf, sem, zsem):
    tm = h2_ref.shape[0]

    def row_copy(tok, k):
        return pltpu.make_async_copy(h2_ref.at[pl.ds(tok, 1)], rows_ref.at[pl.ds(dest_ref[k, tok], 1)], sem)

    def issue(tok, carry):
        row_copy(tok, 0).start()
        row_copy(tok, 1).start()
        return carry

    def drain(tok, carry):
        row_copy(tok, 0).wait()
        row_copy(tok, 1).wait()
        return carry

    lax.fori_loop(0, tm, issue, 0)
    lax.fori_loop(0, tm, drain, 0)

    @pl.when(pl.program_id(0) == pl.num_programs(0) - 1)
    def _():
        zbuf[...] = jnp.zeros(zbuf.shape, zbuf.dtype)

        def pad_copies(e, do_start):
            def go(cp):
                if do_start:
                    cp.start()
                else:
                    cp.wait()

            cnt = cnt_sm[e]
            npad = (-cnt) & (MOE_BLOCK - 1)
            off = ps_sm[e] + cnt
            nhead = npad & (SUBLANES - 1)
            for r in range(SUBLANES - 1):
                @pl.when(r < nhead)
                def _():
                    go(pltpu.make_async_copy(zbuf.at[pl.ds(0, 1)], rows_ref.at[pl.ds(off + r, 1)], zsem))
            off = off + nhead
            for bit in PAD_RUNS:
                @pl.when((npad & bit) != 0)
                def _():
                    go(pltpu.make_async_copy(zbuf.at[pl.ds(0, bit)],
                                             rows_ref.at[pl.ds(pl.multiple_of(off, SUBLANES), bit)], zsem))
                off = off + (npad & bit)

        def start_e(e, carry):
            pad_copies(e, True)
            return carry

        def wait_e(e, carry):
            pad_copies(e, False)
            return carry

        lax.fori_loop(0, N_EXPERTS, start_e, 0)
        lax.fori_loop(0, N_EXPERTS, wait_e, 0)

        last = N_EXPERTS - 1
        used_rows = ps_sm[last] + ((cnt_sm[last] + (MOE_BLOCK - 1)) & -MOE_BLOCK)
        half = MOE_BLOCK // 2

        def tail_copy(j):
            return pltpu.make_async_copy(
                zbuf, rows_ref.at[pl.ds(pl.multiple_of(used_rows + j * half, SUBLANES), half)], zsem)

        n_tail = (rows_ref.shape[0] - used_rows) // half
        lax.fori_loop(0, n_tail, lambda j, c: (tail_copy(j).start(), c)[1], 0)
        lax.fori_loop(0, n_tail, lambda j, c: (tail_copy(j).wait(), c)[1], 0)


def _scatter(cnt_row, ps_row, dest, h2, n_rows):
    t, d = h2.shape
    tm = min(256, t)
    return pl.pallas_call(
        _scatter_kernel,
        grid_spec=pltpu.PrefetchScalarGridSpec(
            num_scalar_prefetch=2,
            grid=(t // tm,),
            in_specs=[
                pl.BlockSpec((2, tm), lambda i, *_: (0, i), memory_space=pltpu.SMEM),
                pl.BlockSpec((tm, d), lambda i, *_: (i, 0)),
            ],
            out_specs=pl.BlockSpec(memory_space=pl.ANY),
            scratch_shapes=[
                pltpu.VMEM((MOE_BLOCK // 2, d), h2.dtype),
                pltpu.SemaphoreType.DMA(()),
                pltpu.SemaphoreType.DMA(()),
            ],
        ),
        out_shape=jax.ShapeDtypeStruct((n_rows, d), h2.dtype),
        compiler_params=_cparams(("arbitrary",)),
        name="scatter",
    )(cnt_row, ps_row, dest, h2)


def _expert_kernel(be_sm, nu_sm, cnt_sm, rows_ref, wg_hbm, wu_hbm, wd_hbm, out_ref,
                   land_g, land_u, land_d, wg_scr, wu_scr, wd_scr, sems, st_sm, *, layer):
    b = pl.program_id(0)

    def fetch(e, slot):
        return (pltpu.make_async_copy(wg_hbm.at[layer, e], land_g.at[slot], sems.at[slot, 0]),
                pltpu.make_async_copy(wu_hbm.at[layer, e], land_u.at[slot], sems.at[slot, 1]),
                pltpu.make_async_copy(wd_hbm.at[layer, e], land_d.at[slot], sems.at[slot, 2]))

    def next_used(e):
        return lax.while_loop(lambda n: (n < N_EXPERTS) & (cnt_sm[jnp.minimum(n, N_EXPERTS - 1)] == 0),
                              lambda n: n + 1, e + 1)

    def start_next(slot):
        nxt = next_used(st_sm[1])

        @pl.when(nxt < N_EXPERTS)
        def _():
            for cp in fetch(nxt, slot):
                cp.start()
        st_sm[1] = nxt

    @pl.when(b == 0)
    def _():
        st_sm[0] = 0
        st_sm[1] = -1
        start_next(0)
        start_next(1)

    @pl.when(b < nu_sm[0])
    def _():
        prev = be_sm[jnp.maximum(b - 1, 0)]

        @pl.when((b == 0) | (be_sm[b] != prev))
        def _():
            slot = st_sm[0] & 1
            for cp in fetch(be_sm[b], slot):
                cp.wait()
            wg_scr[...] = land_g[slot].astype(BF16)
            wu_scr[...] = land_u[slot].astype(BF16)
            wd_scr[...] = land_d[slot].astype(BF16)
            st_sm[0] = st_sm[0] + 1
            start_next(slot)

        xb = rows_ref[...].astype(BF16)
        gate = jnp.dot(xb, wg_scr[...], preferred_element_type=F32)
        up = jnp.dot(xb, wu_scr[...], preferred_element_type=F32)
        act = (_silu(gate) * up).astype(BF16)
        out_ref[...] = jnp.dot(act, wd_scr[...], preferred_element_type=F32)

    @pl.when(b >= nu_sm[0])
    def _():
        out_ref[...] = jnp.zeros(out_ref.shape, out_ref.dtype)


def _experts(blk_e, nused, cnt_row, rows, w_gate, w_up, w_down, layer):
    n_rows, d = rows.shape
    de = w_gate.shape[3]
    n_blocks = n_rows // MOE_BLOCK
    blk = lambda b, be, nu, cnt: (jnp.minimum(b, nu[0] - 1), 0)
    hbm = pl.BlockSpec(memory_space=pl.ANY)
    return pl.pallas_call(
        functools.partial(_expert_kernel, layer=layer),
        grid_spec=pltpu.PrefetchScalarGridSpec(
            num_scalar_prefetch=3,
            grid=(n_blocks,),
            in_specs=[pl.BlockSpec((MOE_BLOCK, d), blk), hbm, hbm, hbm],
            out_specs=pl.BlockSpec((MOE_BLOCK, d), lambda b, be, nu, cnt: (b, 0)),
            scratch_shapes=[
                pltpu.VMEM((2, d, de), F32),
                pltpu.VMEM((2, d, de), F32),
                pltpu.VMEM((2, de, d), F32),
                pltpu.VMEM((d, de), BF16),
                pltpu.VMEM((d, de), BF16),
                pltpu.VMEM((de, d), BF16),
                pltpu.SemaphoreType.DMA((2, 3)),
                pltpu.SMEM((2,), I32),
            ],
        ),
        out_shape=jax.ShapeDtypeStruct((n_rows, d), F32),
        compiler_params=_cparams(("arbitrary",)),
        name="experts",
    )(blk_e, nused, cnt_row, rows, w_gate, w_up, w_down)


def _combine_kernel(dest_ref, gate_ref, x_ref, g2_ref, fw_ref, os_ref, out_ref, gbuf, sem, *, final):
    tm = x_ref.shape[0]

    def row_copy(tok, k):
        return pltpu.make_async_copy(os_ref.at[pl.ds(dest_ref[k, tok], 1)], gbuf.at[k, pl.ds(tok, 1)], sem)

    def issue(tok, carry):
        row_copy(tok, 0).start()
        row_copy(tok, 1).start()
        return carry

    def drain(tok, carry):
        row_copy(tok, 0).wait()
        row_copy(tok, 1).wait()
        return carry

    lax.fori_loop(0, tm, issue, 0)
    lax.fori_loop(0, tm, drain, 0)

    r = lax.broadcasted_iota(I32, (tm, tm), 0)
    c = lax.broadcasted_iota(I32, (tm, tm), 1)
    eye = r == c
    w0 = jnp.sum(jnp.where(eye, gate_ref[0:1, :], 0.0), axis=1, keepdims=True)
    w1 = jnp.sum(jnp.where(eye, gate_ref[1:2, :], 0.0), axis=1, keepdims=True)
    y = gbuf[0] * w0 + gbuf[1] * w1
    x = x_ref[...] + g2_ref[...] * y
    if final:
        ms = jnp.mean(x * x, axis=-1, keepdims=True)
        x = x * lax.rsqrt(ms + EPS) * fw_ref[...]
    out_ref[...] = x


def _combine(dest, gate, xf, mod, final_w, out_sorted, seq, final):
    t, d = xf.shape
    tm = min(256, seq)
    per_batch = seq // tm
    return pl.pallas_call(
        functools.partial(_combine_kernel, final=final),
        grid=(t // tm,),
        in_specs=[
            pl.BlockSpec((2, tm), lambda i: (0, i), memory_space=pltpu.SMEM),
            pl.BlockSpec((2, tm), lambda i: (0, i)),
            pl.BlockSpec((tm, d), lambda i: (i, 0)),
            pl.BlockSpec((None, None, 1, d), lambda i: (i // per_batch, 5, 0, 0)),
            pl.BlockSpec((1, d), lambda i: (0, 0)),
            pl.BlockSpec(memory_space=pl.ANY),
        ],
        out_specs=pl.BlockSpec((tm, d), lambda i: (i, 0)),
        out_shape=jax.ShapeDtypeStruct((t, d), F32),
        scratch_shapes=[pltpu.VMEM((2, tm, d), F32), pltpu.SemaphoreType.DMA(())],
        compiler_params=_cparams(("arbitrary",)),
        name="combine",
    )(dest, gate, xf, mod, final_w.reshape(1, d), out_sorted)


def _pad_lanes(v, n=LANES):
    return jnp.pad(v, (0, n - v.shape[0])).reshape(1, n)


def kernel(x, c, ada_w, ada_b, norm1_w, w_in, gm_ln_w, gm_ln_b, gm_ws, gm_bs, conv_w, conv_b, dt_bias, a_log,
           d_skip, ssd_norm_w, w_out, norm2_w, router_w, router_b, exp_w_gate, exp_w_up, exp_w_down,
           final_norm_w):
    batch, seq, d = x.shape
    t = batch * seq
    depth = ada_w.shape[0]
    assert batch <= 8 and seq % CHUNK == 0 and w_in.shape[2] == MAIN_PROJ + SSD_HEADS
    n_rows = (-(-(t * 2) // MOE_BLOCK) + N_EXPERTS) * MOE_BLOCK
    assert n_rows // MOE_BLOCK <= META_LANES

    ada = _ada(jnp.pad(c, ((0, 8 - batch), (0, 0))), ada_w, ada_b)
    rw_t = router_w.T.astype(BF16)
    xf = x.reshape(t, d)
    for l in range(depth):
        mod = ada[l, :batch].reshape(batch, 6, 1, d)
        w_main = w_in[l, :, :MAIN_PROJ].astype(BF16)
        w_dt = jnp.pad(w_in[l, :, MAIN_PROJ:], ((0, 0), (0, LANES - SSD_HEADS))).astype(BF16)
        proj, dt_raw = _inproj(xf, norm1_w[l], mod, w_main, w_dt, seq)
        mixer_params = dict(
            lnw=gm_ln_w[l].reshape(1, GM_WIDTH), lnb=gm_ln_b[l].reshape(1, GM_WIDTH),
            ws=gm_ws[l], bst=gm_bs[l].T,
            cw=conv_w[l], cb=conv_b[l].reshape(1, CONV_DIM),
            dtb=_pad_lanes(dt_bias[l]), alog=_pad_lanes(a_log[l]),
            dsk=jnp.repeat(d_skip[l], SSD_WIDTH // SSD_HEADS).reshape(1, SSD_WIDTH),
            nw=ssd_norm_w[l].reshape(1, SSD_WIDTH))
        y_mix = _mixer(proj, dt_raw, mixer_params, batch, seq)
        xf, h2, eidx, gate, rank, cnt = _post(y_mix, w_out[l].astype(BF16), xf, mod, norm2_w[l], rw_t,
                                              router_b, seq)
        dest, meta = _meta(cnt, eidx, rank)
        rows = _scatter(meta[ROW_CNT, :N_EXPERTS], meta[ROW_PSTART, :N_EXPERTS], dest, h2, n_rows)
        out_sorted = _experts(meta[ROW_BLK_E, :n_rows // MOE_BLOCK], meta[ROW_NUSED, :1],
                              meta[ROW_CNT, :N_EXPERTS], rows, exp_w_gate, exp_w_up, exp_w_down, l)
        xf = _combine(dest, gate, xf, mod, final_norm_w, out_sorted, seq, final=(l == depth - 1))
    return xf.reshape(batch, seq, d)
```

```python
import functools

import jax
import jax.numpy as jnp
from jax import lax
from jax.experimental import pallas as pl
from jax.experimental.pallas import tpu as pltpu

F32 = jnp.float32
BF16 = jnp.bfloat16
I32 = jnp.int32

EPS = 1e-6
LANES = 128
CHUNK = 128
GM_HEADS = 8
GM_WIDTH = 1024
SSD_WIDTH = 1024
SSD_HEADS = 16
SSD_GROUPS = 2
SSD_STATE = 128
SSD_CONV = 4
CONV_DIM = SSD_WIDTH + 2 * SSD_GROUPS * SSD_STATE
MAIN_PROJ = 2 * GM_WIDTH + SSD_WIDTH + CONV_DIM
N_EXPERTS = 64
EXPERTS_PER_GROUP = 8
N_EXPERT_GROUPS = 8
MOE_BLOCK = 128
HALO = 8
SLAB = 8
VMEM_LIMIT = 56 * 1024 * 1024


def _cparams(sem, vmem=VMEM_LIMIT):
    return pltpu.CompilerParams(dimension_semantics=sem, vmem_limit_bytes=vmem)


def _silu(x):
    return x * jax.nn.sigmoid(x)


def _gelu(x):
    return 0.5 * x * (1.0 + lax.erf(x * 0.7071067811865476))


def _softplus(x):
    return jnp.maximum(x, 0.0) + jnp.log1p(jnp.exp(-jnp.abs(x)))


def _ada_kernel(c_ref, w_ref, b_ref, o_ref):
    sc = _silu(c_ref[...])
    o_ref[0] = jnp.dot(sc.astype(BF16), w_ref[0].astype(BF16), preferred_element_type=F32) + b_ref[0]


def _ada(c_pad, ada_w, ada_b):
    n_layers, d, n = ada_w.shape
    tn = 1024
    return pl.pallas_call(
        _ada_kernel,
        grid=(n_layers, n // tn),
        in_specs=[
            pl.BlockSpec((8, d), lambda l, j: (0, 0)),
            pl.BlockSpec((1, d, tn), lambda l, j: (l, 0, j)),
            pl.BlockSpec((1, 1, tn), lambda l, j: (l, 0, j)),
        ],
        out_specs=pl.BlockSpec((1, 8, tn), lambda l, j: (l, 0, j)),
        out_shape=jax.ShapeDtypeStruct((n_layers, 8, n), F32),
        compiler_params=_cparams(("parallel", "parallel")),
        name="ada",
    )(c_pad, ada_w, ada_b.reshape(n_layers, 1, n))


def _inproj_kernel(x_ref, nw_ref, s_ref, sh_ref, w_ref, wdt_ref, o_ref, dt_ref, h_scr):
    @pl.when(pl.program_id(1) == 0)
    def _():
        x = x_ref[...]
        ms = jnp.mean(x * x, axis=-1, keepdims=True)
        y = x * lax.rsqrt(ms + EPS) * nw_ref[...]
        h = (y * (1.0 + s_ref[...]) + sh_ref[...]).astype(BF16)
        h_scr[...] = h
        dt_ref[...] = jnp.dot(h, wdt_ref[...], preferred_element_type=F32)

    o_ref[...] = jnp.dot(h_scr[...], w_ref[...], preferred_element_type=F32)


def _inproj(xf, norm_w, mod, w_main, w_dt, seq):
    t, d = xf.shape
    n = w_main.shape[1]
    tm = min(1024, seq)
    tn = 512
    per_batch = seq // tm
    return pl.pallas_call(
        _inproj_kernel,
        grid=(t // tm, n // tn),
        in_specs=[
            pl.BlockSpec((tm, d), lambda i, j: (i, 0)),
            pl.BlockSpec((1, d), lambda i, j: (0, 0)),
            pl.BlockSpec((None, None, 1, d), lambda i, j: (i // per_batch, 1, 0, 0)),
            pl.BlockSpec((None, None, 1, d), lambda i, j: (i // per_batch, 0, 0, 0)),
            pl.BlockSpec((d, tn), lambda i, j: (0, j)),
            pl.BlockSpec((d, LANES), lambda i, j: (0, 0)),
        ],
        out_specs=[
            pl.BlockSpec((tm, tn), lambda i, j: (i, j)),
            pl.BlockSpec((tm, LANES), lambda i, j: (i, 0)),
        ],
        out_shape=[
            jax.ShapeDtypeStruct((t, n), F32),
            jax.ShapeDtypeStruct((t, LANES), F32),
        ],
        scratch_shapes=[pltpu.VMEM((tm, d), BF16)],
        compiler_params=_cparams(("parallel", "arbitrary")),
        name="inproj",
    )(xf, norm_w.reshape(1, d), mod, mod, w_main, w_dt)


def _mixer_kernel(u_ref, v_ref, z_ref, xbc_ref, dt_ref,
                  lnw_ref, lnb_ref, ws_ref, bst_ref, cw_ref, cb_ref, dtb_ref, alog_ref,
                  dsk_ref, nw_ref, y_ref, buf_scr, xa_scr, state_scr, ys_scr):
    @pl.when(pl.program_id(1) == 0)
    def _():
        buf_scr[0:HALO, :] = jnp.zeros((HALO, CONV_DIM), F32)
        state_scr[...] = jnp.zeros(state_scr.shape, F32)

    row = lax.broadcasted_iota(I32, (CHUNK, CHUNK), 0)
    col = lax.broadcasted_iota(I32, (CHUNK, CHUNK), 1)
    tril = row >= col
    lane_lo = col < (LANES // 2)
    lane_lo_row = lane_lo[0:1, :]

    for h in range(GM_HEADS):
        sl = slice(h * LANES, (h + 1) * LANES)
        gu = _gelu(u_ref[:, sl])
        gv = _gelu(v_ref[:, sl])
        mu = jnp.mean(gv, axis=-1, keepdims=True)
        dv = gv - mu
        var = jnp.mean(dv * dv, axis=-1, keepdims=True)
        vn = dv * lax.rsqrt(var + EPS) * lnw_ref[:, sl] + lnb_ref[:, sl]
        w = jnp.where(tril, ws_ref[h], 0.0).astype(BF16)
        s = jnp.dot(w, vn.astype(BF16), preferred_element_type=F32) + bst_ref[:, h:h + 1]
        y_ref[:, sl] = (gu * s).astype(y_ref.dtype)

    buf_scr[HALO:HALO + CHUNK, :] = xbc_ref[...]
    for cb in range(CONV_DIM // 256):
        cs_ = slice(cb * 256, (cb + 1) * 256)
        acc = cb_ref[:, cs_] + cw_ref[0:1, cs_] * buf_scr[HALO - 3:HALO - 3 + CHUNK, cs_]
        for k in range(1, SSD_CONV):
            acc = acc + cw_ref[k:k + 1, cs_] * buf_scr[HALO - 3 + k:HALO - 3 + k + CHUNK, cs_]
        xa_scr[:, cs_] = _silu(acc)
    buf_scr[0:HALO, :] = buf_scr[CHUNK:CHUNK + HALO, :]

    dt = _softplus(dt_ref[...] + dtb_ref[...])
    a = -jnp.exp(alog_ref[...])
    ad = dt * a
    cs = jnp.dot(tril.astype(F32), ad, precision=lax.Precision.HIGHEST, preferred_element_type=F32)
    cs_t = cs.T
    last = cs[CHUNK - 1:CHUNK, :]
    ds = jnp.exp(last - cs)
    ecs = jnp.exp(cs)
    cd = jnp.exp(last)

    def pair_expand(m, p):
        return jnp.where(lane_lo[0:m.shape[0], :], m[:, 2 * p:2 * p + 1], m[:, 2 * p + 1:2 * p + 2])

    pairs_per_group = SSD_HEADS // SSD_GROUPS // 2
    gw = SSD_WIDTH // SSD_GROUPS
    for g in range(SSD_GROUPS):
        bm_g = xa_scr[:, SSD_WIDTH + g * SSD_STATE:SSD_WIDTH + (g + 1) * SSD_STATE]
        cm_g = xa_scr[:, SSD_WIDTH + (SSD_GROUPS + g) * SSD_STATE:SSD_WIDTH + (SSD_GROUPS + g + 1) * SSD_STATE]
        cmb = cm_g.astype(BF16)
        bmb = bm_g.astype(BF16)
        cbm = lax.dot_general(cmb, bmb, (((1,), (1,)), ((), ())), preferred_element_type=F32)
        bm_t = bm_g.T.astype(BF16)
        st_prev = state_scr[:, g * gw:(g + 1) * gw]
        y_off = jnp.dot(cmb, st_prev.astype(BF16), preferred_element_type=F32)
        xdds = []
        cds = []
        for q in range(pairs_per_group):
            p = g * pairs_per_group + q
            sl = slice(p * LANES, (p + 1) * LANES)
            xs_p = xa_scr[:, sl]
            xd = xs_p * pair_expand(dt, p)
            xdb = xd.astype(BF16)
            ys = []
            for hh in (2 * p, 2 * p + 1):
                diff = cs[:, hh:hh + 1] - cs_t[hh:hh + 1, :]
                lm = jnp.where(tril, jnp.exp(jnp.where(tril, diff, 0.0)), 0.0)
                wmat = (cbm * lm).astype(BF16)
                ys.append(jnp.dot(wmat, xdb, preferred_element_type=F32))
            y_diag = jnp.where(lane_lo, ys[0], ys[1])
            y = y_diag + y_off[:, q * LANES:(q + 1) * LANES] * pair_expand(ecs, p)
            ys_scr[:, sl] = y + dsk_ref[:, sl] * xs_p
            xdds.append((xd * pair_expand(ds, p)).astype(BF16))
            cds.append(jnp.where(lane_lo_row, cd[:, 2 * p:2 * p + 1], cd[:, 2 * p + 1:2 * p + 2]))
        st_new = jnp.dot(bm_t, jnp.concatenate(xdds, axis=1), preferred_element_type=F32)
        state_scr[:, g * gw:(g + 1) * gw] = st_prev * jnp.concatenate(cds, axis=1) + st_new

    for g in range(SSD_GROUPS):
        sl = slice(g * gw, (g + 1) * gw)
        gg = ys_scr[:, sl] * _silu(z_ref[:, sl])
        ms = jnp.mean(gg * gg, axis=-1, keepdims=True)
        y_ref[:, GM_WIDTH + g * gw:GM_WIDTH + (g + 1) * gw] = (
            gg * lax.rsqrt(ms + EPS) * nw_ref[:, sl]).astype(y_ref.dtype)


def _mixer(proj, dt_raw, p, batch, seq):
    t = proj.shape[0]
    nc = seq // CHUNK
    rows = lambda b, c: b * nc + c
    full = lambda shape: pl.BlockSpec(shape, lambda b, c: (0,) * len(shape))
    return pl.pallas_call(
        _mixer_kernel,
        grid=(batch, nc),
        in_specs=[
            pl.BlockSpec((CHUNK, GM_WIDTH), lambda b, c: (rows(b, c), 0)),
            pl.BlockSpec((CHUNK, GM_WIDTH), lambda b, c: (rows(b, c), 1)),
            pl.BlockSpec((CHUNK, SSD_WIDTH), lambda b, c: (rows(b, c), 2)),
            pl.BlockSpec((CHUNK, CONV_DIM), lambda b, c: (rows(b, c), 2)),
            pl.BlockSpec((CHUNK, LANES), lambda b, c: (rows(b, c), 0)),
            full((1, GM_WIDTH)), full((1, GM_WIDTH)),
            full((GM_HEADS, CHUNK, CHUNK)), full((CHUNK, GM_HEADS)),
            full((SSD_CONV, CONV_DIM)), full((1, CONV_DIM)),
            full((1, LANES)), full((1, LANES)),
            full((1, SSD_WIDTH)), full((1, SSD_WIDTH)),
        ],
        out_specs=pl.BlockSpec((CHUNK, GM_WIDTH + SSD_WIDTH), lambda b, c: (rows(b, c), 0)),
        out_shape=jax.ShapeDtypeStruct((t, GM_WIDTH + SSD_WIDTH), BF16),
        scratch_shapes=[
            pltpu.VMEM((HALO + CHUNK, CONV_DIM), F32),
            pltpu.VMEM((CHUNK, CONV_DIM), F32),
            pltpu.VMEM((SSD_STATE, SSD_WIDTH), F32),
            pltpu.VMEM((CHUNK, SSD_WIDTH), F32),
        ],
        compiler_params=_cparams(("parallel", "arbitrary")),
        name="mixer",
    )(proj, proj, proj, proj, dt_raw,
      p["lnw"], p["lnb"], p["ws"], p["bst"], p["cw"], p["cb"], p["dtb"], p["alog"], p["dsk"], p["nw"])


def _first_max(vals, axis_iota, n):
    m = jnp.max(vals, axis=0, keepdims=True)
    idx = jnp.min(jnp.where(vals == m, axis_iota, n), axis=0, keepdims=True)
    return m, idx


def _post_kernel(y_ref, wout_ref, x_ref, g1_ref, n2w_ref, s2_ref, sh2_ref, rwt_ref, rb_ref,
                 xo_ref, h2_ref, eidx_ref, gate_ref, rank_ref, cnt_ref, carry_scr):
    @pl.when(pl.program_id(0) == 0)
    def _():
        carry_scr[...] = jnp.zeros(carry_scr.shape, F32)

    tm = x_ref.shape[0]
    mix = jnp.dot(y_ref[...], wout_ref[...], preferred_element_type=F32)
    x = x_ref[...] + g1_ref[...] * mix
    xo_ref[...] = x
    ms = jnp.mean(x * x, axis=-1, keepdims=True)
    h = x * lax.rsqrt(ms + EPS) * n2w_ref[...] * (1.0 + s2_ref[...]) + sh2_ref[...]
    sw = h2_ref.shape[2]
    for s in range(SLAB):
        h2_ref[:, s, :] = h[:, s * sw:(s + 1) * sw]

    logits_t = lax.dot_general(rwt_ref[...], h.astype(BF16), (((1,), (1,)), ((), ())),
                               preferred_element_type=F32)
    scores = jax.nn.sigmoid(logits_t)
    biased = scores + rb_ref[...]

    sub = lax.broadcasted_iota(I32, (EXPERTS_PER_GROUP, tm), 0)
    neg = jnp.float32(-jnp.inf)
    best = None
    for g in range(N_EXPERT_GROUPS):
        grp = biased[g * EXPERTS_PER_GROUP:(g + 1) * EXPERTS_PER_GROUP, :]
        m1, i1 = _first_max(grp, sub, EXPERTS_PER_GROUP)
        m2, i2 = _first_max(jnp.where(sub == i1, neg, grp), sub, EXPERTS_PER_GROUP)
        gs = m1 + m2
        if best is None:
            best, bi, l1, l2 = gs, jnp.zeros((1, tm), I32), i1, i2
        else:
            upd = gs > best
            best = jnp.where(upd, gs, best)
            bi = jnp.where(upd, g, bi)
            l1 = jnp.where(upd, i1, l1)
            l2 = jnp.where(upd, i2, l2)
    e0 = bi * EXPERTS_PER_GROUP + l1
    e1 = bi * EXPERTS_PER_GROUP + l2

    eio = lax.broadcasted_iota(I32, (N_EXPERTS, tm), 0)
    oh0 = eio == e0
    oh1 = eio == e1
    s0 = jnp.sum(jnp.where(oh0, scores, 0.0), axis=0, keepdims=True)
    s1 = jnp.sum(jnp.where(oh1, scores, 0.0), axis=0, keepdims=True)
    tot = s0 + s1
    eidx_ref[0:1, :] = e0
    eidx_ref[1:2, :] = e1
    gate_ref[0:1, :] = s0 / tot
    gate_ref[1:2, :] = s1 / tot

    ohs = oh0.astype(F32) + oh1.astype(F32)
    tr = lax.broadcasted_iota(I32, (tm, tm), 0)
    tc = lax.broadcasted_iota(I32, (tm, tm), 1)
    before = (tr < tc).astype(BF16)
    prefix = jnp.dot(ohs.astype(BF16), before, preferred_element_type=F32)
    base = carry_scr[:, 0:1] + prefix
    rank_ref[0:1, :] = jnp.sum(jnp.where(oh0, base, 0.0), axis=0, keepdims=True).astype(I32)
    rank_ref[1:2, :] = jnp.sum(jnp.where(oh1, base, 0.0), axis=0, keepdims=True).astype(I32)
    carry_scr[...] = carry_scr[...] + jnp.sum(ohs, axis=1, keepdims=True)
    cnt_ref[...] = carry_scr[...]


def _post(y_mix, w_out, xf, mod, norm2_w, rw_t, rb, seq):
    t, d = xf.shape
    dm = y_mix.shape[1]
    tm = min(256, seq)
    per_batch = seq // tm
    modspec = lambda k: pl.BlockSpec((None, None, 1, d), lambda i: (i // per_batch, k, 0, 0))
    tok = pl.BlockSpec((2, tm), lambda i: (0, i))
    return pl.pallas_call(
        _post_kernel,
        grid=(t // tm,),
        in_specs=[
            pl.BlockSpec((tm, dm), lambda i: (i, 0)),
            pl.BlockSpec((dm, d), lambda i: (0, 0)),
            pl.BlockSpec((tm, d), lambda i: (i, 0)),
            modspec(2),
            pl.BlockSpec((1, d), lambda i: (0, 0)),
            modspec(4),
            modspec(3),
            pl.BlockSpec((N_EXPERTS, d), lambda i: (0, 0)),
            pl.BlockSpec((N_EXPERTS, 1), lambda i: (0, 0)),
        ],
        out_specs=[
            pl.BlockSpec((tm, d), lambda i: (i, 0)),
            pl.BlockSpec((tm, SLAB, d // SLAB), lambda i: (i, 0, 0)),
            tok, tok, tok,
            pl.BlockSpec((N_EXPERTS, LANES), lambda i: (0, 0)),
        ],
        out_shape=[
            jax.ShapeDtypeStruct((t, d), F32),
            jax.ShapeDtypeStruct((t, SLAB, d // SLAB), F32),
            jax.ShapeDtypeStruct((2, t), I32),
            jax.ShapeDtypeStruct((2, t), F32),
            jax.ShapeDtypeStruct((2, t), I32),
            jax.ShapeDtypeStruct((N_EXPERTS, LANES), F32),
        ],
        scratch_shapes=[pltpu.VMEM((N_EXPERTS, LANES), F32)],
        compiler_params=_cparams(("arbitrary",)),
        name="post",
    )(y_mix, w_out, xf, mod, norm2_w.reshape(1, d), mod, mod, rw_t, rb.reshape(N_EXPERTS, 1))


META_ROWS = 8
META_LANES = 256
ROW_BLK_E, ROW_CNT, ROW_PSTART, ROW_NUSED, ROW_NVALID = 0, 1, 2, 3, 4


def _col_to_row(colv):
    n = colv.shape[0]
    r = lax.broadcasted_iota(I32, (n, n), 0)
    c = lax.broadcasted_iota(I32, (n, n), 1)
    return jnp.sum(jnp.where(r == c, colv, 0.0), axis=0, keepdims=True)


def _meta_kernel(cnt_ref, eidx_ref, rank_ref, dest_ref, meta_ref):
    t = eidx_ref.shape[1]
    cnt = cnt_ref[...]
    nblk = jnp.floor((cnt + (MOE_BLOCK - 1)) * (1.0 / MOE_BLOCK))
    r = lax.broadcasted_iota(I32, (N_EXPERTS, N_EXPERTS), 0)
    c = lax.broadcasted_iota(I32, (N_EXPERTS, N_EXPERTS), 1)
    lower = (c < r).astype(BF16)
    pstart = jnp.dot(lower, nblk.astype(BF16), preferred_element_type=F32)
    pend = pstart + nblk

    chunk = min(1024, t)
    for j in range(t // chunk):
        sl = slice(j * chunk, (j + 1) * chunk)
        eio = lax.broadcasted_iota(I32, (N_EXPERTS, chunk), 0)
        for k in range(2):
            oh = eio == eidx_ref[k:k + 1, sl]
            ps = jnp.sum(jnp.where(oh, pstart[:, 0:1], 0.0), axis=0, keepdims=True)
            dest_ref[k:k + 1, sl] = (ps * MOE_BLOCK).astype(I32) + rank_ref[k:k + 1, sl]

    bl = lax.broadcasted_iota(I32, (N_EXPERTS, META_LANES), 1).astype(F32)
    raw = jnp.sum((pend[:, 0:1] <= bl).astype(F32), axis=0, keepdims=True)
    raw = jnp.minimum(raw, N_EXPERTS - 1.0)
    nused = pend[N_EXPERTS - 1:N_EXPERTS, 0:1]
    used = bl[0:1, :] < nused
    last_e = jnp.max(jnp.where(used, raw, 0.0), axis=1, keepdims=True)
    meta_ref[...] = jnp.zeros(meta_ref.shape, I32)
    meta_ref[ROW_BLK_E:ROW_BLK_E + 1, :] = jnp.where(used, raw, last_e).astype(I32)
    meta_ref[ROW_CNT:ROW_CNT + 1, 0:N_EXPERTS] = _col_to_row(cnt[:, 0:1]).astype(I32)
    meta_ref[ROW_PSTART:ROW_PSTART + 1, 0:N_EXPERTS] = (_col_to_row(pstart[:, 0:1]) * MOE_BLOCK).astype(I32)
    meta_ref[ROW_NUSED:ROW_NUSED + 1, :] = jnp.broadcast_to(nused, (1, META_LANES)).astype(I32)
    mine = lax.broadcasted_iota(I32, (N_EXPERTS, META_LANES), 0).astype(F32) == raw
    cnt_b = jnp.sum(jnp.where(mine, cnt[:, 0:1], 0.0), axis=0, keepdims=True)
    first_b = jnp.sum(jnp.where(mine, pstart[:, 0:1], 0.0), axis=0, keepdims=True)
    nvalid = jnp.clip(cnt_b - (bl[0:1, :] - first_b) * MOE_BLOCK, 0.0, float(MOE_BLOCK))
    meta_ref[ROW_NVALID:ROW_NVALID + 1, :] = jnp.where(used, nvalid, 0.0).astype(I32)


def _meta(cnt, eidx, rank):
    t = eidx.shape[1]
    full = lambda shape: pl.BlockSpec(shape, lambda: (0,) * len(shape))
    return pl.pallas_call(
        _meta_kernel,
        in_specs=[full((N_EXPERTS, LANES)), full((2, t)), full((2, t))],
        out_specs=[full((2, t)), full((META_ROWS, META_LANES))],
        out_shape=[jax.ShapeDtypeStruct((2, t), I32), jax.ShapeDtypeStruct((META_ROWS, META_LANES), I32)],
        name="meta",
    )(cnt, eidx, rank)


def _invert_kernel(dest_ref, tok_ref, slot_ref):
    i = pl.program_id(0)
    tm = dest_ref.shape[1]
    t = tm * pl.num_programs(0)

    @pl.when(i == 0)
    def _():
        def clear(p, carry):
            tok_ref[p] = 0
            slot_ref[p] = 0
            return carry
        lax.fori_loop(0, slot_ref.shape[0], clear, 0, unroll=8)

    def put(tok, carry):
        for k in range(2):
            p = dest_ref[k, tok]
            tok_ref[p] = i * tm + tok
            slot_ref[p] = k * t + i * tm + tok
        return carry
    lax.fori_loop(0, tm, put, 0, unroll=8)


def _invert(dest, n_rows):
    t = dest.shape[1]
    tm = min(1024, t)
    table = pl.BlockSpec((n_rows,), lambda i: (0,), memory_space=pltpu.SMEM)
    return pl.pallas_call(
        _invert_kernel,
        grid=(t // tm,),
        in_specs=[pl.BlockSpec((2, tm), lambda i: (0, i), memory_space=pltpu.SMEM)],
        out_specs=[table, table],
        out_shape=[jax.ShapeDtypeStruct((n_rows,), I32), jax.ShapeDtypeStruct((n_rows,), I32)],
        compiler_params=_cparams(("arbitrary",)),
        name="invert",
    )(dest)


def _expert_kernel(be_sm, nu_sm, cnt_sm, nv_sm, tokc_ref, tokn_ref, slotc_ref, h2_hbm, wg_hbm, wu_hbm, wd_hbm,
                   yk_hbm, xbuf, obuf, land_g, land_u, land_d, wg_scr, wu_scr, wd_scr, sems, gsem, ssem,
                   st_sm, *, layer):
    b = pl.program_id(0)
    nu = nu_sm[0]
    par = b & 1

    def gather_row(ids_ref, r, slot):
        return pltpu.make_async_copy(h2_hbm.at[ids_ref[0, r]], xbuf.at[slot, r], gsem.at[slot])

    def scatter_row(r, slot):
        return pltpu.make_async_copy(obuf.at[slot, r], yk_hbm.at[slotc_ref[0, r]], ssem.at[slot])

    def gather_block(slot):
        return pltpu.make_async_copy(h2_hbm.at[pl.ds(0, MOE_BLOCK)], xbuf.at[slot], gsem.at[slot])

    def scatter_block(slot):
        return pltpu.make_async_copy(obuf.at[slot], yk_hbm.at[pl.ds(0, MOE_BLOCK)], ssem.at[slot])

    def start_rows(n, row_copy):
        @pl.when(n == MOE_BLOCK)
        def _():
            lax.fori_loop(0, MOE_BLOCK, lambda r, c: (row_copy(r).start(), c)[1], 0, unroll=8)

        @pl.when(n != MOE_BLOCK)
        def _():
            lax.fori_loop(0, n, lambda r, c: (row_copy(r).start(), c)[1], 0)

    def wait_rows(n, row_copy, block_copy):
        @pl.when(n == MOE_BLOCK)
        def _():
            block_copy.wait()

        @pl.when(n != MOE_BLOCK)
        def _():
            lax.fori_loop(0, n, lambda r, c: (row_copy(r).wait(), c)[1], 0)

    def fetch(e, slot):
        return (pltpu.make_async_copy(wg_hbm.at[layer, e], land_g.at[slot], sems.at[slot, 0]),
                pltpu.make_async_copy(wu_hbm.at[layer, e], land_u.at[slot], sems.at[slot, 1]),
                pltpu.make_async_copy(wd_hbm.at[layer, e], land_d.at[slot], sems.at[slot, 2]))

    def next_used(e):
        return lax.while_loop(lambda n: (n < N_EXPERTS) & (cnt_sm[jnp.minimum(n, N_EXPERTS - 1)] == 0),
                              lambda n: n + 1, e + 1)

    def start_next(slot):
        nxt = next_used(st_sm[1])

        @pl.when(nxt < N_EXPERTS)
        def _():
            for cp in fetch(nxt, slot):
                cp.start()
        st_sm[1] = nxt

    @pl.when(b == 0)
    def _():
        st_sm[0] = 0
        st_sm[1] = -1
        start_next(0)
        start_next(1)
        xbuf[...] = jnp.zeros(xbuf.shape, xbuf.dtype)
        start_rows(nv_sm[0], lambda r: gather_row(tokc_ref, r, 0))

    @pl.when(b < nu)
    def _():
        @pl.when(b + 1 < nu)
        def _():
            start_rows(nv_sm[b + 1], lambda r: gather_row(tokn_ref, r, 1 - par))

        prev = be_sm[jnp.maximum(b - 1, 0)]

        @pl.when((b == 0) | (be_sm[b] != prev))
        def _():
            slot = st_sm[0] & 1
            for cp in fetch(be_sm[b], slot):
                cp.wait()
            wg_scr[...] = land_g[slot].astype(BF16)
            wu_scr[...] = land_u[slot].astype(BF16)
            wd_scr[...] = land_d[slot].astype(BF16)
            st_sm[0] = st_sm[0] + 1
            start_next(slot)

        wait_rows(nv_sm[b], lambda r: gather_row(tokc_ref, r, par), gather_block(par))
        sw = xbuf.shape[3]
        xb = jnp.concatenate([xbuf[par, :, s, :] for s in range(SLAB)], axis=1).astype(BF16)
        gate = jnp.dot(xb, wg_scr[...], preferred_element_type=F32)
        up = jnp.dot(xb, wu_scr[...], preferred_element_type=F32)
        act = (_silu(gate) * up).astype(BF16)
        out = jnp.dot(act, wd_scr[...], preferred_element_type=F32)

        @pl.when(b >= 2)
        def _():
            wait_rows(nv_sm[jnp.maximum(b - 2, 0)], lambda r: scatter_row(r, par), scatter_block(par))

        for s in range(SLAB):
            obuf[par, :, s, :] = out[:, s * sw:(s + 1) * sw]
        start_rows(nv_sm[b], lambda r: scatter_row(r, par))

        @pl.when(b == nu - 1)
        def _():
            @pl.when(b >= 1)
            def _():
                wait_rows(nv_sm[jnp.maximum(b - 1, 0)], lambda r: scatter_row(r, 1 - par),
                          scatter_block(1 - par))
            wait_rows(nv_sm[b], lambda r: scatter_row(r, par), scatter_block(par))


def _experts(blk_e, nused, cnt_row, nvalid, tok_sorted, slot_sorted, h2, w_gate, w_up, w_down, layer):
    t, _, sw = h2.shape
    d = SLAB * sw
    de = w_gate.shape[3]
    n_blocks = slot_sorted.shape[0] // MOE_BLOCK
    toks = tok_sorted.reshape(n_blocks, 1, MOE_BLOCK)
    slots = slot_sorted.reshape(n_blocks, 1, MOE_BLOCK)
    hbm = pl.BlockSpec(memory_space=pl.ANY)
    id_spec = lambda ahead: pl.BlockSpec(
        (None, 1, MOE_BLOCK), lambda b, be, nu, cnt, nv: (jnp.minimum(b + ahead, nu[0] - 1), 0, 0),
        memory_space=pltpu.SMEM)
    return pl.pallas_call(
        functools.partial(_expert_kernel, layer=layer),
        grid_spec=pltpu.PrefetchScalarGridSpec(
            num_scalar_prefetch=4,
            grid=(n_blocks,),
            in_specs=[id_spec(0), id_spec(1), id_spec(0), hbm, hbm, hbm, hbm],
            out_specs=hbm,
            scratch_shapes=[
                pltpu.VMEM((2, MOE_BLOCK, SLAB, sw), F32),
                pltpu.VMEM((2, MOE_BLOCK, SLAB, sw), F32),
                pltpu.VMEM((2, d, de), F32),
                pltpu.VMEM((2, d, de), F32),
                pltpu.VMEM((2, de, d), F32),
                pltpu.VMEM((d, de), BF16),
                pltpu.VMEM((d, de), BF16),
                pltpu.VMEM((de, d), BF16),
                pltpu.SemaphoreType.DMA((2, 3)),
                pltpu.SemaphoreType.DMA((2,)),
                pltpu.SemaphoreType.DMA((2,)),
                pltpu.SMEM((2,), I32),
            ],
        ),
        out_shape=jax.ShapeDtypeStruct((2 * t, SLAB, sw), F32),
        compiler_params=_cparams(("arbitrary",)),
        name="experts",
    )(blk_e, nused, cnt_row, nvalid, toks, toks, slots, h2, w_gate, w_up, w_down)


def _combine_kernel(gate_ref, x_ref, g2_ref, fw_ref, yk_ref, out_ref, *, final):
    tm = x_ref.shape[0]
    r = lax.broadcasted_iota(I32, (tm, tm), 0)
    c = lax.broadcasted_iota(I32, (tm, tm), 1)
    eye = r == c
    w0 = jnp.sum(jnp.where(eye, gate_ref[0:1, :], 0.0), axis=1, keepdims=True)
    w1 = jnp.sum(jnp.where(eye, gate_ref[1:2, :], 0.0), axis=1, keepdims=True)
    y = jnp.concatenate([yk_ref[0, :, s, :] * w0 + yk_ref[1, :, s, :] * w1 for s in range(SLAB)], axis=1)
    x = x_ref[...] + g2_ref[...] * y
    if final:
        ms = jnp.mean(x * x, axis=-1, keepdims=True)
        x = x * lax.rsqrt(ms + EPS) * fw_ref[...]
    out_ref[...] = x


def _combine(gate, xf, mod, final_w, yk, seq, final):
    t, d = xf.shape
    tm = min(256, seq)
    per_batch = seq // tm
    return pl.pallas_call(
        functools.partial(_combine_kernel, final=final),
        grid=(t // tm,),
        in_specs=[
            pl.BlockSpec((2, tm), lambda i: (0, i)),
            pl.BlockSpec((tm, d), lambda i: (i, 0)),
            pl.BlockSpec((None, None, 1, d), lambda i: (i // per_batch, 5, 0, 0)),
            pl.BlockSpec((1, d), lambda i: (0, 0)),
            pl.BlockSpec((2, tm, SLAB, d // SLAB), lambda i: (0, i, 0, 0)),
        ],
        out_specs=pl.BlockSpec((tm, d), lambda i: (i, 0)),
        out_shape=jax.ShapeDtypeStruct((t, d), F32),
        compiler_params=_cparams(("parallel",)),
        name="combine",
    )(gate, xf, mod, final_w.reshape(1, d), yk.reshape(2, t, SLAB, d // SLAB))


def _pad_lanes(v, n=LANES):
    return jnp.pad(v, (0, n - v.shape[0])).reshape(1, n)


def kernel(x, c, ada_w, ada_b, norm1_w, w_in, gm_ln_w, gm_ln_b, gm_ws, gm_bs, conv_w, conv_b, dt_bias, a_log,
           d_skip, ssd_norm_w, w_out, norm2_w, router_w, router_b, exp_w_gate, exp_w_up, exp_w_down,
           final_norm_w):
    batch, seq, d = x.shape
    t = batch * seq
    depth = ada_w.shape[0]
    assert batch <= 8 and seq % CHUNK == 0 and w_in.shape[2] == MAIN_PROJ + SSD_HEADS
    n_rows = (-(-(t * 2) // MOE_BLOCK) + N_EXPERTS) * MOE_BLOCK
    assert n_rows // MOE_BLOCK <= META_LANES

    ada = _ada(jnp.pad(c, ((0, 8 - batch), (0, 0))), ada_w, ada_b)
    rw_t = router_w.T.astype(BF16)
    xf = x.reshape(t, d)
    for l in range(depth):
        mod = ada[l, :batch].reshape(batch, 6, 1, d)
        w_main = w_in[l, :, :MAIN_PROJ].astype(BF16)
        w_dt = jnp.pad(w_in[l, :, MAIN_PROJ:], ((0, 0), (0, LANES - SSD_HEADS))).astype(BF16)
        proj, dt_raw = _inproj(xf, norm1_w[l], mod, w_main, w_dt, seq)
        mixer_params = dict(
            lnw=gm_ln_w[l].reshape(1, GM_WIDTH), lnb=gm_ln_b[l].reshape(1, GM_WIDTH),
            ws=gm_ws[l], bst=gm_bs[l].T,
            cw=conv_w[l], cb=conv_b[l].reshape(1, CONV_DIM),
            dtb=_pad_lanes(dt_bias[l]), alog=_pad_lanes(a_log[l]),
            dsk=jnp.repeat(d_skip[l], SSD_WIDTH // SSD_HEADS).reshape(1, SSD_WIDTH),
            nw=ssd_norm_w[l].reshape(1, SSD_WIDTH))
        y_mix = _mixer(proj, dt_raw, mixer_params, batch, seq)
        xf, h2, eidx, gate, rank, cnt = _post(y_mix, w_out[l].astype(BF16), xf, mod, norm2_w[l], rw_t,
                                              router_b, seq)
        dest, meta = _meta(cnt, eidx, rank)
        n_blocks = n_rows // MOE_BLOCK
        tok_sorted, slot_sorted = _invert(dest, n_rows)
        yk = _experts(meta[ROW_BLK_E, :n_blocks], meta[ROW_NUSED, :1], meta[ROW_CNT, :N_EXPERTS],
                      meta[ROW_NVALID, :n_blocks], tok_sorted, slot_sorted, h2, exp_w_gate, exp_w_up,
                      exp_w_down, l)
        xf = _combine(gate, xf, mod, final_norm_w, yk, seq, final=(l == depth - 1))
    return xf.reshape(batch, seq, d)
```

```python
import functools

import jax
import jax.numpy as jnp
from jax import lax
from jax.experimental import pallas as pl
from jax.experimental.pallas import tpu as pltpu

F32 = jnp.float32
BF16 = jnp.bfloat16
I32 = jnp.int32

EPS = 1e-6
LANES = 128
CHUNK = 128
GM_HEADS = 8
GM_WIDTH = 1024
SSD_WIDTH = 1024
SSD_HEADS = 16
SSD_GROUPS = 2
SSD_STATE = 128
SSD_CONV = 4
CONV_DIM = SSD_WIDTH + 2 * SSD_GROUPS * SSD_STATE
MAIN_PROJ = 2 * GM_WIDTH + SSD_WIDTH + CONV_DIM
N_EXPERTS = 64
EXPERTS_PER_GROUP = 8
N_EXPERT_GROUPS = 8
MOE_BLOCK = 128
HALO = 8
SLAB = 8
VMEM_LIMIT = 56 * 1024 * 1024


def _cparams(sem, vmem=VMEM_LIMIT):
    return pltpu.CompilerParams(dimension_semantics=sem, vmem_limit_bytes=vmem)


def _silu(x):
    return x * jax.nn.sigmoid(x)


def _gelu(x):
    return 0.5 * x * (1.0 + lax.erf(x * 0.7071067811865476))


def _softplus(x):
    return jnp.maximum(x, 0.0) + jnp.log1p(jnp.exp(-jnp.abs(x)))


def _ada_kernel(c_ref, w_ref, b_ref, o_ref):
    sc = _silu(c_ref[...])
    o_ref[0] = jnp.dot(sc.astype(BF16), w_ref[0].astype(BF16), preferred_element_type=F32) + b_ref[0]


def _ada(c_pad, ada_w, ada_b):
    n_layers, d, n = ada_w.shape
    tn = 1024
    return pl.pallas_call(
        _ada_kernel,
        grid=(n_layers, n // tn),
        in_specs=[
            pl.BlockSpec((8, d), lambda l, j: (0, 0)),
            pl.BlockSpec((1, d, tn), lambda l, j: (l, 0, j)),
            pl.BlockSpec((1, 1, tn), lambda l, j: (l, 0, j)),
        ],
        out_specs=pl.BlockSpec((1, 8, tn), lambda l, j: (l, 0, j)),
        out_shape=jax.ShapeDtypeStruct((n_layers, 8, n), F32),
        compiler_params=_cparams(("parallel", "parallel")),
        name="ada",
    )(c_pad, ada_w, ada_b.reshape(n_layers, 1, n))


def _inproj_kernel(x_ref, nw_ref, s_ref, sh_ref, w_ref, win_hbm, o_ref, dt_ref, h_scr, wdt_scr, wnarrow_scr,
                   sem, *, layer):
    i = pl.program_id(0)
    j = pl.program_id(1)

    @pl.when((i == 0) & (j == 0))
    def _():
        cp = pltpu.make_async_copy(win_hbm.at[layer, :, pl.ds(MAIN_PROJ, SSD_HEADS)], wnarrow_scr, sem)
        cp.start()
        cp.wait()
        wdt_scr[...] = jnp.zeros(wdt_scr.shape, wdt_scr.dtype)
        wdt_scr[:, 0:SSD_HEADS] = wnarrow_scr[...]

    @pl.when(j == 0)
    def _():
        x = x_ref[...]
        ms = jnp.mean(x * x, axis=-1, keepdims=True)
        y = x * lax.rsqrt(ms + EPS) * nw_ref[...]
        h = (y * (1.0 + s_ref[...]) + sh_ref[...]).astype(BF16)
        h_scr[...] = h
        dt_ref[...] = jnp.dot(h, wdt_scr[...].astype(BF16), preferred_element_type=F32)

    o_ref[...] = jnp.dot(h_scr[...], w_ref[...].astype(BF16), preferred_element_type=F32)


def _inproj(xf, norm_w, mod, w_in, layer, seq):
    t, d = xf.shape
    tm = min(1024, seq)
    tn = 512
    per_batch = seq // tm
    return pl.pallas_call(
        functools.partial(_inproj_kernel, layer=layer),
        grid=(t // tm, MAIN_PROJ // tn),
        in_specs=[
            pl.BlockSpec((tm, d), lambda i, j: (i, 0)),
            pl.BlockSpec((1, d), lambda i, j: (0, 0)),
            pl.BlockSpec((None, None, 1, d), lambda i, j: (i // per_batch, 1, 0, 0)),
            pl.BlockSpec((None, None, 1, d), lambda i, j: (i // per_batch, 0, 0, 0)),
            pl.BlockSpec((None, d, tn), lambda i, j: (layer, 0, j)),
            pl.BlockSpec(memory_space=pl.ANY),
        ],
        out_specs=[
            pl.BlockSpec((tm, tn), lambda i, j: (i, j)),
            pl.BlockSpec((tm, LANES), lambda i, j: (i, 0)),
        ],
        out_shape=[
            jax.ShapeDtypeStruct((t, MAIN_PROJ), F32),
            jax.ShapeDtypeStruct((t, LANES), F32),
        ],
        scratch_shapes=[
            pltpu.VMEM((tm, d), BF16),
            pltpu.VMEM((d, LANES), F32),
            pltpu.VMEM((d, SSD_HEADS), F32),
            pltpu.SemaphoreType.DMA(()),
        ],
        compiler_params=_cparams(("arbitrary", "arbitrary")),
        name="inproj",
    )(xf, norm_w.reshape(1, d), mod, mod, w_in, w_in)


def _mixer_kernel(u_ref, v_ref, z_ref, xbc_ref, dt_ref,
                  lnw_ref, lnb_ref, ws_ref, bst_ref, cw_ref, cb_ref, dtb_ref, alog_ref,
                  dsk_ref, nw_ref, y_ref, buf_scr, xa_scr, state_scr, ys_scr):
    @pl.when(pl.program_id(1) == 0)
    def _():
        buf_scr[0:HALO, :] = jnp.zeros((HALO, CONV_DIM), F32)
        state_scr[...] = jnp.zeros(state_scr.shape, F32)

    row = lax.broadcasted_iota(I32, (CHUNK, CHUNK), 0)
    col = lax.broadcasted_iota(I32, (CHUNK, CHUNK), 1)
    tril = row >= col
    lane_lo = col < (LANES // 2)
    lane_lo_row = lane_lo[0:1, :]

    for h in range(GM_HEADS):
        sl = slice(h * LANES, (h + 1) * LANES)
        gu = _gelu(u_ref[:, sl])
        gv = _gelu(v_ref[:, sl])
        mu = jnp.mean(gv, axis=-1, keepdims=True)
        dv = gv - mu
        var = jnp.mean(dv * dv, axis=-1, keepdims=True)
        vn = dv * lax.rsqrt(var + EPS) * lnw_ref[:, sl] + lnb_ref[:, sl]
        w = jnp.where(tril, ws_ref[h], 0.0).astype(BF16)
        s = jnp.dot(w, vn.astype(BF16), preferred_element_type=F32) + bst_ref[:, h:h + 1]
        y_ref[:, sl] = (gu * s).astype(y_ref.dtype)

    buf_scr[HALO:HALO + CHUNK, :] = xbc_ref[...]
    for cb in range(CONV_DIM // 256):
        cs_ = slice(cb * 256, (cb + 1) * 256)
        acc = cb_ref[:, cs_] + cw_ref[0:1, cs_] * buf_scr[HALO - 3:HALO - 3 + CHUNK, cs_]
        for k in range(1, SSD_CONV):
            acc = acc + cw_ref[k:k + 1, cs_] * buf_scr[HALO - 3 + k:HALO - 3 + k + CHUNK, cs_]
        xa_scr[:, cs_] = _silu(acc)
    buf_scr[0:HALO, :] = buf_scr[CHUNK:CHUNK + HALO, :]

    dt = _softplus(dt_ref[...] + dtb_ref[...])
    a = -jnp.exp(alog_ref[...])
    ad = dt * a
    cs = jnp.dot(tril.astype(F32), ad, precision=lax.Precision.HIGHEST, preferred_element_type=F32)
    cs_t = cs.T
    last = cs[CHUNK - 1:CHUNK, :]
    ds = jnp.exp(last - cs)
    ecs = jnp.exp(cs)
    cd = jnp.exp(last)

    def pair_expand(m, p):
        return jnp.where(lane_lo[0:m.shape[0], :], m[:, 2 * p:2 * p + 1], m[:, 2 * p + 1:2 * p + 2])

    pairs_per_group = SSD_HEADS // SSD_GROUPS // 2
    gw = SSD_WIDTH // SSD_GROUPS
    for g in range(SSD_GROUPS):
        bm_g = xa_scr[:, SSD_WIDTH + g * SSD_STATE:SSD_WIDTH + (g + 1) * SSD_STATE]
        cm_g = xa_scr[:, SSD_WIDTH + (SSD_GROUPS + g) * SSD_STATE:SSD_WIDTH + (SSD_GROUPS + g + 1) * SSD_STATE]
        cmb = cm_g.astype(BF16)
        bmb = bm_g.astype(BF16)
        cbm = lax.dot_general(cmb, bmb, (((1,), (1,)), ((), ())), preferred_element_type=F32)
        bm_t = bm_g.T.astype(BF16)
        st_prev = state_scr[:, g * gw:(g + 1) * gw]
        y_off = jnp.dot(cmb, st_prev.astype(BF16), preferred_element_type=F32)
        xdds = []
        cds = []
        for q in range(pairs_per_group):
            p = g * pairs_per_group + q
            sl = slice(p * LANES, (p + 1) * LANES)
            xs_p = xa_scr[:, sl]
            xd = xs_p * pair_expand(dt, p)
            xdb = xd.astype(BF16)
            ys = []
            for hh in (2 * p, 2 * p + 1):
                diff = cs[:, hh:hh + 1] - cs_t[hh:hh + 1, :]
                lm = jnp.where(tril, jnp.exp(jnp.where(tril, diff, 0.0)), 0.0)
                wmat = (cbm * lm).astype(BF16)
                ys.append(jnp.dot(wmat, xdb, preferred_element_type=F32))
            y_diag = jnp.where(lane_lo, ys[0], ys[1])
            y = y_diag + y_off[:, q * LANES:(q + 1) * LANES] * pair_expand(ecs, p)
            ys_scr[:, sl] = y + dsk_ref[:, sl] * xs_p
            xdds.append((xd * pair_expand(ds, p)).astype(BF16))
            cds.append(jnp.where(lane_lo_row, cd[:, 2 * p:2 * p + 1], cd[:, 2 * p + 1:2 * p + 2]))
        st_new = jnp.dot(bm_t, jnp.concatenate(xdds, axis=1), preferred_element_type=F32)
        state_scr[:, g * gw:(g + 1) * gw] = st_prev * jnp.concatenate(cds, axis=1) + st_new

    for g in range(SSD_GROUPS):
        sl = slice(g * gw, (g + 1) * gw)
        gg = ys_scr[:, sl] * _silu(z_ref[:, sl])
        ms = jnp.mean(gg * gg, axis=-1, keepdims=True)
        y_ref[:, GM_WIDTH + g * gw:GM_WIDTH + (g + 1) * gw] = (
            gg * lax.rsqrt(ms + EPS) * nw_ref[:, sl]).astype(y_ref.dtype)


def _mixer(proj, dt_raw, p, batch, seq):
    t = proj.shape[0]
    nc = seq // CHUNK
    rows = lambda b, c: b * nc + c
    full = lambda shape: pl.BlockSpec(shape, lambda b, c: (0,) * len(shape))
    return pl.pallas_call(
        _mixer_kernel,
        grid=(batch, nc),
        in_specs=[
            pl.BlockSpec((CHUNK, GM_WIDTH), lambda b, c: (rows(b, c), 0)),
            pl.BlockSpec((CHUNK, GM_WIDTH), lambda b, c: (rows(b, c), 1)),
            pl.BlockSpec((CHUNK, SSD_WIDTH), lambda b, c: (rows(b, c), 2)),
            pl.BlockSpec((CHUNK, CONV_DIM), lambda b, c: (rows(b, c), 2)),
            pl.BlockSpec((CHUNK, LANES), lambda b, c: (rows(b, c), 0)),
            full((1, GM_WIDTH)), full((1, GM_WIDTH)),
            full((GM_HEADS, CHUNK, CHUNK)), full((CHUNK, GM_HEADS)),
            full((SSD_CONV, CONV_DIM)), full((1, CONV_DIM)),
            full((1, LANES)), full((1, LANES)),
            full((1, SSD_WIDTH)), full((1, SSD_WIDTH)),
        ],
        out_specs=pl.BlockSpec((CHUNK, GM_WIDTH + SSD_WIDTH), lambda b, c: (rows(b, c), 0)),
        out_shape=jax.ShapeDtypeStruct((t, GM_WIDTH + SSD_WIDTH), BF16),
        scratch_shapes=[
            pltpu.VMEM((HALO + CHUNK, CONV_DIM), F32),
            pltpu.VMEM((CHUNK, CONV_DIM), F32),
            pltpu.VMEM((SSD_STATE, SSD_WIDTH), F32),
            pltpu.VMEM((CHUNK, SSD_WIDTH), F32),
        ],
        compiler_params=_cparams(("parallel", "arbitrary")),
        name="mixer",
    )(proj, proj, proj, proj, dt_raw,
      p["lnw"], p["lnb"], p["ws"], p["bst"], p["cw"], p["cb"], p["dtb"], p["alog"], p["dsk"], p["nw"])


def _first_max(vals, axis_iota, n):
    m = jnp.max(vals, axis=0, keepdims=True)
    idx = jnp.min(jnp.where(vals == m, axis_iota, n), axis=0, keepdims=True)
    return m, idx


def _post_kernel(y_ref, wout_ref, x_ref, g1_ref, n2w_ref, s2_ref, sh2_ref, rwt_ref, rb_ref,
                 xo_ref, h2_ref, eidx_ref, gate_ref, rank_ref, cnt_ref, carry_scr):
    @pl.when(pl.program_id(0) == 0)
    def _():
        carry_scr[...] = jnp.zeros(carry_scr.shape, F32)

    tm = x_ref.shape[0]
    mix = jnp.dot(y_ref[...], wout_ref[...], preferred_element_type=F32)
    x = x_ref[...] + g1_ref[...] * mix
    xo_ref[...] = x
    ms = jnp.mean(x * x, axis=-1, keepdims=True)
    h = x * lax.rsqrt(ms + EPS) * n2w_ref[...] * (1.0 + s2_ref[...]) + sh2_ref[...]
    sw = h2_ref.shape[2]
    for s in range(SLAB):
        h2_ref[:, s, :] = h[:, s * sw:(s + 1) * sw]

    logits_t = lax.dot_general(rwt_ref[...], h.astype(BF16), (((1,), (1,)), ((), ())),
                               preferred_element_type=F32)
    scores = jax.nn.sigmoid(logits_t)
    biased = scores + rb_ref[...]

    sub = lax.broadcasted_iota(I32, (EXPERTS_PER_GROUP, tm), 0)
    neg = jnp.float32(-jnp.inf)
    best = None
    for g in range(N_EXPERT_GROUPS):
        grp = biased[g * EXPERTS_PER_GROUP:(g + 1) * EXPERTS_PER_GROUP, :]
        m1, i1 = _first_max(grp, sub, EXPERTS_PER_GROUP)
        m2, i2 = _first_max(jnp.where(sub == i1, neg, grp), sub, EXPERTS_PER_GROUP)
        gs = m1 + m2
        if best is None:
            best, bi, l1, l2 = gs, jnp.zeros((1, tm), I32), i1, i2
        else:
            upd = gs > best
            best = jnp.where(upd, gs, best)
            bi = jnp.where(upd, g, bi)
            l1 = jnp.where(upd, i1, l1)
            l2 = jnp.where(upd, i2, l2)
    e0 = bi * EXPERTS_PER_GROUP + l1
    e1 = bi * EXPERTS_PER_GROUP + l2

    eio = lax.broadcasted_iota(I32, (N_EXPERTS, tm), 0)
    oh0 = eio == e0
    oh1 = eio == e1
    s0 = jnp.sum(jnp.where(oh0, scores, 0.0), axis=0, keepdims=True)
    s1 = jnp.sum(jnp.where(oh1, scores, 0.0), axis=0, keepdims=True)
    tot = s0 + s1
    eidx_ref[0:1, :] = e0
    eidx_ref[1:2, :] = e1
    gate_ref[0:1, :] = s0 / tot
    gate_ref[1:2, :] = s1 / tot

    ohs = oh0.astype(F32) + oh1.astype(F32)
    tr = lax.broadcasted_iota(I32, (tm, tm), 0)
    tc = lax.broadcasted_iota(I32, (tm, tm), 1)
    before = (tr < tc).astype(BF16)
    prefix = jnp.dot(ohs.astype(BF16), before, preferred_element_type=F32)
    base = carry_scr[:, 0:1] + prefix
    rank_ref[0:1, :] = jnp.sum(jnp.where(oh0, base, 0.0), axis=0, keepdims=True).astype(I32)
    rank_ref[1:2, :] = jnp.sum(jnp.where(oh1, base, 0.0), axis=0, keepdims=True).astype(I32)
    carry_scr[...] = carry_scr[...] + jnp.sum(ohs, axis=1, keepdims=True)
    cnt_ref[...] = carry_scr[...]


def _post(y_mix, w_out, xf, mod, norm2_w, rw_t, rb, seq):
    t, d = xf.shape
    dm = y_mix.shape[1]
    tm = min(256, seq)
    per_batch = seq // tm
    modspec = lambda k: pl.BlockSpec((None, None, 1, d), lambda i: (i // per_batch, k, 0, 0))
    tok = pl.BlockSpec((2, tm), lambda i: (0, i))
    return pl.pallas_call(
        _post_kernel,
        grid=(t // tm,),
        in_specs=[
            pl.BlockSpec((tm, dm), lambda i: (i, 0)),
            pl.BlockSpec((dm, d), lambda i: (0, 0)),
            pl.BlockSpec((tm, d), lambda i: (i, 0)),
            modspec(2),
            pl.BlockSpec((1, d), lambda i: (0, 0)),
            modspec(4),
            modspec(3),
            pl.BlockSpec((N_EXPERTS, d), lambda i: (0, 0)),
            pl.BlockSpec((N_EXPERTS, 1), lambda i: (0, 0)),
        ],
        out_specs=[
            pl.BlockSpec((tm, d), lambda i: (i, 0)),
            pl.BlockSpec((tm, SLAB, d // SLAB), lambda i: (i, 0, 0)),
            tok, tok, tok,
            pl.BlockSpec((N_EXPERTS, LANES), lambda i: (0, 0)),
        ],
        out_shape=[
            jax.ShapeDtypeStruct((t, d), F32),
            jax.ShapeDtypeStruct((t, SLAB, d // SLAB), F32),
            jax.ShapeDtypeStruct((2, t), I32),
            jax.ShapeDtypeStruct((2, t), F32),
            jax.ShapeDtypeStruct((2, t), I32),
            jax.ShapeDtypeStruct((N_EXPERTS, LANES), F32),
        ],
        scratch_shapes=[pltpu.VMEM((N_EXPERTS, LANES), F32)],
        compiler_params=_cparams(("arbitrary",)),
        name="post",
    )(y_mix, w_out, xf, mod, norm2_w.reshape(1, d), mod, mod, rw_t, rb.reshape(N_EXPERTS, 1))


META_ROWS = 8
META_LANES = 256
ROW_BLK_E, ROW_CNT, ROW_PSTART, ROW_NUSED, ROW_NVALID = 0, 1, 2, 3, 4


def _col_to_row(colv):
    n = colv.shape[0]
    r = lax.broadcasted_iota(I32, (n, n), 0)
    c = lax.broadcasted_iota(I32, (n, n), 1)
    return jnp.sum(jnp.where(r == c, colv, 0.0), axis=0, keepdims=True)


def _meta_kernel(cnt_ref, eidx_ref, rank_ref, dest_ref, meta_ref):
    t = eidx_ref.shape[1]
    cnt = cnt_ref[...]
    nblk = jnp.floor((cnt + (MOE_BLOCK - 1)) * (1.0 / MOE_BLOCK))
    r = lax.broadcasted_iota(I32, (N_EXPERTS, N_EXPERTS), 0)
    c = lax.broadcasted_iota(I32, (N_EXPERTS, N_EXPERTS), 1)
    lower = (c < r).astype(BF16)
    pstart = jnp.dot(lower, nblk.astype(BF16), preferred_element_type=F32)
    pend = pstart + nblk

    chunk = min(1024, t)
    for j in range(t // chunk):
        sl = slice(j * chunk, (j + 1) * chunk)
        eio = lax.broadcasted_iota(I32, (N_EXPERTS, chunk), 0)
        for k in range(2):
            oh = eio == eidx_ref[k:k + 1, sl]
            ps = jnp.sum(jnp.where(oh, pstart[:, 0:1], 0.0), axis=0, keepdims=True)
            dest_ref[k:k + 1, sl] = (ps * MOE_BLOCK).astype(I32) + rank_ref[k:k + 1, sl]

    bl = lax.broadcasted_iota(I32, (N_EXPERTS, META_LANES), 1).astype(F32)
    raw = jnp.sum((pend[:, 0:1] <= bl).astype(F32), axis=0, keepdims=True)
    raw = jnp.minimum(raw, N_EXPERTS - 1.0)
    nused = pend[N_EXPERTS - 1:N_EXPERTS, 0:1]
    used = bl[0:1, :] < nused
    last_e = jnp.max(jnp.where(used, raw, 0.0), axis=1, keepdims=True)
    meta_ref[...] = jnp.zeros(meta_ref.shape, I32)
    meta_ref[ROW_BLK_E:ROW_BLK_E + 1, :] = jnp.where(used, raw, last_e).astype(I32)
    meta_ref[ROW_CNT:ROW_CNT + 1, 0:N_EXPERTS] = _col_to_row(cnt[:, 0:1]).astype(I32)
    meta_ref[ROW_PSTART:ROW_PSTART + 1, 0:N_EXPERTS] = (_col_to_row(pstart[:, 0:1]) * MOE_BLOCK).astype(I32)
    meta_ref[ROW_NUSED:ROW_NUSED + 1, :] = jnp.broadcast_to(nused, (1, META_LANES)).astype(I32)
    mine = lax.broadcasted_iota(I32, (N_EXPERTS, META_LANES), 0).astype(F32) == raw
    cnt_b = jnp.sum(jnp.where(mine, cnt[:, 0:1], 0.0), axis=0, keepdims=True)
    first_b = jnp.sum(jnp.where(mine, pstart[:, 0:1], 0.0), axis=0, keepdims=True)
    nvalid = jnp.clip(cnt_b - (bl[0:1, :] - first_b) * MOE_BLOCK, 0.0, float(MOE_BLOCK))
    meta_ref[ROW_NVALID:ROW_NVALID + 1, :] = jnp.where(used, nvalid, 0.0).astype(I32)


def _meta(cnt, eidx, rank):
    t = eidx.shape[1]
    full = lambda shape: pl.BlockSpec(shape, lambda: (0,) * len(shape))
    return pl.pallas_call(
        _meta_kernel,
        in_specs=[full((N_EXPERTS, LANES)), full((2, t)), full((2, t))],
        out_specs=[full((2, t)), full((META_ROWS, META_LANES))],
        out_shape=[jax.ShapeDtypeStruct((2, t), I32), jax.ShapeDtypeStruct((META_ROWS, META_LANES), I32)],
        name="meta",
    )(cnt, eidx, rank)


def _invert_kernel(dest_ref, tok_ref, slot_ref):
    i = pl.program_id(0)
    tm = dest_ref.shape[1]
    t = tm * pl.num_programs(0)

    @pl.when(i == 0)
    def _():
        def clear(p, carry):
            tok_ref[p] = 0
            slot_ref[p] = 0
            return carry
        lax.fori_loop(0, slot_ref.shape[0], clear, 0, unroll=8)

    def put(tok, carry):
        for k in range(2):
            p = dest_ref[k, tok]
            tok_ref[p] = i * tm + tok
            slot_ref[p] = k * t + i * tm + tok
        return carry
    lax.fori_loop(0, tm, put, 0, unroll=8)


def _invert(dest, n_rows):
    t = dest.shape[1]
    tm = min(1024, t)
    table = pl.BlockSpec((n_rows,), lambda i: (0,), memory_space=pltpu.SMEM)
    return pl.pallas_call(
        _invert_kernel,
        grid=(t // tm,),
        in_specs=[pl.BlockSpec((2, tm), lambda i: (0, i), memory_space=pltpu.SMEM)],
        out_specs=[table, table],
        out_shape=[jax.ShapeDtypeStruct((n_rows,), I32), jax.ShapeDtypeStruct((n_rows,), I32)],
        compiler_params=_cparams(("arbitrary",)),
        name="invert",
    )(dest)


def _expert_kernel(be_sm, nu_sm, cnt_sm, nv_sm, tokc_ref, tokn_ref, slotc_ref, h2_hbm, wg_hbm, wu_hbm, wd_hbm,
                   yk_hbm, xbuf, obuf, land_g, land_u, land_d, wg_scr, wu_scr, wd_scr, sems, gsem, ssem,
                   st_sm, *, layer):
    b = pl.program_id(0)
    nu = nu_sm[0]
    par = b & 1

    def gather_row(ids_ref, r, slot):
        return pltpu.make_async_copy(h2_hbm.at[ids_ref[0, r]], xbuf.at[slot, r], gsem.at[slot])

    def scatter_row(r, slot):
        return pltpu.make_async_copy(obuf.at[slot, r], yk_hbm.at[slotc_ref[0, r]], ssem.at[slot])

    def gather_block(slot):
        return pltpu.make_async_copy(h2_hbm.at[pl.ds(0, MOE_BLOCK)], xbuf.at[slot], gsem.at[slot])

    def scatter_block(slot):
        return pltpu.make_async_copy(obuf.at[slot], yk_hbm.at[pl.ds(0, MOE_BLOCK)], ssem.at[slot])

    def start_rows(n, row_copy):
        @pl.when(n == MOE_BLOCK)
        def _():
            lax.fori_loop(0, MOE_BLOCK, lambda r, c: (row_copy(r).start(), c)[1], 0, unroll=8)

        @pl.when(n != MOE_BLOCK)
        def _():
            lax.fori_loop(0, n, lambda r, c: (row_copy(r).start(), c)[1], 0)

    def wait_rows(n, row_copy, block_copy):
        @pl.when(n == MOE_BLOCK)
        def _():
            block_copy.wait()

        @pl.when(n != MOE_BLOCK)
        def _():
            lax.fori_loop(0, n, lambda r, c: (row_copy(r).wait(), c)[1], 0)

    def fetch(e, slot):
        return (pltpu.make_async_copy(wg_hbm.at[layer, e], land_g.at[slot], sems.at[slot, 0]),
                pltpu.make_async_copy(wu_hbm.at[layer, e], land_u.at[slot], sems.at[slot, 1]),
                pltpu.make_async_copy(wd_hbm.at[layer, e], land_d.at[slot], sems.at[slot, 2]))

    def next_used(e):
        return lax.while_loop(lambda n: (n < N_EXPERTS) & (cnt_sm[jnp.minimum(n, N_EXPERTS - 1)] == 0),
                              lambda n: n + 1, e + 1)

    def start_next(slot):
        nxt = next_used(st_sm[1])

        @pl.when(nxt < N_EXPERTS)
        def _():
            for cp in fetch(nxt, slot):
                cp.start(priority=1)
        st_sm[1] = nxt

    @pl.when(b == 0)
    def _():
        st_sm[0] = 0
        st_sm[1] = -1
        start_next(0)
        start_next(1)
        xbuf[...] = jnp.zeros(xbuf.shape, xbuf.dtype)
        start_rows(nv_sm[0], lambda r: gather_row(tokc_ref, r, 0))

    @pl.when(b < nu)
    def _():
        @pl.when(b + 1 < nu)
        def _():
            start_rows(nv_sm[b + 1], lambda r: gather_row(tokn_ref, r, 1 - par))

        prev = be_sm[jnp.maximum(b - 1, 0)]

        @pl.when((b == 0) | (be_sm[b] != prev))
        def _():
            slot = st_sm[0] & 1
            for cp in fetch(be_sm[b], slot):
                cp.wait()
            wg_scr[...] = land_g[slot].astype(BF16)
            wu_scr[...] = land_u[slot].astype(BF16)
            wd_scr[...] = land_d[slot].astype(BF16)
            st_sm[0] = st_sm[0] + 1
            start_next(slot)

        wait_rows(nv_sm[b], lambda r: gather_row(tokc_ref, r, par), gather_block(par))
        sw = xbuf.shape[3]
        xb = jnp.concatenate([xbuf[par, :, s, :] for s in range(SLAB)], axis=1).astype(BF16)
        gate = jnp.dot(xb, wg_scr[...], preferred_element_type=F32)
        up = jnp.dot(xb, wu_scr[...], preferred_element_type=F32)
        act = (_silu(gate) * up).astype(BF16)
        out = jnp.dot(act, wd_scr[...], preferred_element_type=F32)

        @pl.when(b >= 2)
        def _():
            wait_rows(nv_sm[jnp.maximum(b - 2, 0)], lambda r: scatter_row(r, par), scatter_block(par))

        for s in range(SLAB):
            obuf[par, :, s, :] = out[:, s * sw:(s + 1) * sw]
        start_rows(nv_sm[b], lambda r: scatter_row(r, par))

        @pl.when(b == nu - 1)
        def _():
            @pl.when(b >= 1)
            def _():
                wait_rows(nv_sm[jnp.maximum(b - 1, 0)], lambda r: scatter_row(r, 1 - par),
                          scatter_block(1 - par))
            wait_rows(nv_sm[b], lambda r: scatter_row(r, par), scatter_block(par))


def _experts(blk_e, nused, cnt_row, nvalid, tok_sorted, slot_sorted, h2, w_gate, w_up, w_down, layer):
    t, _, sw = h2.shape
    d = SLAB * sw
    de = w_gate.shape[3]
    n_blocks = slot_sorted.shape[0] // MOE_BLOCK
    toks = tok_sorted.reshape(n_blocks, 1, MOE_BLOCK)
    slots = slot_sorted.reshape(n_blocks, 1, MOE_BLOCK)
    hbm = pl.BlockSpec(memory_space=pl.ANY)
    id_spec = lambda ahead: pl.BlockSpec(
        (None, 1, MOE_BLOCK), lambda b, be, nu, cnt, nv: (jnp.minimum(b + ahead, nu[0] - 1), 0, 0),
        memory_space=pltpu.SMEM)
    return pl.pallas_call(
        functools.partial(_expert_kernel, layer=layer),
        grid_spec=pltpu.PrefetchScalarGridSpec(
            num_scalar_prefetch=4,
            grid=(n_blocks,),
            in_specs=[id_spec(0), id_spec(1), id_spec(0), hbm, hbm, hbm, hbm],
            out_specs=hbm,
            scratch_shapes=[
                pltpu.VMEM((2, MOE_BLOCK, SLAB, sw), F32),
                pltpu.VMEM((2, MOE_BLOCK, SLAB, sw), F32),
                pltpu.VMEM((2, d, de), F32),
                pltpu.VMEM((2, d, de), F32),
                pltpu.VMEM((2, de, d), F32),
                pltpu.VMEM((d, de), BF16),
                pltpu.VMEM((d, de), BF16),
                pltpu.VMEM((de, d), BF16),
                pltpu.SemaphoreType.DMA((2, 3)),
                pltpu.SemaphoreType.DMA((2,)),
                pltpu.SemaphoreType.DMA((2,)),
                pltpu.SMEM((2,), I32),
            ],
        ),
        out_shape=jax.ShapeDtypeStruct((2 * t, SLAB, sw), F32),
        compiler_params=_cparams(("arbitrary",)),
        name="experts",
    )(blk_e, nused, cnt_row, nvalid, toks, toks, slots, h2, w_gate, w_up, w_down)


def _combine_kernel(gate_ref, x_ref, g2_ref, fw_ref, yk_ref, out_ref, *, final):
    tm = x_ref.shape[0]
    r = lax.broadcasted_iota(I32, (tm, tm), 0)
    c = lax.broadcasted_iota(I32, (tm, tm), 1)
    eye = r == c
    w0 = jnp.sum(jnp.where(eye, gate_ref[0:1, :], 0.0), axis=1, keepdims=True)
    w1 = jnp.sum(jnp.where(eye, gate_ref[1:2, :], 0.0), axis=1, keepdims=True)
    y = jnp.concatenate([yk_ref[0, :, s, :] * w0 + yk_ref[1, :, s, :] * w1 for s in range(SLAB)], axis=1)
    x = x_ref[...] + g2_ref[...] * y
    if final:
        ms = jnp.mean(x * x, axis=-1, keepdims=True)
        x = x * lax.rsqrt(ms + EPS) * fw_ref[...]
    out_ref[...] = x


def _combine(gate, xf, mod, final_w, yk, seq, final):
    t, d = xf.shape
    tm = min(256, seq)
    per_batch = seq // tm
    return pl.pallas_call(
        functools.partial(_combine_kernel, final=final),
        grid=(t // tm,),
        in_specs=[
            pl.BlockSpec((2, tm), lambda i: (0, i)),
            pl.BlockSpec((tm, d), lambda i: (i, 0)),
            pl.BlockSpec((None, None, 1, d), lambda i: (i // per_batch, 5, 0, 0)),
            pl.BlockSpec((1, d), lambda i: (0, 0)),
            pl.BlockSpec((2, tm, SLAB, d // SLAB), lambda i: (0, i, 0, 0)),
        ],
        out_specs=pl.BlockSpec((tm, d), lambda i: (i, 0)),
        out_shape=jax.ShapeDtypeStruct((t, d), F32),
        compiler_params=_cparams(("parallel",)),
        name="combine",
    )(gate, xf, mod, final_w.reshape(1, d), yk.reshape(2, t, SLAB, d // SLAB))


def _pad_lanes(v, n=LANES):
    return jnp.pad(v, (0, n - v.shape[0])).reshape(1, n)


def kernel(x, c, ada_w, ada_b, norm1_w, w_in, gm_ln_w, gm_ln_b, gm_ws, gm_bs, conv_w, conv_b, dt_bias, a_log,
           d_skip, ssd_norm_w, w_out, norm2_w, router_w, router_b, exp_w_gate, exp_w_up, exp_w_down,
           final_norm_w):
    batch, seq, d = x.shape
    t = batch * seq
    depth = ada_w.shape[0]
    assert batch <= 8 and seq % CHUNK == 0 and w_in.shape[2] == MAIN_PROJ + SSD_HEADS
    n_rows = (-(-(t * 2) // MOE_BLOCK) + N_EXPERTS) * MOE_BLOCK
    assert n_rows // MOE_BLOCK <= META_LANES

    ada = _ada(jnp.pad(c, ((0, 8 - batch), (0, 0))), ada_w, ada_b)
    rw_t = router_w.T.astype(BF16)
    xf = x.reshape(t, d)
    for l in range(depth):
        mod = ada[l, :batch].reshape(batch, 6, 1, d)
        proj, dt_raw = _inproj(xf, norm1_w[l], mod, w_in, l, seq)
        mixer_params = dict(
            lnw=gm_ln_w[l].reshape(1, GM_WIDTH), lnb=gm_ln_b[l].reshape(1, GM_WIDTH),
            ws=gm_ws[l], bst=gm_bs[l].T,
            cw=conv_w[l], cb=conv_b[l].reshape(1, CONV_DIM),
            dtb=_pad_lanes(dt_bias[l]), alog=_pad_lanes(a_log[l]),
            dsk=jnp.repeat(d_skip[l], SSD_WIDTH // SSD_HEADS).reshape(1, SSD_WIDTH),
            nw=ssd_norm_w[l].reshape(1, SSD_WIDTH))
        y_mix = _mixer(proj, dt_raw, mixer_params, batch, seq)
        xf, h2, eidx, gate, rank, cnt = _post(y_mix, w_out[l].astype(BF16), xf, mod, norm2_w[l], rw_t,
                                              router_b, seq)
        dest, meta = _meta(cnt, eidx, rank)
        n_blocks = n_rows // MOE_BLOCK
        tok_sorted, slot_sorted = _invert(dest, n_rows)
        yk = _experts(meta[ROW_BLK_E, :n_blocks], meta[ROW_NUSED, :1], meta[ROW_CNT, :N_EXPERTS],
                      meta[ROW_NVALID, :n_blocks], tok_sorted, slot_sorted, h2, exp_w_gate, exp_w_up,
                      exp_w_down, l)
        xf = _combine(gate, xf, mod, final_norm_w, yk, seq, final=(l == depth - 1))
    return xf.reshape(batch, seq, d)
```

```python
import functools

import jax
import jax.numpy as jnp
from jax import lax
from jax.experimental import pallas as pl
from jax.experimental.pallas import tpu as pltpu

F32 = jnp.float32
BF16 = jnp.bfloat16
I32 = jnp.int32

EPS = 1e-6
LANES = 128
CHUNK = 128
GM_HEADS = 8
GM_WIDTH = 1024
SSD_WIDTH = 1024
SSD_HEADS = 16
SSD_GROUPS = 2
SSD_STATE = 128
SSD_CONV = 4
CONV_DIM = SSD_WIDTH + 2 * SSD_GROUPS * SSD_STATE
MAIN_PROJ = 2 * GM_WIDTH + SSD_WIDTH + CONV_DIM
N_EXPERTS = 64
EXPERTS_PER_GROUP = 8
N_EXPERT_GROUPS = 8
MOE_BLOCK = 128
HALO = 8
SLAB = 16
VMEM_PITCH = 24
SUBLANES = 8
WEIGHT_QUEUE = 1
GATHER_QUEUES = (0,)
GATHER_AHEAD = 2
SCATTER_QUEUES = (0, 1)
VMEM_LIMIT = 56 * 1024 * 1024


def _cparams(sem, vmem=VMEM_LIMIT):
    return pltpu.CompilerParams(dimension_semantics=sem, vmem_limit_bytes=vmem)


def _silu(x):
    return x * (0.5 * (1.0 + jnp.tanh(0.5 * x)))


def _gelu(x):
    return 0.5 * x * (1.0 + lax.erf(x * 0.7071067811865476))


def _softplus(x):
    return jnp.maximum(x, 0.0) + jnp.log1p(jnp.exp(-jnp.abs(x)))


def _store_slab_rows(ref, base, x, pitch=SLAB):
    n = x.shape[0]
    w = ref.shape[1]
    for s in range(SLAB):
        ref[pl.ds(base + s, n, stride=pitch), :] = x[:, s * w:(s + 1) * w]


def _load_slab_rows(ref, base, n, pitch=SLAB):
    return [ref[pl.ds(base + s, n, stride=pitch), :] for s in range(SLAB)]


def _ada_kernel(c_ref, w_ref, b_ref, o_ref):
    sc = _silu(c_ref[...])
    o_ref[0] = jnp.dot(sc.astype(BF16), w_ref[0].astype(BF16), preferred_element_type=F32) + b_ref[0]


def _ada(c_pad, ada_w, ada_b):
    n_layers, d, n = ada_w.shape
    tn = 1024
    return pl.pallas_call(
        _ada_kernel,
        grid=(n_layers, n // tn),
        in_specs=[
            pl.BlockSpec((8, d), lambda l, j: (0, 0)),
            pl.BlockSpec((1, d, tn), lambda l, j: (l, 0, j)),
            pl.BlockSpec((1, 1, tn), lambda l, j: (l, 0, j)),
        ],
        out_specs=pl.BlockSpec((1, 8, tn), lambda l, j: (l, 0, j)),
        out_shape=jax.ShapeDtypeStruct((n_layers, 8, n), F32),
        compiler_params=_cparams(("parallel", "parallel")),
        name="ada",
    )(c_pad, ada_w, ada_b.reshape(n_layers, 1, n))


_NT = (((1,), (1,)), ((), ()))


def _inproj_kernel(x_ref, nw_ref, s_ref, sh_ref, wt_ref, wdt_ref, o_ref, dt_ref, h_scr, wdt_scr):
    @pl.when(pl.program_id(1) == 0)
    def _():
        x = x_ref[...]
        ms = jnp.mean(x * x, axis=-1, keepdims=True)
        y = x * lax.rsqrt(ms + EPS) * nw_ref[...]
        h = (y * (1.0 + s_ref[...]) + sh_ref[...]).astype(BF16)
        h_scr[...] = h
        wdt_scr[...] = jnp.zeros(wdt_scr.shape, wdt_scr.dtype)
        wdt_scr[0:SSD_HEADS, :] = wdt_ref[...].astype(BF16)
        dt_ref[...] = lax.dot_general(h, wdt_scr[...], _NT, preferred_element_type=F32)

    o_ref[...] = lax.dot_general(h_scr[...], wt_ref[...].astype(BF16), _NT, preferred_element_type=F32)


def _inproj(xf, norm_w, mod, w_in_t, layer, seq):
    t, d = xf.shape
    tm = min(1024, seq)
    tn = 512
    per_batch = seq // tm
    return pl.pallas_call(
        _inproj_kernel,
        grid=(t // tm, MAIN_PROJ // tn),
        in_specs=[
            pl.BlockSpec((tm, d), lambda i, j: (i, 0)),
            pl.BlockSpec((1, d), lambda i, j: (0, 0)),
            pl.BlockSpec((None, None, 1, d), lambda i, j: (i // per_batch, 1, 0, 0)),
            pl.BlockSpec((None, None, 1, d), lambda i, j: (i // per_batch, 0, 0, 0)),
            pl.BlockSpec((None, tn, d), lambda i, j: (layer, j, 0)),
            pl.BlockSpec((None, SSD_HEADS, d), lambda i, j: (layer, MAIN_PROJ // SSD_HEADS, 0)),
        ],
        out_specs=[
            pl.BlockSpec((tm, tn), lambda i, j: (i, j)),
            pl.BlockSpec((tm, LANES), lambda i, j: (i, 0)),
        ],
        out_shape=[
            jax.ShapeDtypeStruct((t, MAIN_PROJ), F32),
            jax.ShapeDtypeStruct((t, LANES), F32),
        ],
        scratch_shapes=[
            pltpu.VMEM((tm, d), BF16),
            pltpu.VMEM((LANES, d), BF16),
        ],
        compiler_params=_cparams(("parallel", "arbitrary")),
        name="inproj",
    )(xf, norm_w.reshape(1, d), mod, mod, w_in_t, w_in_t)


def _mixer_kernel(u_ref, v_ref, z_ref, xbc_ref, dt_ref,
                  lnw_ref, lnb_ref, ws_ref, bst_ref, cw_ref, cb_ref, dtb_ref, alog_ref,
                  dsk_ref, nw_ref, y_ref, buf_scr, xa_scr, state_scr, ys_scr):
    @pl.when(pl.program_id(1) == 0)
    def _():
        buf_scr[0:HALO, :] = jnp.zeros((HALO, CONV_DIM), F32)
        state_scr[...] = jnp.zeros(state_scr.shape, F32)

    row = lax.broadcasted_iota(I32, (CHUNK, CHUNK), 0)
    col = lax.broadcasted_iota(I32, (CHUNK, CHUNK), 1)
    tril = row >= col
    lane_lo = col < (LANES // 2)
    lane_lo_row = lane_lo[0:1, :]

    for h in range(GM_HEADS):
        sl = slice(h * LANES, (h + 1) * LANES)
        gu = _gelu(u_ref[:, sl])
        gv = _gelu(v_ref[:, sl])
        mu = jnp.mean(gv, axis=-1, keepdims=True)
        dv = gv - mu
        var = jnp.mean(dv * dv, axis=-1, keepdims=True)
        vn = dv * lax.rsqrt(var + EPS) * lnw_ref[:, sl] + lnb_ref[:, sl]
        w = jnp.where(tril, ws_ref[h], 0.0).astype(BF16)
        s = jnp.dot(w, vn.astype(BF16), preferred_element_type=F32) + bst_ref[:, h:h + 1]
        y_ref[:, sl] = (gu * s).astype(y_ref.dtype)

    buf_scr[HALO:HALO + CHUNK, :] = xbc_ref[...]
    for cb in range(CONV_DIM // 256):
        cs_ = slice(cb * 256, (cb + 1) * 256)
        acc = cb_ref[:, cs_] + cw_ref[0:1, cs_] * buf_scr[HALO - 3:HALO - 3 + CHUNK, cs_]
        for k in range(1, SSD_CONV):
            acc = acc + cw_ref[k:k + 1, cs_] * buf_scr[HALO - 3 + k:HALO - 3 + k + CHUNK, cs_]
        xa_scr[:, cs_] = _silu(acc)
    buf_scr[0:HALO, :] = buf_scr[CHUNK:CHUNK + HALO, :]

    dt = _softplus(dt_ref[...] + dtb_ref[...])
    a = -jnp.exp(alog_ref[...])
    ad = dt * a
    cs = jnp.dot(tril.astype(F32), ad, precision=lax.Precision.HIGHEST, preferred_element_type=F32)
    cs_t = cs.T
    last = cs[CHUNK - 1:CHUNK, :]
    ds = jnp.exp(last - cs)
    ecs = jnp.exp(cs)
    cd = jnp.exp(last)

    def pair_expand(m, p):
        return jnp.where(lane_lo[0:m.shape[0], :], m[:, 2 * p:2 * p + 1], m[:, 2 * p + 1:2 * p + 2])

    pairs_per_group = SSD_HEADS // SSD_GROUPS // 2
    gw = SSD_WIDTH // SSD_GROUPS
    for g in range(SSD_GROUPS):
        bm_g = xa_scr[:, SSD_WIDTH + g * SSD_STATE:SSD_WIDTH + (g + 1) * SSD_STATE]
        cm_g = xa_scr[:, SSD_WIDTH + (SSD_GROUPS + g) * SSD_STATE:SSD_WIDTH + (SSD_GROUPS + g + 1) * SSD_STATE]
        cmb = cm_g.astype(BF16)
        bmb = bm_g.astype(BF16)
        cbm = lax.dot_general(cmb, bmb, (((1,), (1,)), ((), ())), preferred_element_type=F32)
        bm_t = bm_g.T.astype(BF16)
        st_prev = state_scr[:, g * gw:(g + 1) * gw]
        y_off = jnp.dot(cmb, st_prev.astype(BF16), preferred_element_type=F32)
        xdds = []
        cds = []
        for q in range(pairs_per_group):
            p = g * pairs_per_group + q
            sl = slice(p * LANES, (p + 1) * LANES)
            xs_p = xa_scr[:, sl]
            xd = xs_p * pair_expand(dt, p)
            xdb = xd.astype(BF16)
            ys = []
            for hh in (2 * p, 2 * p + 1):
                diff = cs[:, hh:hh + 1] - cs_t[hh:hh + 1, :]
                lm = jnp.where(tril, jnp.exp(jnp.where(tril, diff, 0.0)), 0.0)
                wmat = (cbm * lm).astype(BF16)
                ys.append(jnp.dot(wmat, xdb, preferred_element_type=F32))
            y_diag = jnp.where(lane_lo, ys[0], ys[1])
            y = y_diag + y_off[:, q * LANES:(q + 1) * LANES] * pair_expand(ecs, p)
            ys_scr[:, sl] = y + dsk_ref[:, sl] * xs_p
            xdds.append((xd * pair_expand(ds, p)).astype(BF16))
            cds.append(jnp.where(lane_lo_row, cd[:, 2 * p:2 * p + 1], cd[:, 2 * p + 1:2 * p + 2]))
        st_new = jnp.dot(bm_t, jnp.concatenate(xdds, axis=1), preferred_element_type=F32)
        state_scr[:, g * gw:(g + 1) * gw] = st_prev * jnp.concatenate(cds, axis=1) + st_new

    for g in range(SSD_GROUPS):
        sl = slice(g * gw, (g + 1) * gw)
        gg = ys_scr[:, sl] * _silu(z_ref[:, sl])
        ms = jnp.mean(gg * gg, axis=-1, keepdims=True)
        y_ref[:, GM_WIDTH + g * gw:GM_WIDTH + (g + 1) * gw] = (
            gg * lax.rsqrt(ms + EPS) * nw_ref[:, sl]).astype(y_ref.dtype)


def _mixer(proj, dt_raw, p, batch, seq):
    t = proj.shape[0]
    nc = seq // CHUNK
    rows = lambda b, c: b * nc + c
    full = lambda shape: pl.BlockSpec(shape, lambda b, c: (0,) * len(shape))
    return pl.pallas_call(
        _mixer_kernel,
        grid=(batch, nc),
        in_specs=[
            pl.BlockSpec((CHUNK, GM_WIDTH), lambda b, c: (rows(b, c), 0)),
            pl.BlockSpec((CHUNK, GM_WIDTH), lambda b, c: (rows(b, c), 1)),
            pl.BlockSpec((CHUNK, SSD_WIDTH), lambda b, c: (rows(b, c), 2)),
            pl.BlockSpec((CHUNK, CONV_DIM), lambda b, c: (rows(b, c), 2)),
            pl.BlockSpec((CHUNK, LANES), lambda b, c: (rows(b, c), 0)),
            full((1, GM_WIDTH)), full((1, GM_WIDTH)),
            full((GM_HEADS, CHUNK, CHUNK)), full((CHUNK, GM_HEADS)),
            full((SSD_CONV, CONV_DIM)), full((1, CONV_DIM)),
            full((1, LANES)), full((1, LANES)),
            full((1, SSD_WIDTH)), full((1, SSD_WIDTH)),
        ],
        out_specs=pl.BlockSpec((CHUNK, GM_WIDTH + SSD_WIDTH), lambda b, c: (rows(b, c), 0)),
        out_shape=jax.ShapeDtypeStruct((t, GM_WIDTH + SSD_WIDTH), BF16),
        scratch_shapes=[
            pltpu.VMEM((HALO + CHUNK, CONV_DIM), F32),
            pltpu.VMEM((CHUNK, CONV_DIM), F32),
            pltpu.VMEM((SSD_STATE, SSD_WIDTH), F32),
            pltpu.VMEM((CHUNK, SSD_WIDTH), F32),
        ],
        compiler_params=_cparams(("parallel", "arbitrary")),
        name="mixer",
    )(proj, proj, proj, proj, dt_raw,
      p["lnw"], p["lnb"], p["ws"], p["bst"], p["cw"], p["cb"], p["dtb"], p["alog"], p["dsk"], p["nw"])


def _first_max(vals, axis_iota, n):
    m = jnp.max(vals, axis=0, keepdims=True)
    idx = jnp.min(jnp.where(vals == m, axis_iota, n), axis=0, keepdims=True)
    return m, idx


def _post_kernel(y_ref, wout_ref, x_ref, g1_ref, n2w_ref, s2_ref, sh2_ref, rwt_ref, rb_ref,
                 xo_ref, h2_ref, eidx_ref, gate_ref, rank_ref, cnt_ref, carry_scr):
    @pl.when(pl.program_id(0) == 0)
    def _():
        carry_scr[...] = jnp.zeros(carry_scr.shape, F32)

    tm = x_ref.shape[0]
    mix = jnp.dot(y_ref[...], wout_ref[...], preferred_element_type=F32)
    x = x_ref[...] + g1_ref[...] * mix
    xo_ref[...] = x
    ms = jnp.mean(x * x, axis=-1, keepdims=True)
    h = x * lax.rsqrt(ms + EPS) * n2w_ref[...] * (1.0 + s2_ref[...]) + sh2_ref[...]
    hb = h.astype(BF16)
    _store_slab_rows(h2_ref, 0, h)

    logits_t = lax.dot_general(rwt_ref[...], hb, (((1,), (1,)), ((), ())),
                               preferred_element_type=F32)
    scores = jax.nn.sigmoid(logits_t)
    biased = scores + rb_ref[...]

    sub = lax.broadcasted_iota(I32, (EXPERTS_PER_GROUP, tm), 0)
    neg = jnp.float32(-jnp.inf)
    best = None
    for g in range(N_EXPERT_GROUPS):
        grp = biased[g * EXPERTS_PER_GROUP:(g + 1) * EXPERTS_PER_GROUP, :]
        m1, i1 = _first_max(grp, sub, EXPERTS_PER_GROUP)
        m2, i2 = _first_max(jnp.where(sub == i1, neg, grp), sub, EXPERTS_PER_GROUP)
        gs = m1 + m2
        if best is None:
            best, bi, l1, l2 = gs, jnp.zeros((1, tm), I32), i1, i2
        else:
            upd = gs > best
            best = jnp.where(upd, gs, best)
            bi = jnp.where(upd, g, bi)
            l1 = jnp.where(upd, i1, l1)
            l2 = jnp.where(upd, i2, l2)
    e0 = bi * EXPERTS_PER_GROUP + l1
    e1 = bi * EXPERTS_PER_GROUP + l2

    eio = lax.broadcasted_iota(I32, (N_EXPERTS, tm), 0)
    oh0 = eio == e0
    oh1 = eio == e1
    s0 = jnp.sum(jnp.where(oh0, scores, 0.0), axis=0, keepdims=True)
    s1 = jnp.sum(jnp.where(oh1, scores, 0.0), axis=0, keepdims=True)
    tot = s0 + s1
    eidx_ref[0:1, :] = e0
    eidx_ref[1:2, :] = e1
    gate_ref[0:1, :] = s0 / tot
    gate_ref[1:2, :] = s1 / tot

    ohs = oh0.astype(F32) + oh1.astype(F32)
    tr = lax.broadcasted_iota(I32, (tm, tm), 0)
    tc = lax.broadcasted_iota(I32, (tm, tm), 1)
    before = (tr < tc).astype(BF16)
    prefix = jnp.dot(ohs.astype(BF16), before, preferred_element_type=F32)
    base = carry_scr[:, 0:1] + prefix
    rank_ref[0:1, :] = jnp.sum(jnp.where(oh0, base, 0.0), axis=0, keepdims=True).astype(I32)
    rank_ref[1:2, :] = jnp.sum(jnp.where(oh1, base, 0.0), axis=0, keepdims=True).astype(I32)
    carry_scr[...] = carry_scr[...] + jnp.sum(ohs, axis=1, keepdims=True)
    cnt_ref[...] = carry_scr[...]


def _post(y_mix, w_out, xf, mod, norm2_w, rw_t, rb, seq):
    t, d = xf.shape
    dm = y_mix.shape[1]
    tm = min(256, seq)
    per_batch = seq // tm
    modspec = lambda k: pl.BlockSpec((None, None, 1, d), lambda i: (i // per_batch, k, 0, 0))
    tok = pl.BlockSpec((2, tm), lambda i: (0, i))
    return pl.pallas_call(
        _post_kernel,
        grid=(t // tm,),
        in_specs=[
            pl.BlockSpec((tm, dm), lambda i: (i, 0)),
            pl.BlockSpec((dm, d), lambda i: (0, 0)),
            pl.BlockSpec((tm, d), lambda i: (i, 0)),
            modspec(2),
            pl.BlockSpec((1, d), lambda i: (0, 0)),
            modspec(4),
            modspec(3),
            pl.BlockSpec((N_EXPERTS, d), lambda i: (0, 0)),
            pl.BlockSpec((N_EXPERTS, 1), lambda i: (0, 0)),
        ],
        out_specs=[
            pl.BlockSpec((tm, d), lambda i: (i, 0)),
            pl.BlockSpec((tm * SLAB, d // SLAB), lambda i: (i, 0)),
            tok, tok, tok,
            pl.BlockSpec((N_EXPERTS, LANES), lambda i: (0, 0)),
        ],
        out_shape=[
            jax.ShapeDtypeStruct((t, d), F32),
            jax.ShapeDtypeStruct((t * SLAB, d // SLAB), F32),
            jax.ShapeDtypeStruct((2, t), I32),
            jax.ShapeDtypeStruct((2, t), F32),
            jax.ShapeDtypeStruct((2, t), I32),
            jax.ShapeDtypeStruct((N_EXPERTS, LANES), F32),
        ],
        scratch_shapes=[pltpu.VMEM((N_EXPERTS, LANES), F32)],
        compiler_params=_cparams(("arbitrary",)),
        name="post",
    )(y_mix, w_out, xf, mod, norm2_w.reshape(1, d), mod, mod, rw_t, rb.reshape(N_EXPERTS, 1))


META_ROWS = 8
META_LANES = 256
ROW_BLK_E, ROW_CNT, ROW_PSTART, ROW_NUSED, ROW_NVALID = 0, 1, 2, 3, 4


def _col_to_row(colv):
    n = colv.shape[0]
    r = lax.broadcasted_iota(I32, (n, n), 0)
    c = lax.broadcasted_iota(I32, (n, n), 1)
    return jnp.sum(jnp.where(r == c, colv, 0.0), axis=0, keepdims=True)


def _meta_kernel(cnt_ref, eidx_ref, rank_ref, dest_ref, meta_ref):
    t = eidx_ref.shape[1]
    cnt = cnt_ref[...]
    nblk = jnp.floor((cnt + (MOE_BLOCK - 1)) * (1.0 / MOE_BLOCK))
    r = lax.broadcasted_iota(I32, (N_EXPERTS, N_EXPERTS), 0)
    c = lax.broadcasted_iota(I32, (N_EXPERTS, N_EXPERTS), 1)
    lower = (c < r).astype(BF16)
    pstart = jnp.dot(lower, nblk.astype(BF16), preferred_element_type=F32)
    pend = pstart + nblk

    chunk = min(1024, t)
    for j in range(t // chunk):
        sl = slice(j * chunk, (j + 1) * chunk)
        eio = lax.broadcasted_iota(I32, (N_EXPERTS, chunk), 0)
        for k in range(2):
            oh = eio == eidx_ref[k:k + 1, sl]
            ps = jnp.sum(jnp.where(oh, pstart[:, 0:1], 0.0), axis=0, keepdims=True)
            dest_ref[k:k + 1, sl] = (ps * MOE_BLOCK).astype(I32) + rank_ref[k:k + 1, sl]

    bl = lax.broadcasted_iota(I32, (N_EXPERTS, META_LANES), 1).astype(F32)
    raw = jnp.sum((pend[:, 0:1] <= bl).astype(F32), axis=0, keepdims=True)
    raw = jnp.minimum(raw, N_EXPERTS - 1.0)
    nused = pend[N_EXPERTS - 1:N_EXPERTS, 0:1]
    used = bl[0:1, :] < nused
    last_e = jnp.max(jnp.where(used, raw, 0.0), axis=1, keepdims=True)
    meta_ref[...] = jnp.zeros(meta_ref.shape, I32)
    meta_ref[ROW_BLK_E:ROW_BLK_E + 1, :] = jnp.where(used, raw, last_e).astype(I32)
    meta_ref[ROW_CNT:ROW_CNT + 1, 0:N_EXPERTS] = _col_to_row(cnt[:, 0:1]).astype(I32)
    meta_ref[ROW_PSTART:ROW_PSTART + 1, 0:N_EXPERTS] = (_col_to_row(pstart[:, 0:1]) * MOE_BLOCK).astype(I32)
    meta_ref[ROW_NUSED:ROW_NUSED + 1, :] = jnp.broadcast_to(nused, (1, META_LANES)).astype(I32)
    mine = lax.broadcasted_iota(I32, (N_EXPERTS, META_LANES), 0).astype(F32) == raw
    cnt_b = jnp.sum(jnp.where(mine, cnt[:, 0:1], 0.0), axis=0, keepdims=True)
    first_b = jnp.sum(jnp.where(mine, pstart[:, 0:1], 0.0), axis=0, keepdims=True)
    nvalid = jnp.clip(cnt_b - (bl[0:1, :] - first_b) * MOE_BLOCK, 0.0, float(MOE_BLOCK))
    meta_ref[ROW_NVALID:ROW_NVALID + 1, :] = jnp.where(used, nvalid, 0.0).astype(I32)


def _meta(cnt, eidx, rank):
    t = eidx.shape[1]
    full = lambda shape: pl.BlockSpec(shape, lambda: (0,) * len(shape))
    return pl.pallas_call(
        _meta_kernel,
        in_specs=[full((N_EXPERTS, LANES)), full((2, t)), full((2, t))],
        out_specs=[full((2, t)), full((META_ROWS, META_LANES))],
        out_shape=[jax.ShapeDtypeStruct((2, t), I32), jax.ShapeDtypeStruct((META_ROWS, META_LANES), I32)],
        name="meta",
    )(cnt, eidx, rank)


def _invert_kernel(dest_ref, code_ref):
    i = pl.program_id(0)
    tm = dest_ref.shape[1]

    @pl.when(i == 0)
    def _():
        def clear(p, carry):
            code_ref[p] = 0
            return carry
        lax.fori_loop(0, code_ref.shape[0], clear, 0, unroll=16)

    def put(tok, carry):
        for k in range(2):
            code_ref[dest_ref[k, tok]] = (i * tm + tok) * 2 + k
        return carry
    lax.fori_loop(0, tm, put, 0, unroll=8)


def _invert(dest, n_rows):
    t = dest.shape[1]
    tm = min(1024, t)
    return pl.pallas_call(
        _invert_kernel,
        grid=(t // tm,),
        in_specs=[pl.BlockSpec((2, tm), lambda i: (0, i), memory_space=pltpu.SMEM)],
        out_specs=pl.BlockSpec((n_rows,), lambda i: (0,), memory_space=pltpu.SMEM),
        out_shape=jax.ShapeDtypeStruct((n_rows,), I32),
        compiler_params=_cparams(("arbitrary",)),
        name="invert",
    )(dest)


def _expert_kernel(be_sm, nu_sm, cnt_sm, nv_sm, code0_ref, code1_ref, code2_ref, h2_hbm, wg_hbm, wu_hbm, wd_hbm,
                   yk_hbm, xbuf, obuf, land_g, land_u, land_d, wg_scr, wu_scr, wd_scr, sems, gsem, ssem,
                   st_sm, *, layer):
    b = pl.program_id(0)
    nu = nu_sm[0]
    par = b & 1
    t = h2_hbm.shape[0] // SLAB
    n_blocks = pl.num_programs(0)
    code_refs = (code0_ref, code1_ref, code2_ref)
    blk_rows = MOE_BLOCK * VMEM_PITCH

    def hbm_rows(ref, index, n=1):
        return ref.at[pl.ds(pl.multiple_of(index * SLAB, SUBLANES), n * SLAB)]

    def vmem_row(ref, slot, r):
        return ref.at[pl.ds(pl.multiple_of((slot * MOE_BLOCK + r) * VMEM_PITCH, SUBLANES), SLAB)]

    def vmem_span(ref, slot):
        return ref.at[pl.ds(pl.multiple_of(slot * blk_rows, SUBLANES), MOE_BLOCK * SLAB)]

    def gather_row(ahead, r, slot):
        return pltpu.make_async_copy(hbm_rows(h2_hbm, code_refs[ahead][0, r] >> 1), vmem_row(xbuf, slot, r),
                                     gsem.at[slot])

    def scatter_row(r, slot):
        code = code0_ref[0, r]
        return pltpu.make_async_copy(vmem_row(obuf, slot, r), hbm_rows(yk_hbm, (code & 1) * t + (code >> 1)),
                                     ssem.at[slot])

    def gather_block(slot):
        return pltpu.make_async_copy(hbm_rows(h2_hbm, 0, MOE_BLOCK), vmem_span(xbuf, slot), gsem.at[slot])

    def scatter_block(slot):
        return pltpu.make_async_copy(vmem_span(obuf, slot), hbm_rows(yk_hbm, 0, MOE_BLOCK), ssem.at[slot])

    def start_rows(n, row_copy, queues):
        @pl.when(n == MOE_BLOCK)
        def _():
            def eight(g, c):
                for u in range(SUBLANES):
                    row_copy(g * SUBLANES + u).start(priority=queues[u % len(queues)])
                return c
            lax.fori_loop(0, MOE_BLOCK // SUBLANES, eight, 0)

        @pl.when(n != MOE_BLOCK)
        def _():
            lax.fori_loop(0, n, lambda r, c: (row_copy(r).start(priority=queues[0]), c)[1], 0)

    def wait_rows(n, row_copy, block_copy):
        @pl.when(n == MOE_BLOCK)
        def _():
            block_copy.wait()

        @pl.when(n != MOE_BLOCK)
        def _():
            lax.fori_loop(0, n, lambda r, c: (row_copy(r).wait(), c)[1], 0)

    def fetch(e, slot):
        return (pltpu.make_async_copy(wg_hbm.at[layer, e], land_g.at[slot], sems.at[slot, 0]),
                pltpu.make_async_copy(wu_hbm.at[layer, e], land_u.at[slot], sems.at[slot, 1]),
                pltpu.make_async_copy(wd_hbm.at[layer, e], land_d.at[slot], sems.at[slot, 2]))

    def next_used(e):
        return lax.while_loop(lambda n: (n < N_EXPERTS) & (cnt_sm[jnp.minimum(n, N_EXPERTS - 1)] == 0),
                              lambda n: n + 1, e + 1)

    def start_next(slot):
        nxt = next_used(st_sm[1])

        @pl.when(nxt < N_EXPERTS)
        def _():
            for cp in fetch(nxt, slot):
                cp.start(priority=WEIGHT_QUEUE)
        st_sm[1] = nxt

    @pl.when(b == 0)
    def _():
        st_sm[0] = 0
        st_sm[1] = -1
        start_next(0)
        start_next(1)
        xbuf[...] = jnp.zeros(xbuf.shape, xbuf.dtype)
        for ahead in range(GATHER_AHEAD):
            @pl.when(ahead < nu)
            def _():
                start_rows(nv_sm[ahead], lambda r: gather_row(ahead, r, ahead), GATHER_QUEUES)

    xslot = lax.rem(b, GATHER_AHEAD + 1)

    @pl.when(b < nu)
    def _():
        @pl.when(b + GATHER_AHEAD < nu)
        def _():
            start_rows(nv_sm[jnp.minimum(b + GATHER_AHEAD, n_blocks - 1)],
                       lambda r: gather_row(GATHER_AHEAD, r, lax.rem(b + GATHER_AHEAD, GATHER_AHEAD + 1)),
                       GATHER_QUEUES)

        prev = be_sm[jnp.maximum(b - 1, 0)]

        @pl.when((b == 0) | (be_sm[b] != prev))
        def _():
            slot = st_sm[0] & 1
            for cp in fetch(be_sm[b], slot):
                cp.wait()
            wg_scr[...] = land_g[slot].astype(BF16)
            wu_scr[...] = land_u[slot].astype(BF16)
            wd_scr[...] = land_d[slot].astype(BF16)
            st_sm[0] = st_sm[0] + 1
            start_next(slot)

        wait_rows(nv_sm[b], lambda r: gather_row(0, r, xslot), gather_block(xslot))
        xb = jnp.concatenate(_load_slab_rows(xbuf, xslot * blk_rows, MOE_BLOCK, VMEM_PITCH),
                             axis=1).astype(BF16)
        gate = jnp.dot(xb, wg_scr[...], preferred_element_type=F32)
        up = jnp.dot(xb, wu_scr[...], preferred_element_type=F32)
        act = (_silu(gate) * up).astype(BF16)
        out = jnp.dot(act, wd_scr[...], preferred_element_type=F32)

        @pl.when(b >= 2)
        def _():
            wait_rows(nv_sm[jnp.maximum(b - 2, 0)], lambda r: scatter_row(r, par), scatter_block(par))

        _store_slab_rows(obuf, par * blk_rows, out, VMEM_PITCH)
        start_rows(nv_sm[b], lambda r: scatter_row(r, par), SCATTER_QUEUES)

        @pl.when(b == nu - 1)
        def _():
            @pl.when(b >= 1)
            def _():
                wait_rows(nv_sm[jnp.maximum(b - 1, 0)], lambda r: scatter_row(r, 1 - par),
                          scatter_block(1 - par))
            wait_rows(nv_sm[b], lambda r: scatter_row(r, par), scatter_block(par))


def _experts(blk_e, nused, cnt_row, nvalid, code_sorted, h2, w_gate, w_up, w_down, layer):
    sw = h2.shape[1]
    t = h2.shape[0] // SLAB
    d = SLAB * sw
    de = w_gate.shape[3]
    n_blocks = code_sorted.shape[0] // MOE_BLOCK
    codes = code_sorted.reshape(n_blocks, 1, MOE_BLOCK)
    hbm = pl.BlockSpec(memory_space=pl.ANY)
    id_spec = lambda ahead: pl.BlockSpec(
        (None, 1, MOE_BLOCK), lambda b, be, nu, cnt, nv: (jnp.minimum(b + ahead, nu[0] - 1), 0, 0),
        memory_space=pltpu.SMEM)
    return pl.pallas_call(
        functools.partial(_expert_kernel, layer=layer),
        grid_spec=pltpu.PrefetchScalarGridSpec(
            num_scalar_prefetch=4,
            grid=(n_blocks,),
            in_specs=[id_spec(ahead) for ahead in range(GATHER_AHEAD + 1)] + [hbm, hbm, hbm, hbm],
            out_specs=hbm,
            scratch_shapes=[
                pltpu.VMEM(((GATHER_AHEAD + 1) * MOE_BLOCK * VMEM_PITCH, sw), F32),
                pltpu.VMEM((2 * MOE_BLOCK * VMEM_PITCH, sw), F32),
                pltpu.VMEM((2, d, de), F32),
                pltpu.VMEM((2, d, de), F32),
                pltpu.VMEM((2, de, d), F32),
                pltpu.VMEM((d, de), BF16),
                pltpu.VMEM((d, de), BF16),
                pltpu.VMEM((de, d), BF16),
                pltpu.SemaphoreType.DMA((2, 3)),
                pltpu.SemaphoreType.DMA((GATHER_AHEAD + 1,)),
                pltpu.SemaphoreType.DMA((2,)),
                pltpu.SMEM((2,), I32),
            ],
        ),
        out_shape=jax.ShapeDtypeStruct((2 * t * SLAB, sw), F32),
        compiler_params=_cparams(("arbitrary",)),
        name="experts",
    )(blk_e, nused, cnt_row, nvalid, codes, codes, codes, h2, w_gate, w_up, w_down)


def _combine_kernel(gate_ref, x_ref, g2_ref, fw_ref, y0_ref, y1_ref, out_ref, *, final):
    tm = x_ref.shape[0]
    r = lax.broadcasted_iota(I32, (tm, tm), 0)
    c = lax.broadcasted_iota(I32, (tm, tm), 1)
    eye = r == c
    w0 = jnp.sum(jnp.where(eye, gate_ref[0:1, :], 0.0), axis=1, keepdims=True)
    w1 = jnp.sum(jnp.where(eye, gate_ref[1:2, :], 0.0), axis=1, keepdims=True)
    y = jnp.concatenate([c0 * w0 + c1 * w1 for c0, c1 in zip(_load_slab_rows(y0_ref, 0, tm),
                                                             _load_slab_rows(y1_ref, 0, tm))], axis=1)
    x = x_ref[...] + g2_ref[...] * y
    if final:
        ms = jnp.mean(x * x, axis=-1, keepdims=True)
        x = x * lax.rsqrt(ms + EPS) * fw_ref[...]
    out_ref[...] = x


def _combine(gate, xf, mod, final_w, yk, seq, final):
    t, d = xf.shape
    tm = min(256, seq)
    per_batch = seq // tm
    return pl.pallas_call(
        functools.partial(_combine_kernel, final=final),
        grid=(t // tm,),
        in_specs=[
            pl.BlockSpec((2, tm), lambda i: (0, i)),
            pl.BlockSpec((tm, d), lambda i: (i, 0)),
            pl.BlockSpec((None, None, 1, d), lambda i: (i // per_batch, 5, 0, 0)),
            pl.BlockSpec((1, d), lambda i: (0, 0)),
            pl.BlockSpec((tm * SLAB, yk.shape[1]), lambda i: (i, 0)),
            pl.BlockSpec((tm * SLAB, yk.shape[1]), lambda i: (i + t // tm, 0)),
        ],
        out_specs=pl.BlockSpec((tm, d), lambda i: (i, 0)),
        out_shape=jax.ShapeDtypeStruct((t, d), F32),
        compiler_params=_cparams(("parallel",)),
        name="combine",
    )(gate, xf, mod, final_w.reshape(1, d), yk, yk)


def _pad_lanes(v, n=LANES):
    return jnp.pad(v, (0, n - v.shape[0])).reshape(1, n)


def kernel(x, c, ada_w, ada_b, norm1_w, w_in, gm_ln_w, gm_ln_b, gm_ws, gm_bs, conv_w, conv_b, dt_bias, a_log,
           d_skip, ssd_norm_w, w_out, norm2_w, router_w, router_b, exp_w_gate, exp_w_up, exp_w_down,
           final_norm_w):
    batch, seq, d = x.shape
    t = batch * seq
    depth = ada_w.shape[0]
    assert batch <= 8 and seq % CHUNK == 0 and w_in.shape[2] == MAIN_PROJ + SSD_HEADS
    n_rows = (-(-(t * 2) // MOE_BLOCK) + N_EXPERTS) * MOE_BLOCK
    assert n_rows // MOE_BLOCK <= META_LANES

    ada = _ada(jnp.pad(c, ((0, 8 - batch), (0, 0))), ada_w, ada_b)
    rw_t = router_w.T.astype(BF16)
    w_in_t = jnp.swapaxes(w_in, 1, 2)
    xf = x.reshape(t, d)
    for l in range(depth):
        mod = ada[l, :batch].reshape(batch, 6, 1, d)
        proj, dt_raw = _inproj(xf, norm1_w[l], mod, w_in_t, l, seq)
        mixer_params = dict(
            lnw=gm_ln_w[l].reshape(1, GM_WIDTH), lnb=gm_ln_b[l].reshape(1, GM_WIDTH),
            ws=gm_ws[l], bst=gm_bs[l].T,
            cw=conv_w[l], cb=conv_b[l].reshape(1, CONV_DIM),
            dtb=_pad_lanes(dt_bias[l]), alog=_pad_lanes(a_log[l]),
            dsk=jnp.repeat(d_skip[l], SSD_WIDTH // SSD_HEADS).reshape(1, SSD_WIDTH),
            nw=ssd_norm_w[l].reshape(1, SSD_WIDTH))
        y_mix = _mixer(proj, dt_raw, mixer_params, batch, seq)
        xf, h2, eidx, gate, rank, cnt = _post(y_mix, w_out[l].astype(BF16), xf, mod, norm2_w[l], rw_t,
                                              router_b, seq)
        dest, meta = _meta(cnt, eidx, rank)
        n_blocks = n_rows // MOE_BLOCK
        yk = _experts(meta[ROW_BLK_E, :n_blocks], meta[ROW_NUSED, :1], meta[ROW_CNT, :N_EXPERTS],
                      meta[ROW_NVALID, :n_blocks], _invert(dest, n_rows), h2, exp_w_gate, exp_w_up,
                      exp_w_down, l)
        xf = _combine(gate, xf, mod, final_norm_w, yk, seq, final=(l == depth - 1))
    return xf.reshape(batch, seq, d)
```

```python
import functools

import jax
import jax.numpy as jnp
from jax import lax
from jax.experimental import pallas as pl
from jax.experimental.pallas import tpu as pltpu

F32 = jnp.float32
BF16 = jnp.bfloat16
I32 = jnp.int32

EPS = 1e-6
LANES = 128
CHUNK = 128
GM_HEADS = 8
GM_WIDTH = 1024
SSD_WIDTH = 1024
SSD_HEADS = 16
SSD_GROUPS = 2
SSD_STATE = 128
SSD_CONV = 4
CONV_DIM = SSD_WIDTH + 2 * SSD_GROUPS * SSD_STATE
MAIN_PROJ = 2 * GM_WIDTH + SSD_WIDTH + CONV_DIM
N_EXPERTS = 64
EXPERTS_PER_GROUP = 8
N_EXPERT_GROUPS = 8
MOE_BLOCK = 128
HALO = 8
SLAB = 16
VMEM_PITCH = 24
SUBLANES = 8
WEIGHT_QUEUE = 1
GATHER_QUEUES = (0,)
GATHER_AHEAD = 1
SCATTER_QUEUES = (0, 1)
VMEM_LIMIT = 56 * 1024 * 1024


def _cparams(sem, vmem=VMEM_LIMIT):
    return pltpu.CompilerParams(dimension_semantics=sem, vmem_limit_bytes=vmem)


def _silu(x):
    return x * (0.5 * (1.0 + jnp.tanh(0.5 * x)))


def _gelu(x):
    return 0.5 * x * (1.0 + lax.erf(x * 0.7071067811865476))


def _softplus(x):
    return jnp.maximum(x, 0.0) + jnp.log1p(jnp.exp(-jnp.abs(x)))


def _store_slab_rows(ref, base, x, pitch=SLAB):
    n = x.shape[0]
    w = ref.shape[1]
    for s in range(SLAB):
        ref[pl.ds(base + s, n, stride=pitch), :] = x[:, s * w:(s + 1) * w]


def _load_slab_rows(ref, base, n, pitch=SLAB):
    return [ref[pl.ds(base + s, n, stride=pitch), :] for s in range(SLAB)]


def _ada_kernel(c_ref, w_ref, b_ref, o_ref):
    sc = _silu(c_ref[...])
    o_ref[0] = jnp.dot(sc.astype(BF16), w_ref[0].astype(BF16), preferred_element_type=F32) + b_ref[0]


def _ada(c_pad, ada_w, ada_b):
    n_layers, d, n = ada_w.shape
    tn = 1024
    return pl.pallas_call(
        _ada_kernel,
        grid=(n_layers, n // tn),
        in_specs=[
            pl.BlockSpec((8, d), lambda l, j: (0, 0)),
            pl.BlockSpec((1, d, tn), lambda l, j: (l, 0, j)),
            pl.BlockSpec((1, 1, tn), lambda l, j: (l, 0, j)),
        ],
        out_specs=pl.BlockSpec((1, 8, tn), lambda l, j: (l, 0, j)),
        out_shape=jax.ShapeDtypeStruct((n_layers, 8, n), F32),
        compiler_params=_cparams(("parallel", "parallel")),
        name="ada",
    )(c_pad, ada_w, ada_b.reshape(n_layers, 1, n))


_NT = (((1,), (1,)), ((), ()))


def _inproj_kernel(x_ref, nw_ref, s_ref, sh_ref, wt_ref, wdt_ref, o_ref, dt_ref, h_scr, wdt_scr):
    @pl.when(pl.program_id(1) == 0)
    def _():
        x = x_ref[...]
        ms = jnp.mean(x * x, axis=-1, keepdims=True)
        y = x * lax.rsqrt(ms + EPS) * nw_ref[...]
        h = (y * (1.0 + s_ref[...]) + sh_ref[...]).astype(BF16)
        h_scr[...] = h
        wdt_scr[...] = jnp.zeros(wdt_scr.shape, wdt_scr.dtype)
        wdt_scr[0:SSD_HEADS, :] = wdt_ref[...].astype(BF16)
        dt_ref[...] = lax.dot_general(h, wdt_scr[...], _NT, preferred_element_type=F32)

    o_ref[...] = lax.dot_general(h_scr[...], wt_ref[...].astype(BF16), _NT, preferred_element_type=F32)


def _inproj(xf, norm_w, mod, w_in_t, layer, seq):
    t, d = xf.shape
    tm = min(1024, seq)
    tn = 512
    per_batch = seq // tm
    return pl.pallas_call(
        _inproj_kernel,
        grid=(t // tm, MAIN_PROJ // tn),
        in_specs=[
            pl.BlockSpec((tm, d), lambda i, j: (i, 0)),
            pl.BlockSpec((1, d), lambda i, j: (0, 0)),
            pl.BlockSpec((None, None, 1, d), lambda i, j: (i // per_batch, 1, 0, 0)),
            pl.BlockSpec((None, None, 1, d), lambda i, j: (i // per_batch, 0, 0, 0)),
            pl.BlockSpec((None, tn, d), lambda i, j: (layer, j, 0)),
            pl.BlockSpec((None, SSD_HEADS, d), lambda i, j: (layer, MAIN_PROJ // SSD_HEADS, 0)),
        ],
        out_specs=[
            pl.BlockSpec((tm, tn), lambda i, j: (i, j)),
            pl.BlockSpec((tm, LANES), lambda i, j: (i, 0)),
        ],
        out_shape=[
            jax.ShapeDtypeStruct((t, MAIN_PROJ), F32),
            jax.ShapeDtypeStruct((t, LANES), F32),
        ],
        scratch_shapes=[
            pltpu.VMEM((tm, d), BF16),
            pltpu.VMEM((LANES, d), BF16),
        ],
        compiler_params=_cparams(("parallel", "arbitrary")),
        name="inproj",
    )(xf, norm_w.reshape(1, d), mod, mod, w_in_t, w_in_t)


def _mixer_kernel(u_ref, v_ref, z_ref, xbc_ref, dt_ref,
                  lnw_ref, lnb_ref, ws_ref, bst_ref, cw_ref, cb_ref, dtb_ref, alog_ref,
                  dsk_ref, nw_ref, y_ref, buf_scr, xa_scr, state_scr, ys_scr):
    @pl.when(pl.program_id(1) == 0)
    def _():
        buf_scr[0:HALO, :] = jnp.zeros((HALO, CONV_DIM), F32)
        state_scr[...] = jnp.zeros(state_scr.shape, F32)

    row = lax.broadcasted_iota(I32, (CHUNK, CHUNK), 0)
    col = lax.broadcasted_iota(I32, (CHUNK, CHUNK), 1)
    tril = row >= col
    lane_lo = col < (LANES // 2)
    lane_lo_row = lane_lo[0:1, :]

    for h in range(GM_HEADS):
        sl = slice(h * LANES, (h + 1) * LANES)
        gu = _gelu(u_ref[:, sl])
        gv = _gelu(v_ref[:, sl])
        mu = jnp.mean(gv, axis=-1, keepdims=True)
        dv = gv - mu
        var = jnp.mean(dv * dv, axis=-1, keepdims=True)
        vn = dv * lax.rsqrt(var + EPS) * lnw_ref[:, sl] + lnb_ref[:, sl]
        w = jnp.where(tril, ws_ref[h], 0.0).astype(BF16)
        s = jnp.dot(w, vn.astype(BF16), preferred_element_type=F32) + bst_ref[:, h:h + 1]
        y_ref[:, sl] = (gu * s).astype(y_ref.dtype)

    buf_scr[HALO:HALO + CHUNK, :] = xbc_ref[...]
    for cb in range(CONV_DIM // 256):
        cs_ = slice(cb * 256, (cb + 1) * 256)
        acc = cb_ref[:, cs_] + cw_ref[0:1, cs_] * buf_scr[HALO - 3:HALO - 3 + CHUNK, cs_]
        for k in range(1, SSD_CONV):
            acc = acc + cw_ref[k:k + 1, cs_] * buf_scr[HALO - 3 + k:HALO - 3 + k + CHUNK, cs_]
        xa_scr[:, cs_] = _silu(acc)
    buf_scr[0:HALO, :] = buf_scr[CHUNK:CHUNK + HALO, :]

    dt = _softplus(dt_ref[...] + dtb_ref[...])
    a = -jnp.exp(alog_ref[...])
    ad = dt * a
    cs = jnp.dot(tril.astype(F32), ad, precision=lax.Precision.HIGHEST, preferred_element_type=F32)
    cs_t = cs.T
    last = cs[CHUNK - 1:CHUNK, :]
    ds = jnp.exp(last - cs)
    ecs = jnp.exp(cs)
    cd = jnp.exp(last)

    def pair_expand(m, p):
        return jnp.where(lane_lo[0:m.shape[0], :], m[:, 2 * p:2 * p + 1], m[:, 2 * p + 1:2 * p + 2])

    pairs_per_group = SSD_HEADS // SSD_GROUPS // 2
    gw = SSD_WIDTH // SSD_GROUPS
    for g in range(SSD_GROUPS):
        bm_g = xa_scr[:, SSD_WIDTH + g * SSD_STATE:SSD_WIDTH + (g + 1) * SSD_STATE]
        cm_g = xa_scr[:, SSD_WIDTH + (SSD_GROUPS + g) * SSD_STATE:SSD_WIDTH + (SSD_GROUPS + g + 1) * SSD_STATE]
        cmb = cm_g.astype(BF16)
        bmb = bm_g.astype(BF16)
        cbm = lax.dot_general(cmb, bmb, (((1,), (1,)), ((), ())), preferred_element_type=F32)
        bm_t = bm_g.T.astype(BF16)
        st_prev = state_scr[:, g * gw:(g + 1) * gw]
        y_off = jnp.dot(cmb, st_prev.astype(BF16), preferred_element_type=F32)
        xdds = []
        cds = []
        for q in range(pairs_per_group):
            p = g * pairs_per_group + q
            sl = slice(p * LANES, (p + 1) * LANES)
            xs_p = xa_scr[:, sl]
            xd = xs_p * pair_expand(dt, p)
            xdb = xd.astype(BF16)
            ys = []
            for hh in (2 * p, 2 * p + 1):
                diff = cs[:, hh:hh + 1] - cs_t[hh:hh + 1, :]
                lm = jnp.where(tril, jnp.exp(jnp.where(tril, diff, 0.0)), 0.0)
                wmat = (cbm * lm).astype(BF16)
                ys.append(jnp.dot(wmat, xdb, preferred_element_type=F32))
            y_diag = jnp.where(lane_lo, ys[0], ys[1])
            y = y_diag + y_off[:, q * LANES:(q + 1) * LANES] * pair_expand(ecs, p)
            ys_scr[:, sl] = y + dsk_ref[:, sl] * xs_p
            xdds.append((xd * pair_expand(ds, p)).astype(BF16))
            cds.append(jnp.where(lane_lo_row, cd[:, 2 * p:2 * p + 1], cd[:, 2 * p + 1:2 * p + 2]))
        st_new = jnp.dot(bm_t, jnp.concatenate(xdds, axis=1), preferred_element_type=F32)
        state_scr[:, g * gw:(g + 1) * gw] = st_prev * jnp.concatenate(cds, axis=1) + st_new

    for g in range(SSD_GROUPS):
        sl = slice(g * gw, (g + 1) * gw)
        gg = ys_scr[:, sl] * _silu(z_ref[:, sl])
        ms = jnp.mean(gg * gg, axis=-1, keepdims=True)
        y_ref[:, GM_WIDTH + g * gw:GM_WIDTH + (g + 1) * gw] = (
            gg * lax.rsqrt(ms + EPS) * nw_ref[:, sl]).astype(y_ref.dtype)


def _mixer(proj, dt_raw, p, batch, seq):
    t = proj.shape[0]
    nc = seq // CHUNK
    rows = lambda b, c: b * nc + c
    full = lambda shape: pl.BlockSpec(shape, lambda b, c: (0,) * len(shape))
    return pl.pallas_call(
        _mixer_kernel,
        grid=(batch, nc),
        in_specs=[
            pl.BlockSpec((CHUNK, GM_WIDTH), lambda b, c: (rows(b, c), 0)),
            pl.BlockSpec((CHUNK, GM_WIDTH), lambda b, c: (rows(b, c), 1)),
            pl.BlockSpec((CHUNK, SSD_WIDTH), lambda b, c: (rows(b, c), 2)),
            pl.BlockSpec((CHUNK, CONV_DIM), lambda b, c: (rows(b, c), 2)),
            pl.BlockSpec((CHUNK, LANES), lambda b, c: (rows(b, c), 0)),
            full((1, GM_WIDTH)), full((1, GM_WIDTH)),
            full((GM_HEADS, CHUNK, CHUNK)), full((CHUNK, GM_HEADS)),
            full((SSD_CONV, CONV_DIM)), full((1, CONV_DIM)),
            full((1, LANES)), full((1, LANES)),
            full((1, SSD_WIDTH)), full((1, SSD_WIDTH)),
        ],
        out_specs=pl.BlockSpec((CHUNK, GM_WIDTH + SSD_WIDTH), lambda b, c: (rows(b, c), 0)),
        out_shape=jax.ShapeDtypeStruct((t, GM_WIDTH + SSD_WIDTH), BF16),
        scratch_shapes=[
            pltpu.VMEM((HALO + CHUNK, CONV_DIM), F32),
            pltpu.VMEM((CHUNK, CONV_DIM), F32),
            pltpu.VMEM((SSD_STATE, SSD_WIDTH), F32),
            pltpu.VMEM((CHUNK, SSD_WIDTH), F32),
        ],
        compiler_params=_cparams(("parallel", "arbitrary")),
        name="mixer",
    )(proj, proj, proj, proj, dt_raw,
      p["lnw"], p["lnb"], p["ws"], p["bst"], p["cw"], p["cb"], p["dtb"], p["alog"], p["dsk"], p["nw"])


def _first_max(vals, axis_iota, n):
    m = jnp.max(vals, axis=0, keepdims=True)
    idx = jnp.min(jnp.where(vals == m, axis_iota, n), axis=0, keepdims=True)
    return m, idx


def _post_kernel(y_ref, wout_ref, x_ref, g1_ref, n2w_ref, s2_ref, sh2_ref, rwt_ref, rb_ref,
                 xo_ref, h2_ref, eidx_ref, gate_ref, rank_ref, cnt_ref, carry_scr):
    @pl.when(pl.program_id(0) == 0)
    def _():
        carry_scr[...] = jnp.zeros(carry_scr.shape, F32)

    tm = x_ref.shape[0]
    mix = jnp.dot(y_ref[...], wout_ref[...], preferred_element_type=F32)
    x = x_ref[...] + g1_ref[...] * mix
    xo_ref[...] = x
    ms = jnp.mean(x * x, axis=-1, keepdims=True)
    h = x * lax.rsqrt(ms + EPS) * n2w_ref[...] * (1.0 + s2_ref[...]) + sh2_ref[...]
    _store_slab_rows(h2_ref, 0, h)

    logits_t = lax.dot_general(rwt_ref[...], h.astype(BF16), _NT, preferred_element_type=F32)
    scores = jax.nn.sigmoid(logits_t)
    biased = scores + rb_ref[...]

    sub = lax.broadcasted_iota(I32, (EXPERTS_PER_GROUP, tm), 0)
    neg = jnp.float32(-jnp.inf)
    best = None
    for g in range(N_EXPERT_GROUPS):
        grp = biased[g * EXPERTS_PER_GROUP:(g + 1) * EXPERTS_PER_GROUP, :]
        m1, i1 = _first_max(grp, sub, EXPERTS_PER_GROUP)
        m2, i2 = _first_max(jnp.where(sub == i1, neg, grp), sub, EXPERTS_PER_GROUP)
        gs = m1 + m2
        if best is None:
            best, bi, l1, l2 = gs, jnp.zeros((1, tm), I32), i1, i2
        else:
            upd = gs > best
            best = jnp.where(upd, gs, best)
            bi = jnp.where(upd, g, bi)
            l1 = jnp.where(upd, i1, l1)
            l2 = jnp.where(upd, i2, l2)
    e0 = bi * EXPERTS_PER_GROUP + l1
    e1 = bi * EXPERTS_PER_GROUP + l2

    eio = lax.broadcasted_iota(I32, (N_EXPERTS, tm), 0)
    oh0 = eio == e0
    oh1 = eio == e1
    s0 = jnp.sum(jnp.where(oh0, scores, 0.0), axis=0, keepdims=True)
    s1 = jnp.sum(jnp.where(oh1, scores, 0.0), axis=0, keepdims=True)
    tot = s0 + s1
    eidx_ref[0:1, :] = e0
    eidx_ref[1:2, :] = e1
    gate_ref[0:1, :] = s0 / tot
    gate_ref[1:2, :] = s1 / tot

    ohs = oh0.astype(F32) + oh1.astype(F32)
    tr = lax.broadcasted_iota(I32, (tm, tm), 0)
    tc = lax.broadcasted_iota(I32, (tm, tm), 1)
    before = (tr < tc).astype(BF16)
    prefix = jnp.dot(ohs.astype(BF16), before, preferred_element_type=F32)
    base = carry_scr[:, 0:1] + prefix
    rank_ref[0:1, :] = jnp.sum(jnp.where(oh0, base, 0.0), axis=0, keepdims=True).astype(I32)
    rank_ref[1:2, :] = jnp.sum(jnp.where(oh1, base, 0.0), axis=0, keepdims=True).astype(I32)
    carry_scr[...] = carry_scr[...] + jnp.sum(ohs, axis=1, keepdims=True)
    cnt_ref[...] = carry_scr[...]


def _post(y_mix, w_out, xf, mod, norm2_w, rw_t, rb, seq):
    t, d = xf.shape
    dm = y_mix.shape[1]
    tm = min(256, seq)
    per_batch = seq // tm
    modspec = lambda k: pl.BlockSpec((None, None, 1, d), lambda i: (i // per_batch, k, 0, 0))
    tok = pl.BlockSpec((2, tm), lambda i: (0, i))
    return pl.pallas_call(
        _post_kernel,
        grid=(t // tm,),
        in_specs=[
            pl.BlockSpec((tm, dm), lambda i: (i, 0)),
            pl.BlockSpec((dm, d), lambda i: (0, 0)),
            pl.BlockSpec((tm, d), lambda i: (i, 0)),
            modspec(2),
            pl.BlockSpec((1, d), lambda i: (0, 0)),
            modspec(4),
            modspec(3),
            pl.BlockSpec((N_EXPERTS, d), lambda i: (0, 0)),
            pl.BlockSpec((N_EXPERTS, 1), lambda i: (0, 0)),
        ],
        out_specs=[
            pl.BlockSpec((tm, d), lambda i: (i, 0)),
            pl.BlockSpec((tm * SLAB, d // SLAB), lambda i: (i, 0)),
            tok, tok, tok,
            pl.BlockSpec((N_EXPERTS, LANES), lambda i: (0, 0)),
        ],
        out_shape=[
            jax.ShapeDtypeStruct((t, d), F32),
            jax.ShapeDtypeStruct((t * SLAB, d // SLAB), F32),
            jax.ShapeDtypeStruct((2, t), I32),
            jax.ShapeDtypeStruct((2, t), F32),
            jax.ShapeDtypeStruct((2, t), I32),
            jax.ShapeDtypeStruct((N_EXPERTS, LANES), F32),
        ],
        scratch_shapes=[pltpu.VMEM((N_EXPERTS, LANES), F32)],
        compiler_params=_cparams(("arbitrary",)),
        name="post",
    )(y_mix, w_out, xf, mod, norm2_w.reshape(1, d), mod, mod, rw_t, rb.reshape(N_EXPERTS, 1))


META_ROWS = 8
META_LANES = 256
ROW_BLK_E, ROW_CNT, ROW_PSTART, ROW_NUSED, ROW_NVALID = 0, 1, 2, 3, 4


def _col_to_row(colv):
    n = colv.shape[0]
    r = lax.broadcasted_iota(I32, (n, n), 0)
    c = lax.broadcasted_iota(I32, (n, n), 1)
    return jnp.sum(jnp.where(r == c, colv, 0.0), axis=0, keepdims=True)


def _meta_kernel(cnt_ref, eidx_ref, rank_ref, dest_ref, meta_ref):
    t = eidx_ref.shape[1]
    cnt = cnt_ref[...]
    nblk = jnp.floor((cnt + (MOE_BLOCK - 1)) * (1.0 / MOE_BLOCK))
    r = lax.broadcasted_iota(I32, (N_EXPERTS, N_EXPERTS), 0)
    c = lax.broadcasted_iota(I32, (N_EXPERTS, N_EXPERTS), 1)
    lower = (c < r).astype(BF16)
    pstart = jnp.dot(lower, nblk.astype(BF16), preferred_element_type=F32)
    pend = pstart + nblk

    chunk = min(1024, t)
    for j in range(t // chunk):
        sl = slice(j * chunk, (j + 1) * chunk)
        eio = lax.broadcasted_iota(I32, (N_EXPERTS, chunk), 0)
        for k in range(2):
            oh = eio == eidx_ref[k:k + 1, sl]
            ps = jnp.sum(jnp.where(oh, pstart[:, 0:1], 0.0), axis=0, keepdims=True)
            dest_ref[k:k + 1, sl] = (ps * MOE_BLOCK).astype(I32) + rank_ref[k:k + 1, sl]

    bl = lax.broadcasted_iota(I32, (N_EXPERTS, META_LANES), 1).astype(F32)
    raw = jnp.sum((pend[:, 0:1] <= bl).astype(F32), axis=0, keepdims=True)
    raw = jnp.minimum(raw, N_EXPERTS - 1.0)
    nused = pend[N_EXPERTS - 1:N_EXPERTS, 0:1]
    used = bl[0:1, :] < nused
    last_e = jnp.max(jnp.where(used, raw, 0.0), axis=1, keepdims=True)
    meta_ref[...] = jnp.zeros(meta_ref.shape, I32)
    meta_ref[ROW_BLK_E:ROW_BLK_E + 1, :] = jnp.where(used, raw, last_e).astype(I32)
    meta_ref[ROW_CNT:ROW_CNT + 1, 0:N_EXPERTS] = _col_to_row(cnt[:, 0:1]).astype(I32)
    meta_ref[ROW_PSTART:ROW_PSTART + 1, 0:N_EXPERTS] = (_col_to_row(pstart[:, 0:1]) * MOE_BLOCK).astype(I32)
    meta_ref[ROW_NUSED:ROW_NUSED + 1, :] = jnp.broadcast_to(nused, (1, META_LANES)).astype(I32)
    mine = lax.broadcasted_iota(I32, (N_EXPERTS, META_LANES), 0).astype(F32) == raw
    cnt_b = jnp.sum(jnp.where(mine, cnt[:, 0:1], 0.0), axis=0, keepdims=True)
    first_b = jnp.sum(jnp.where(mine, pstart[:, 0:1], 0.0), axis=0, keepdims=True)
    nvalid = jnp.clip(cnt_b - (bl[0:1, :] - first_b) * MOE_BLOCK, 0.0, float(MOE_BLOCK))
    meta_ref[ROW_NVALID:ROW_NVALID + 1, :] = jnp.where(used, nvalid, 0.0).astype(I32)


def _meta(cnt, eidx, rank):
    t = eidx.shape[1]
    full = lambda shape: pl.BlockSpec(shape, lambda: (0,) * len(shape))
    return pl.pallas_call(
        _meta_kernel,
        in_specs=[full((N_EXPERTS, LANES)), full((2, t)), full((2, t))],
        out_specs=[full((2, t)), full((META_ROWS, META_LANES))],
        out_shape=[jax.ShapeDtypeStruct((2, t), I32), jax.ShapeDtypeStruct((META_ROWS, META_LANES), I32)],
        name="meta",
    )(cnt, eidx, rank)


def _invert_kernel(dest_ref, code_ref):
    i = pl.program_id(0)
    tm = dest_ref.shape[1]

    @pl.when(i == 0)
    def _():
        def clear(p, carry):
            code_ref[p] = 0
            return carry
        lax.fori_loop(0, code_ref.shape[0], clear, 0, unroll=16)

    def put(tok, carry):
        for k in range(2):
            code_ref[dest_ref[k, tok]] = (i * tm + tok) * 2 + k
        return carry
    lax.fori_loop(0, tm, put, 0, unroll=8)


def _invert(dest, n_rows):
    t = dest.shape[1]
    tm = min(1024, t)
    return pl.pallas_call(
        _invert_kernel,
        grid=(t // tm,),
        in_specs=[pl.BlockSpec((2, tm), lambda i: (0, i), memory_space=pltpu.SMEM)],
        out_specs=pl.BlockSpec((n_rows,), lambda i: (0,), memory_space=pltpu.SMEM),
        out_shape=jax.ShapeDtypeStruct((n_rows,), I32),
        compiler_params=_cparams(("arbitrary",)),
        name="invert",
    )(dest)


def _expert_kernel(be_sm, nu_sm, cnt_sm, nv_sm, *refs, layer):
    code_refs = refs[:GATHER_AHEAD + 1]
    code0_ref = code_refs[0]
    (h2_hbm, wg_hbm, wu_hbm, wd_hbm, yk_hbm, xbuf, obuf, land_g, land_u, land_d, wg_scr, wu_scr, wd_scr,
     sems, gsem, ssem, st_sm) = refs[GATHER_AHEAD + 1:]
    b = pl.program_id(0)
    nu = nu_sm[0]
    par = b & 1
    t = h2_hbm.shape[0] // SLAB
    n_blocks = pl.num_programs(0)
    blk_rows = MOE_BLOCK * VMEM_PITCH

    def hbm_rows(ref, index, n=1):
        return ref.at[pl.ds(pl.multiple_of(index * SLAB, SUBLANES), n * SLAB)]

    def vmem_row(ref, slot, r):
        return ref.at[pl.ds(pl.multiple_of((slot * MOE_BLOCK + r) * VMEM_PITCH, SUBLANES), SLAB)]

    def vmem_span(ref, slot):
        return ref.at[pl.ds(pl.multiple_of(slot * blk_rows, SUBLANES), MOE_BLOCK * SLAB)]

    def gather_row(ahead, r, slot):
        return pltpu.make_async_copy(hbm_rows(h2_hbm, code_refs[ahead][0, r] >> 1), vmem_row(xbuf, slot, r),
                                     gsem.at[slot])

    def scatter_row(r, slot):
        code = code0_ref[0, r]
        return pltpu.make_async_copy(vmem_row(obuf, slot, r), hbm_rows(yk_hbm, (code & 1) * t + (code >> 1)),
                                     ssem.at[slot])

    def gather_block(slot):
        return pltpu.make_async_copy(hbm_rows(h2_hbm, 0, MOE_BLOCK), vmem_span(xbuf, slot), gsem.at[slot])

    def scatter_block(slot):
        return pltpu.make_async_copy(vmem_span(obuf, slot), hbm_rows(yk_hbm, 0, MOE_BLOCK), ssem.at[slot])

    def start_rows(n, row_copy, queues):
        @pl.when(n == MOE_BLOCK)
        def _():
            def eight(g, c):
                for u in range(SUBLANES):
                    row_copy(g * SUBLANES + u).start(priority=queues[u % len(queues)])
                return c
            lax.fori_loop(0, MOE_BLOCK // SUBLANES, eight, 0)

        @pl.when(n != MOE_BLOCK)
        def _():
            lax.fori_loop(0, n, lambda r, c: (row_copy(r).start(priority=queues[0]), c)[1], 0)

    def wait_rows(n, row_copy, block_copy):
        @pl.when(n == MOE_BLOCK)
        def _():
            block_copy.wait()

        @pl.when(n != MOE_BLOCK)
        def _():
            lax.fori_loop(0, n, lambda r, c: (row_copy(r).wait(), c)[1], 0)

    def fetch(e, slot):
        return (pltpu.make_async_copy(wg_hbm.at[layer, e], land_g.at[slot], sems.at[slot, 0]),
                pltpu.make_async_copy(wu_hbm.at[layer, e], land_u.at[slot], sems.at[slot, 1]),
                pltpu.make_async_copy(wd_hbm.at[layer, e], land_d.at[slot], sems.at[slot, 2]))

    def next_used(e):
        return lax.while_loop(lambda n: (n < N_EXPERTS) & (cnt_sm[jnp.minimum(n, N_EXPERTS - 1)] == 0),
                              lambda n: n + 1, e + 1)

    def start_next(slot):
        nxt = next_used(st_sm[1])

        @pl.when(nxt < N_EXPERTS)
        def _():
            for cp in fetch(nxt, slot):
                cp.start(priority=WEIGHT_QUEUE)
        st_sm[1] = nxt

    @pl.when(b == 0)
    def _():
        st_sm[0] = 0
        st_sm[1] = -1
        start_next(0)
        start_next(1)
        xbuf[...] = jnp.zeros(xbuf.shape, xbuf.dtype)
        for ahead in range(GATHER_AHEAD):
            @pl.when(ahead < nu)
            def _():
                start_rows(nv_sm[ahead], lambda r: gather_row(ahead, r, ahead), GATHER_QUEUES)

    xslot = lax.rem(b, GATHER_AHEAD + 1)

    @pl.when(b < nu)
    def _():
        @pl.when(b + GATHER_AHEAD < nu)
        def _():
            start_rows(nv_sm[jnp.minimum(b + GATHER_AHEAD, n_blocks - 1)],
                       lambda r: gather_row(GATHER_AHEAD, r, lax.rem(b + GATHER_AHEAD, GATHER_AHEAD + 1)),
                       GATHER_QUEUES)

        prev = be_sm[jnp.maximum(b - 1, 0)]

        @pl.when((b == 0) | (be_sm[b] != prev))
        def _():
            slot = st_sm[0] & 1
            for cp in fetch(be_sm[b], slot):
                cp.wait()
            wg_scr[...] = land_g[slot].astype(BF16)
            wu_scr[...] = land_u[slot].astype(BF16)
            wd_scr[...] = land_d[slot].astype(BF16)
            st_sm[0] = st_sm[0] + 1
            start_next(slot)

        wait_rows(nv_sm[b], lambda r: gather_row(0, r, xslot), gather_block(xslot))
        xb = jnp.concatenate(_load_slab_rows(xbuf, xslot * blk_rows, MOE_BLOCK, VMEM_PITCH),
                             axis=1).astype(BF16)
        gate = jnp.dot(xb, wg_scr[...], preferred_element_type=F32)
        up = jnp.dot(xb, wu_scr[...], preferred_element_type=F32)
        act = (_silu(gate) * up).astype(BF16)
        out = jnp.dot(act, wd_scr[...], preferred_element_type=F32)

        @pl.when(b >= 2)
        def _():
            wait_rows(nv_sm[jnp.maximum(b - 2, 0)], lambda r: scatter_row(r, par), scatter_block(par))

        _store_slab_rows(obuf, par * blk_rows, out, VMEM_PITCH)
        start_rows(nv_sm[b], lambda r: scatter_row(r, par), SCATTER_QUEUES)

        @pl.when(b == nu - 1)
        def _():
            @pl.when(b >= 1)
            def _():
                wait_rows(nv_sm[jnp.maximum(b - 1, 0)], lambda r: scatter_row(r, 1 - par),
                          scatter_block(1 - par))
            wait_rows(nv_sm[b], lambda r: scatter_row(r, par), scatter_block(par))


def _experts(blk_e, nused, cnt_row, nvalid, code_sorted, h2, w_gate, w_up, w_down, layer):
    sw = h2.shape[1]
    t = h2.shape[0] // SLAB
    d = SLAB * sw
    de = w_gate.shape[3]
    n_blocks = code_sorted.shape[0] // MOE_BLOCK
    codes = code_sorted.reshape(n_blocks, 1, MOE_BLOCK)
    hbm = pl.BlockSpec(memory_space=pl.ANY)
    id_spec = lambda ahead: pl.BlockSpec(
        (None, 1, MOE_BLOCK), lambda b, be, nu, cnt, nv: (jnp.minimum(b + ahead, nu[0] - 1), 0, 0),
        memory_space=pltpu.SMEM)
    return pl.pallas_call(
        functools.partial(_expert_kernel, layer=layer),
        grid_spec=pltpu.PrefetchScalarGridSpec(
            num_scalar_prefetch=4,
            grid=(n_blocks,),
            in_specs=[id_spec(ahead) for ahead in range(GATHER_AHEAD + 1)] + [hbm, hbm, hbm, hbm],
            out_specs=hbm,
            scratch_shapes=[
                pltpu.VMEM(((GATHER_AHEAD + 1) * MOE_BLOCK * VMEM_PITCH, sw), F32),
                pltpu.VMEM((2 * MOE_BLOCK * VMEM_PITCH, sw), F32),
                pltpu.VMEM((2, d, de), F32),
                pltpu.VMEM((2, d, de), F32),
                pltpu.VMEM((2, de, d), F32),
                pltpu.VMEM((d, de), BF16),
                pltpu.VMEM((d, de), BF16),
                pltpu.VMEM((de, d), BF16),
                pltpu.SemaphoreType.DMA((2, 3)),
                pltpu.SemaphoreType.DMA((GATHER_AHEAD + 1,)),
                pltpu.SemaphoreType.DMA((2,)),
                pltpu.SMEM((2,), I32),
            ],
        ),
        out_shape=jax.ShapeDtypeStruct((2 * t * SLAB, sw), F32),
        compiler_params=_cparams(("arbitrary",)),
        name="experts",
    )(blk_e, nused, cnt_row, nvalid, *([codes] * (GATHER_AHEAD + 1)), h2, w_gate, w_up, w_down)


def _combine_kernel(gate_ref, x_ref, g2_ref, fw_ref, y0_ref, y1_ref, out_ref, *, final):
    tm = x_ref.shape[0]
    r = lax.broadcasted_iota(I32, (tm, tm), 0)
    c = lax.broadcasted_iota(I32, (tm, tm), 1)
    eye = r == c
    w0 = jnp.sum(jnp.where(eye, gate_ref[0:1, :], 0.0), axis=1, keepdims=True)
    w1 = jnp.sum(jnp.where(eye, gate_ref[1:2, :], 0.0), axis=1, keepdims=True)
    y = jnp.concatenate([c0 * w0 + c1 * w1 for c0, c1 in zip(_load_slab_rows(y0_ref, 0, tm),
                                                             _load_slab_rows(y1_ref, 0, tm))], axis=1)
    x = x_ref[...] + g2_ref[...] * y
    if final:
        ms = jnp.mean(x * x, axis=-1, keepdims=True)
        x = x * lax.rsqrt(ms + EPS) * fw_ref[...]
    out_ref[...] = x


def _combine(gate, xf, mod, final_w, yk, seq, final):
    t, d = xf.shape
    tm = min(256, seq)
    per_batch = seq // tm
    return pl.pallas_call(
        functools.partial(_combine_kernel, final=final),
        grid=(t // tm,),
        in_specs=[
            pl.BlockSpec((2, tm), lambda i: (0, i)),
            pl.BlockSpec((tm, d), lambda i: (i, 0)),
            pl.BlockSpec((None, None, 1, d), lambda i: (i // per_batch, 5, 0, 0)),
            pl.BlockSpec((1, d), lambda i: (0, 0)),
            pl.BlockSpec((tm * SLAB, yk.shape[1]), lambda i: (i, 0)),
            pl.BlockSpec((tm * SLAB, yk.shape[1]), lambda i: (i + t // tm, 0)),
        ],
        out_specs=pl.BlockSpec((tm, d), lambda i: (i, 0)),
        out_shape=jax.ShapeDtypeStruct((t, d), F32),
        compiler_params=_cparams(("parallel",)),
        name="combine",
    )(gate, xf, mod, final_w.reshape(1, d), yk, yk)


def _pad_lanes(v, n=LANES):
    return jnp.pad(v, (0, n - v.shape[0])).reshape(1, n)


def kernel(x, c, ada_w, ada_b, norm1_w, w_in, gm_ln_w, gm_ln_b, gm_ws, gm_bs, conv_w, conv_b, dt_bias, a_log,
           d_skip, ssd_norm_w, w_out, norm2_w, router_w, router_b, exp_w_gate, exp_w_up, exp_w_down,
           final_norm_w):
    batch, seq, d = x.shape
    t = batch * seq
    depth = ada_w.shape[0]
    assert batch <= 8 and seq % CHUNK == 0 and w_in.shape[2] == MAIN_PROJ + SSD_HEADS
    n_rows = (-(-(t * 2) // MOE_BLOCK) + N_EXPERTS) * MOE_BLOCK
    assert n_rows // MOE_BLOCK <= META_LANES

    ada = _ada(jnp.pad(c, ((0, 8 - batch), (0, 0))), ada_w, ada_b)
    rw_t = router_w.T.astype(BF16)
    w_in_t = jnp.swapaxes(w_in, 1, 2)
    xf = x.reshape(t, d)
    for l in range(depth):
        mod = ada[l, :batch].reshape(batch, 6, 1, d)
        proj, dt_raw = _inproj(xf, norm1_w[l], mod, w_in_t, l, seq)
        mixer_params = dict(
            lnw=gm_ln_w[l].reshape(1, GM_WIDTH), lnb=gm_ln_b[l].reshape(1, GM_WIDTH),
            ws=gm_ws[l], bst=gm_bs[l].T,
            cw=conv_w[l], cb=conv_b[l].reshape(1, CONV_DIM),
            dtb=_pad_lanes(dt_bias[l]), alog=_pad_lanes(a_log[l]),
            dsk=jnp.repeat(d_skip[l], SSD_WIDTH // SSD_HEADS).reshape(1, SSD_WIDTH),
            nw=ssd_norm_w[l].reshape(1, SSD_WIDTH))
        y_mix = _mixer(proj, dt_raw, mixer_params, batch, seq)
        xf, h2, eidx, gate, rank, cnt = _post(y_mix, w_out[l].astype(BF16), xf, mod, norm2_w[l], rw_t,
                                              router_b, seq)
        dest, meta = _meta(cnt, eidx, rank)
        n_blocks = n_rows // MOE_BLOCK
        yk = _experts(meta[ROW_BLK_E, :n_blocks], meta[ROW_NUSED, :1], meta[ROW_CNT, :N_EXPERTS],
                      meta[ROW_NVALID, :n_blocks], _invert(dest, n_rows), h2, exp_w_gate, exp_w_up,
                      exp_w_down, l)
        xf = _combine(gate, xf, mod, final_norm_w, yk, seq, final=(l == depth - 1))
    return xf.reshape(batch, seq, d)
```

```python
import functools

import jax
import jax.numpy as jnp
from jax import lax
from jax.experimental import pallas as pl
from jax.experimental.pallas import tpu as pltpu

F32 = jnp.float32
BF16 = jnp.bfloat16
I32 = jnp.int32

EPS = 1e-6
LANES = 128
CHUNK = 128
GM_HEADS = 8
GM_WIDTH = 1024
SSD_WIDTH = 1024
SSD_HEADS = 16
SSD_GROUPS = 2
SSD_STATE = 128
SSD_CONV = 4
CONV_DIM = SSD_WIDTH + 2 * SSD_GROUPS * SSD_STATE
MAIN_PROJ = 2 * GM_WIDTH + SSD_WIDTH + CONV_DIM
N_EXPERTS = 64
EXPERTS_PER_GROUP = 8
N_EXPERT_GROUPS = 8
MOE_BLOCK = 128
HALO = 8
SLAB = 16
VMEM_PITCH = 24
SUBLANES = 8
WEIGHT_QUEUE = 1
GATHER_QUEUES = (0,)
GATHER_AHEAD = 1
SCATTER_QUEUES = (0, 1)
VMEM_LIMIT = 56 * 1024 * 1024


def _cparams(sem, vmem=VMEM_LIMIT):
    return pltpu.CompilerParams(dimension_semantics=sem, vmem_limit_bytes=vmem)


def _silu(x):
    return x * (0.5 * (1.0 + jnp.tanh(0.5 * x)))


def _gelu(x):
    return 0.5 * x * (1.0 + lax.erf(x * 0.7071067811865476))


def _softplus(x):
    return jnp.maximum(x, 0.0) + jnp.log1p(jnp.exp(-jnp.abs(x)))


def _store_slab_rows(ref, base, x, pitch=SLAB):
    n = x.shape[0]
    w = ref.shape[1]
    for s in range(SLAB):
        ref[pl.ds(base + s, n, stride=pitch), :] = x[:, s * w:(s + 1) * w]


def _load_slab_rows(ref, base, n, pitch=SLAB):
    return [ref[pl.ds(base + s, n, stride=pitch), :] for s in range(SLAB)]


def _ada_kernel(c_ref, w_ref, b_ref, o_ref):
    sc = _silu(c_ref[...])
    o_ref[0] = jnp.dot(sc.astype(BF16), w_ref[0].astype(BF16), preferred_element_type=F32) + b_ref[0]


def _ada(c_pad, ada_w, ada_b):
    n_layers, d, n = ada_w.shape
    tn = 1024
    return pl.pallas_call(
        _ada_kernel,
        grid=(n_layers, n // tn),
        in_specs=[
            pl.BlockSpec((8, d), lambda l, j: (0, 0)),
            pl.BlockSpec((1, d, tn), lambda l, j: (l, 0, j)),
            pl.BlockSpec((1, 1, tn), lambda l, j: (l, 0, j)),
        ],
        out_specs=pl.BlockSpec((1, 8, tn), lambda l, j: (l, 0, j)),
        out_shape=jax.ShapeDtypeStruct((n_layers, 8, n), F32),
        compiler_params=_cparams(("parallel", "parallel")),
        name="ada",
    )(c_pad, ada_w, ada_b.reshape(n_layers, 1, n))


_NT = (((1,), (1,)), ((), ()))


def _inproj_kernel(x_ref, nw_ref, s_ref, sh_ref, wt_ref, wdt_ref, o_ref, dt_ref, h_scr, wdt_scr):
    @pl.when(pl.program_id(1) == 0)
    def _():
        x = x_ref[...]
        ms = jnp.mean(x * x, axis=-1, keepdims=True)
        y = x * lax.rsqrt(ms + EPS) * nw_ref[...]
        h = (y * (1.0 + s_ref[...]) + sh_ref[...]).astype(BF16)
        h_scr[...] = h
        wdt_scr[...] = jnp.zeros(wdt_scr.shape, wdt_scr.dtype)
        wdt_scr[0:SSD_HEADS, :] = wdt_ref[...].astype(BF16)
        dt_ref[...] = lax.dot_general(h, wdt_scr[...], _NT, preferred_element_type=F32)

    o_ref[...] = lax.dot_general(h_scr[...], wt_ref[...].astype(BF16), _NT, preferred_element_type=F32)


def _inproj(xf, norm_w, mod, w_in_t, layer, seq):
    t, d = xf.shape
    tm = min(1024, seq)
    tn = 1152
    per_batch = seq // tm
    return pl.pallas_call(
        _inproj_kernel,
        grid=(t // tm, MAIN_PROJ // tn),
        in_specs=[
            pl.BlockSpec((tm, d), lambda i, j: (i, 0)),
            pl.BlockSpec((1, d), lambda i, j: (0, 0)),
            pl.BlockSpec((None, None, 1, d), lambda i, j: (i // per_batch, 1, 0, 0)),
            pl.BlockSpec((None, None, 1, d), lambda i, j: (i // per_batch, 0, 0, 0)),
            pl.BlockSpec((None, tn, d), lambda i, j: (layer, j, 0)),
            pl.BlockSpec((None, SSD_HEADS, d), lambda i, j: (layer, MAIN_PROJ // SSD_HEADS, 0)),
        ],
        out_specs=[
            pl.BlockSpec((tm, tn), lambda i, j: (i, j)),
            pl.BlockSpec((tm, LANES), lambda i, j: (i, 0)),
        ],
        out_shape=[
            jax.ShapeDtypeStruct((t, MAIN_PROJ), F32),
            jax.ShapeDtypeStruct((t, LANES), F32),
        ],
        scratch_shapes=[
            pltpu.VMEM((tm, d), BF16),
            pltpu.VMEM((LANES, d), BF16),
        ],
        compiler_params=_cparams(("parallel", "arbitrary")),
        name="inproj",
    )(xf, norm_w.reshape(1, d), mod, mod, w_in_t, w_in_t)


def _mixer_kernel(u_ref, v_ref, z_ref, xbc_ref, dt_ref,
                  lnw_ref, lnb_ref, ws_ref, bst_ref, cw_ref, cb_ref, dtb_ref, alog_ref,
                  dsk_ref, nw_ref, y_ref, buf_scr, xa_scr, state_scr, ys_scr):
    @pl.when(pl.program_id(1) == 0)
    def _():
        buf_scr[0:HALO, :] = jnp.zeros((HALO, CONV_DIM), F32)
        state_scr[...] = jnp.zeros(state_scr.shape, F32)

    row = lax.broadcasted_iota(I32, (CHUNK, CHUNK), 0)
    col = lax.broadcasted_iota(I32, (CHUNK, CHUNK), 1)
    tril = row >= col
    lane_lo = col < (LANES // 2)
    lane_lo_row = lane_lo[0:1, :]

    for h in range(GM_HEADS):
        sl = slice(h * LANES, (h + 1) * LANES)
        gu = _gelu(u_ref[:, sl])
        gv = _gelu(v_ref[:, sl])
        mu = jnp.mean(gv, axis=-1, keepdims=True)
        dv = gv - mu
        var = jnp.mean(dv * dv, axis=-1, keepdims=True)
        vn = dv * lax.rsqrt(var + EPS) * lnw_ref[:, sl] + lnb_ref[:, sl]
        w = jnp.where(tril, ws_ref[h], 0.0).astype(BF16)
        s = jnp.dot(w, vn.astype(BF16), preferred_element_type=F32) + bst_ref[:, h:h + 1]
        y_ref[:, sl] = (gu * s).astype(y_ref.dtype)

    buf_scr[HALO:HALO + CHUNK, :] = xbc_ref[...]
    for cb in range(CONV_DIM // 256):
        cs_ = slice(cb * 256, (cb + 1) * 256)
        acc = cb_ref[:, cs_] + cw_ref[0:1, cs_] * buf_scr[HALO - 3:HALO - 3 + CHUNK, cs_]
        for k in range(1, SSD_CONV):
            acc = acc + cw_ref[k:k + 1, cs_] * buf_scr[HALO - 3 + k:HALO - 3 + k + CHUNK, cs_]
        xa_scr[:, cs_] = _silu(acc)
    buf_scr[0:HALO, :] = buf_scr[CHUNK:CHUNK + HALO, :]

    dt = _softplus(dt_ref[...] + dtb_ref[...])
    a = -jnp.exp(alog_ref[...])
    ad = dt * a
    cs = jnp.dot(tril.astype(F32), ad, precision=lax.Precision.HIGHEST, preferred_element_type=F32)
    cs_t = cs.T
    last = cs[CHUNK - 1:CHUNK, :]
    ds = jnp.exp(last - cs)
    ecs = jnp.exp(cs)
    cd = jnp.exp(last)

    def pair_expand(m, p):
        return jnp.where(lane_lo[0:m.shape[0], :], m[:, 2 * p:2 * p + 1], m[:, 2 * p + 1:2 * p + 2])

    pairs_per_group = SSD_HEADS // SSD_GROUPS // 2
    gw = SSD_WIDTH // SSD_GROUPS
    for g in range(SSD_GROUPS):
        bm_g = xa_scr[:, SSD_WIDTH + g * SSD_STATE:SSD_WIDTH + (g + 1) * SSD_STATE]
        cm_g = xa_scr[:, SSD_WIDTH + (SSD_GROUPS + g) * SSD_STATE:SSD_WIDTH + (SSD_GROUPS + g + 1) * SSD_STATE]
        cmb = cm_g.astype(BF16)
        bmb = bm_g.astype(BF16)
        cbm = lax.dot_general(cmb, bmb, (((1,), (1,)), ((), ())), preferred_element_type=F32)
        bm_t = bm_g.T.astype(BF16)
        st_prev = state_scr[:, g * gw:(g + 1) * gw]
        y_off = jnp.dot(cmb, st_prev.astype(BF16), preferred_element_type=F32)
        xdds = []
        cds = []
        for q in range(pairs_per_group):
            p = g * pairs_per_group + q
            sl = slice(p * LANES, (p + 1) * LANES)
            xs_p = xa_scr[:, sl]
            xd = xs_p * pair_expand(dt, p)
            xdb = xd.astype(BF16)
            ys = []
            for hh in (2 * p, 2 * p + 1):
                diff = cs[:, hh:hh + 1] - cs_t[hh:hh + 1, :]
                lm = jnp.where(tril, jnp.exp(jnp.where(tril, diff, 0.0)), 0.0)
                wmat = (cbm * lm).astype(BF16)
                ys.append(jnp.dot(wmat, xdb, preferred_element_type=F32))
            y_diag = jnp.where(lane_lo, ys[0], ys[1])
            y = y_diag + y_off[:, q * LANES:(q + 1) * LANES] * pair_expand(ecs, p)
            ys_scr[:, sl] = y + dsk_ref[:, sl] * xs_p
            xdds.append((xd * pair_expand(ds, p)).astype(BF16))
            cds.append(jnp.where(lane_lo_row, cd[:, 2 * p:2 * p + 1], cd[:, 2 * p + 1:2 * p + 2]))
        st_new = jnp.dot(bm_t, jnp.concatenate(xdds, axis=1), preferred_element_type=F32)
        state_scr[:, g * gw:(g + 1) * gw] = st_prev * jnp.concatenate(cds, axis=1) + st_new

    for g in range(SSD_GROUPS):
        sl = slice(g * gw, (g + 1) * gw)
        gg = ys_scr[:, sl] * _silu(z_ref[:, sl])
        ms = jnp.mean(gg * gg, axis=-1, keepdims=True)
        y_ref[:, GM_WIDTH + g * gw:GM_WIDTH + (g + 1) * gw] = (
            gg * lax.rsqrt(ms + EPS) * nw_ref[:, sl]).astype(y_ref.dtype)


def _mixer(proj, dt_raw, p, batch, seq):
    t = proj.shape[0]
    nc = seq // CHUNK
    rows = lambda b, c: b * nc + c
    full = lambda shape: pl.BlockSpec(shape, lambda b, c: (0,) * len(shape))
    return pl.pallas_call(
        _mixer_kernel,
        grid=(batch, nc),
        in_specs=[
            pl.BlockSpec((CHUNK, GM_WIDTH), lambda b, c: (rows(b, c), 0)),
            pl.BlockSpec((CHUNK, GM_WIDTH), lambda b, c: (rows(b, c), 1)),
            pl.BlockSpec((CHUNK, SSD_WIDTH), lambda b, c: (rows(b, c), 2)),
            pl.BlockSpec((CHUNK, CONV_DIM), lambda b, c: (rows(b, c), 2)),
            pl.BlockSpec((CHUNK, LANES), lambda b, c: (rows(b, c), 0)),
            full((1, GM_WIDTH)), full((1, GM_WIDTH)),
            full((GM_HEADS, CHUNK, CHUNK)), full((CHUNK, GM_HEADS)),
            full((SSD_CONV, CONV_DIM)), full((1, CONV_DIM)),
            full((1, LANES)), full((1, LANES)),
            full((1, SSD_WIDTH)), full((1, SSD_WIDTH)),
        ],
        out_specs=pl.BlockSpec((CHUNK, GM_WIDTH + SSD_WIDTH), lambda b, c: (rows(b, c), 0)),
        out_shape=jax.ShapeDtypeStruct((t, GM_WIDTH + SSD_WIDTH), BF16),
        scratch_shapes=[
            pltpu.VMEM((HALO + CHUNK, CONV_DIM), F32),
            pltpu.VMEM((CHUNK, CONV_DIM), F32),
            pltpu.VMEM((SSD_STATE, SSD_WIDTH), F32),
            pltpu.VMEM((CHUNK, SSD_WIDTH), F32),
        ],
        compiler_params=_cparams(("parallel", "arbitrary")),
        name="mixer",
    )(proj, proj, proj, proj, dt_raw,
      p["lnw"], p["lnb"], p["ws"], p["bst"], p["cw"], p["cb"], p["dtb"], p["alog"], p["dsk"], p["nw"])


def _first_max(vals, axis_iota, n):
    m = jnp.max(vals, axis=0, keepdims=True)
    idx = jnp.min(jnp.where(vals == m, axis_iota, n), axis=0, keepdims=True)
    return m, idx


def _post_kernel(y_ref, wout_ref, x_ref, g1_ref, n2w_ref, s2_ref, sh2_ref, rwt_ref, rb_ref,
                 xo_ref, h2_ref, eidx_ref, gate_ref, rank_ref, cnt_ref, carry_scr):
    @pl.when(pl.program_id(0) == 0)
    def _():
        carry_scr[...] = jnp.zeros(carry_scr.shape, F32)

    tm = x_ref.shape[0]
    mix = jnp.dot(y_ref[...], wout_ref[...], preferred_element_type=F32)
    x = x_ref[...] + g1_ref[...] * mix
    xo_ref[...] = x
    ms = jnp.mean(x * x, axis=-1, keepdims=True)
    h = x * lax.rsqrt(ms + EPS) * n2w_ref[...] * (1.0 + s2_ref[...]) + sh2_ref[...]
    _store_slab_rows(h2_ref, 0, h)

    logits_t = lax.dot_general(rwt_ref[...], h.astype(BF16), _NT, preferred_element_type=F32)
    scores = jax.nn.sigmoid(logits_t)
    biased = scores + rb_ref[...]

    sub = lax.broadcasted_iota(I32, (EXPERTS_PER_GROUP, tm), 0)
    neg = jnp.float32(-jnp.inf)
    best = None
    for g in range(N_EXPERT_GROUPS):
        grp = biased[g * EXPERTS_PER_GROUP:(g + 1) * EXPERTS_PER_GROUP, :]
        m1, i1 = _first_max(grp, sub, EXPERTS_PER_GROUP)
        m2, i2 = _first_max(jnp.where(sub == i1, neg, grp), sub, EXPERTS_PER_GROUP)
        gs = m1 + m2
        if best is None:
            best, bi, l1, l2 = gs, jnp.zeros((1, tm), I32), i1, i2
        else:
            upd = gs > best
            best = jnp.where(upd, gs, best)
            bi = jnp.where(upd, g, bi)
            l1 = jnp.where(upd, i1, l1)
            l2 = jnp.where(upd, i2, l2)
    e0 = bi * EXPERTS_PER_GROUP + l1
    e1 = bi * EXPERTS_PER_GROUP + l2

    eio = lax.broadcasted_iota(I32, (N_EXPERTS, tm), 0)
    oh0 = eio == e0
    oh1 = eio == e1
    s0 = jnp.sum(jnp.where(oh0, scores, 0.0), axis=0, keepdims=True)
    s1 = jnp.sum(jnp.where(oh1, scores, 0.0), axis=0, keepdims=True)
    tot = s0 + s1
    eidx_ref[0:1, :] = e0
    eidx_ref[1:2, :] = e1
    gate_ref[0:1, :] = s0 / tot
    gate_ref[1:2, :] = s1 / tot

    ohs = oh0.astype(F32) + oh1.astype(F32)
    tr = lax.broadcasted_iota(I32, (tm, tm), 0)
    tc = lax.broadcasted_iota(I32, (tm, tm), 1)
    before = (tr < tc).astype(BF16)
    prefix = jnp.dot(ohs.astype(BF16), before, preferred_element_type=F32)
    base = carry_scr[:, 0:1] + prefix
    rank_ref[0:1, :] = jnp.sum(jnp.where(oh0, base, 0.0), axis=0, keepdims=True).astype(I32)
    rank_ref[1:2, :] = jnp.sum(jnp.where(oh1, base, 0.0), axis=0, keepdims=True).astype(I32)
    carry_scr[...] = carry_scr[...] + jnp.sum(ohs, axis=1, keepdims=True)
    cnt_ref[...] = carry_scr[...]


def _post(y_mix, w_out, xf, mod, norm2_w, rw_t, rb, seq):
    t, d = xf.shape
    dm = y_mix.shape[1]
    tm = min(512, seq)
    per_batch = seq // tm
    modspec = lambda k: pl.BlockSpec((None, None, 1, d), lambda i: (i // per_batch, k, 0, 0))
    tok = pl.BlockSpec((2, tm), lambda i: (0, i))
    return pl.pallas_call(
        _post_kernel,
        grid=(t // tm,),
        in_specs=[
            pl.BlockSpec((tm, dm), lambda i: (i, 0)),
            pl.BlockSpec((dm, d), lambda i: (0, 0)),
            pl.BlockSpec((tm, d), lambda i: (i, 0)),
            modspec(2),
            pl.BlockSpec((1, d), lambda i: (0, 0)),
            modspec(4),
            modspec(3),
            pl.BlockSpec((N_EXPERTS, d), lambda i: (0, 0)),
            pl.BlockSpec((N_EXPERTS, 1), lambda i: (0, 0)),
        ],
        out_specs=[
            pl.BlockSpec((tm, d), lambda i: (i, 0)),
            pl.BlockSpec((tm * SLAB, d // SLAB), lambda i: (i, 0)),
            tok, tok, tok,
            pl.BlockSpec((N_EXPERTS, LANES), lambda i: (0, 0)),
        ],
        out_shape=[
            jax.ShapeDtypeStruct((t, d), F32),
            jax.ShapeDtypeStruct((t * SLAB, d // SLAB), F32),
            jax.ShapeDtypeStruct((2, t), I32),
            jax.ShapeDtypeStruct((2, t), F32),
            jax.ShapeDtypeStruct((2, t), I32),
            jax.ShapeDtypeStruct((N_EXPERTS, LANES), F32),
        ],
        scratch_shapes=[pltpu.VMEM((N_EXPERTS, LANES), F32)],
        compiler_params=_cparams(("arbitrary",)),
        name="post",
    )(y_mix, w_out, xf, mod, norm2_w.reshape(1, d), mod, mod, rw_t, rb.reshape(N_EXPERTS, 1))


META_ROWS = 8
META_LANES = 256
ROW_BLK_E, ROW_CNT, ROW_PSTART, ROW_NUSED, ROW_NVALID = 0, 1, 2, 3, 4


def _col_to_row(colv):
    n = colv.shape[0]
    r = lax.broadcasted_iota(I32, (n, n), 0)
    c = lax.broadcasted_iota(I32, (n, n), 1)
    return jnp.sum(jnp.where(r == c, colv, 0.0), axis=0, keepdims=True)


def _meta_kernel(cnt_ref, eidx_ref, rank_ref, dest_ref, meta_ref):
    t = eidx_ref.shape[1]
    cnt = cnt_ref[...]
    nblk = jnp.floor((cnt + (MOE_BLOCK - 1)) * (1.0 / MOE_BLOCK))
    r = lax.broadcasted_iota(I32, (N_EXPERTS, N_EXPERTS), 0)
    c = lax.broadcasted_iota(I32, (N_EXPERTS, N_EXPERTS), 1)
    lower = (c < r).astype(BF16)
    pstart = jnp.dot(lower, nblk.astype(BF16), preferred_element_type=F32)
    pend = pstart + nblk

    chunk = min(1024, t)
    for j in range(t // chunk):
        sl = slice(j * chunk, (j + 1) * chunk)
        eio = lax.broadcasted_iota(I32, (N_EXPERTS, chunk), 0)
        for k in range(2):
            oh = eio == eidx_ref[k:k + 1, sl]
            ps = jnp.sum(jnp.where(oh, pstart[:, 0:1], 0.0), axis=0, keepdims=True)
            dest_ref[k:k + 1, sl] = (ps * MOE_BLOCK).astype(I32) + rank_ref[k:k + 1, sl]

    bl = lax.broadcasted_iota(I32, (N_EXPERTS, META_LANES), 1).astype(F32)
    raw = jnp.sum((pend[:, 0:1] <= bl).astype(F32), axis=0, keepdims=True)
    raw = jnp.minimum(raw, N_EXPERTS - 1.0)
    nused = pend[N_EXPERTS - 1:N_EXPERTS, 0:1]
    used = bl[0:1, :] < nused
    last_e = jnp.max(jnp.where(used, raw, 0.0), axis=1, keepdims=True)
    meta_ref[...] = jnp.zeros(meta_ref.shape, I32)
    meta_ref[ROW_BLK_E:ROW_BLK_E + 1, :] = jnp.where(used, raw, last_e).astype(I32)
    meta_ref[ROW_CNT:ROW_CNT + 1, 0:N_EXPERTS] = _col_to_row(cnt[:, 0:1]).astype(I32)
    meta_ref[ROW_PSTART:ROW_PSTART + 1, 0:N_EXPERTS] = (_col_to_row(pstart[:, 0:1]) * MOE_BLOCK).astype(I32)
    meta_ref[ROW_NUSED:ROW_NUSED + 1, :] = jnp.broadcast_to(nused, (1, META_LANES)).astype(I32)
    mine = lax.broadcasted_iota(I32, (N_EXPERTS, META_LANES), 0).astype(F32) == raw
    cnt_b = jnp.sum(jnp.where(mine, cnt[:, 0:1], 0.0), axis=0, keepdims=True)
    first_b = jnp.sum(jnp.where(mine, pstart[:, 0:1], 0.0), axis=0, keepdims=True)
    nvalid = jnp.clip(cnt_b - (bl[0:1, :] - first_b) * MOE_BLOCK, 0.0, float(MOE_BLOCK))
    meta_ref[ROW_NVALID:ROW_NVALID + 1, :] = jnp.where(used, nvalid, 0.0).astype(I32)


def _meta(cnt, eidx, rank):
    t = eidx.shape[1]
    full = lambda shape: pl.BlockSpec(shape, lambda: (0,) * len(shape))
    return pl.pallas_call(
        _meta_kernel,
        in_specs=[full((N_EXPERTS, LANES)), full((2, t)), full((2, t))],
        out_specs=[full((2, t)), full((META_ROWS, META_LANES))],
        out_shape=[jax.ShapeDtypeStruct((2, t), I32), jax.ShapeDtypeStruct((META_ROWS, META_LANES), I32)],
        name="meta",
    )(cnt, eidx, rank)


def _invert_kernel(dest_ref, code_ref):
    i = pl.program_id(0)
    tm = dest_ref.shape[1]

    @pl.when(i == 0)
    def _():
        def clear(p, carry):
            code_ref[p] = 0
            return carry
        lax.fori_loop(0, code_ref.shape[0], clear, 0, unroll=16)

    def put(tok, carry):
        for k in range(2):
            code_ref[dest_ref[k, tok]] = (i * tm + tok) * 2 + k
        return carry
    lax.fori_loop(0, tm, put, 0, unroll=8)


def _invert(dest, n_rows):
    t = dest.shape[1]
    tm = min(1024, t)
    return pl.pallas_call(
        _invert_kernel,
        grid=(t // tm,),
        in_specs=[pl.BlockSpec((2, tm), lambda i: (0, i), memory_space=pltpu.SMEM)],
        out_specs=pl.BlockSpec((n_rows,), lambda i: (0,), memory_space=pltpu.SMEM),
        out_shape=jax.ShapeDtypeStruct((n_rows,), I32),
        compiler_params=_cparams(("arbitrary",)),
        name="invert",
    )(dest)


def _expert_kernel(be_sm, nu_sm, cnt_sm, nv_sm, *refs, layer):
    code_refs = refs[:GATHER_AHEAD + 1]
    code0_ref = code_refs[0]
    (h2_hbm, wg_hbm, wu_hbm, wd_hbm, yk_hbm, xbuf, obuf, land_g, land_u, land_d, wg_scr, wu_scr, wd_scr,
     sems, gsem, ssem, st_sm) = refs[GATHER_AHEAD + 1:]
    b = pl.program_id(0)
    nu = nu_sm[0]
    par = b & 1
    t = h2_hbm.shape[0] // SLAB
    n_blocks = pl.num_programs(0)
    blk_rows = MOE_BLOCK * VMEM_PITCH

    def hbm_rows(ref, index, n=1):
        return ref.at[pl.ds(pl.multiple_of(index * SLAB, SUBLANES), n * SLAB)]

    def vmem_row(ref, slot, r):
        return ref.at[pl.ds(pl.multiple_of((slot * MOE_BLOCK + r) * VMEM_PITCH, SUBLANES), SLAB)]

    def vmem_span(ref, slot):
        return ref.at[pl.ds(pl.multiple_of(slot * blk_rows, SUBLANES), MOE_BLOCK * SLAB)]

    def gather_row(ahead, r, slot):
        return pltpu.make_async_copy(hbm_rows(h2_hbm, code_refs[ahead][0, r] >> 1), vmem_row(xbuf, slot, r),
                                     gsem.at[slot])

    def scatter_row(r, slot):
        code = code0_ref[0, r]
        return pltpu.make_async_copy(vmem_row(obuf, slot, r), hbm_rows(yk_hbm, (code & 1) * t + (code >> 1)),
                                     ssem.at[slot])

    def gather_block(slot):
        return pltpu.make_async_copy(hbm_rows(h2_hbm, 0, MOE_BLOCK), vmem_span(xbuf, slot), gsem.at[slot])

    def scatter_block(slot):
        return pltpu.make_async_copy(vmem_span(obuf, slot), hbm_rows(yk_hbm, 0, MOE_BLOCK), ssem.at[slot])

    def start_rows(n, row_copy, queues):
        @pl.when(n == MOE_BLOCK)
        def _():
            def eight(g, c):
                for u in range(SUBLANES):
                    row_copy(g * SUBLANES + u).start(priority=queues[u % len(queues)])
                return c
            lax.fori_loop(0, MOE_BLOCK // SUBLANES, eight, 0)

        @pl.when(n != MOE_BLOCK)
        def _():
            lax.fori_loop(0, n, lambda r, c: (row_copy(r).start(priority=queues[0]), c)[1], 0)

    def wait_rows(n, row_copy, block_copy):
        @pl.when(n == MOE_BLOCK)
        def _():
            block_copy.wait()

        @pl.when(n != MOE_BLOCK)
        def _():
            lax.fori_loop(0, n, lambda r, c: (row_copy(r).wait(), c)[1], 0)

    def fetch(e, slot):
        return (pltpu.make_async_copy(wg_hbm.at[layer, e], land_g.at[slot], sems.at[slot, 0]),
                pltpu.make_async_copy(wu_hbm.at[layer, e], land_u.at[slot], sems.at[slot, 1]),
                pltpu.make_async_copy(wd_hbm.at[layer, e], land_d.at[slot], sems.at[slot, 2]))

    def next_used(e):
        return lax.while_loop(lambda n: (n < N_EXPERTS) & (cnt_sm[jnp.minimum(n, N_EXPERTS - 1)] == 0),
                              lambda n: n + 1, e + 1)

    def start_next(slot):
        nxt = next_used(st_sm[1])

        @pl.when(nxt < N_EXPERTS)
        def _():
            for cp in fetch(nxt, slot):
                cp.start(priority=WEIGHT_QUEUE)
        st_sm[1] = nxt

    @pl.when(b == 0)
    def _():
        st_sm[0] = 0
        st_sm[1] = -1
        start_next(0)
        start_next(1)
        xbuf[...] = jnp.zeros(xbuf.shape, xbuf.dtype)
        for ahead in range(GATHER_AHEAD):
            @pl.when(ahead < nu)
            def _():
                start_rows(nv_sm[ahead], lambda r: gather_row(ahead, r, ahead), GATHER_QUEUES)

    xslot = lax.rem(b, GATHER_AHEAD + 1)

    @pl.when(b < nu)
    def _():
        @pl.when(b + GATHER_AHEAD < nu)
        def _():
            start_rows(nv_sm[jnp.minimum(b + GATHER_AHEAD, n_blocks - 1)],
                       lambda r: gather_row(GATHER_AHEAD, r, lax.rem(b + GATHER_AHEAD, GATHER_AHEAD + 1)),
                       GATHER_QUEUES)

        prev = be_sm[jnp.maximum(b - 1, 0)]

        @pl.when((b == 0) | (be_sm[b] != prev))
        def _():
            slot = st_sm[0] & 1
            for cp in fetch(be_sm[b], slot):
                cp.wait()
            wg_scr[...] = land_g[slot].astype(BF16)
            wu_scr[...] = land_u[slot].astype(BF16)
            wd_scr[...] = land_d[slot].astype(BF16)
            st_sm[0] = st_sm[0] + 1
            start_next(slot)

        wait_rows(nv_sm[b], lambda r: gather_row(0, r, xslot), gather_block(xslot))
        xb = jnp.concatenate(_load_slab_rows(xbuf, xslot * blk_rows, MOE_BLOCK, VMEM_PITCH),
                             axis=1).astype(BF16)
        gate = jnp.dot(xb, wg_scr[...], preferred_element_type=F32)
        up = jnp.dot(xb, wu_scr[...], preferred_element_type=F32)
        act = (_silu(gate) * up).astype(BF16)
        out = jnp.dot(act, wd_scr[...], preferred_element_type=F32)

        @pl.when(b >= 2)
        def _():
            wait_rows(nv_sm[jnp.maximum(b - 2, 0)], lambda r: scatter_row(r, par), scatter_block(par))

        _store_slab_rows(obuf, par * blk_rows, out, VMEM_PITCH)
        start_rows(nv_sm[b], lambda r: scatter_row(r, par), SCATTER_QUEUES)

        @pl.when(b == nu - 1)
        def _():
            @pl.when(b >= 1)
            def _():
                wait_rows(nv_sm[jnp.maximum(b - 1, 0)], lambda r: scatter_row(r, 1 - par),
                          scatter_block(1 - par))
            wait_rows(nv_sm[b], lambda r: scatter_row(r, par), scatter_block(par))


def _experts(blk_e, nused, cnt_row, nvalid, code_sorted, h2, w_gate, w_up, w_down, layer):
    sw = h2.shape[1]
    t = h2.shape[0] // SLAB
    d = SLAB * sw
    de = w_gate.shape[3]
    n_blocks = code_sorted.shape[0] // MOE_BLOCK
    codes = code_sorted.reshape(n_blocks, 1, MOE_BLOCK)
    hbm = pl.BlockSpec(memory_space=pl.ANY)
    id_spec = lambda ahead: pl.BlockSpec(
        (None, 1, MOE_BLOCK), lambda b, be, nu, cnt, nv: (jnp.minimum(b + ahead, nu[0] - 1), 0, 0),
        memory_space=pltpu.SMEM)
    return pl.pallas_call(
        functools.partial(_expert_kernel, layer=layer),
        grid_spec=pltpu.PrefetchScalarGridSpec(
            num_scalar_prefetch=4,
            grid=(n_blocks,),
            in_specs=[id_spec(ahead) for ahead in range(GATHER_AHEAD + 1)] + [hbm, hbm, hbm, hbm],
            out_specs=hbm,
            scratch_shapes=[
                pltpu.VMEM(((GATHER_AHEAD + 1) * MOE_BLOCK * VMEM_PITCH, sw), F32),
                pltpu.VMEM((2 * MOE_BLOCK * VMEM_PITCH, sw), F32),
                pltpu.VMEM((2, d, de), F32),
                pltpu.VMEM((2, d, de), F32),
                pltpu.VMEM((2, de, d), F32),
                pltpu.VMEM((d, de), BF16),
                pltpu.VMEM((d, de), BF16),
                pltpu.VMEM((de, d), BF16),
                pltpu.SemaphoreType.DMA((2, 3)),
                pltpu.SemaphoreType.DMA((GATHER_AHEAD + 1,)),
                pltpu.SemaphoreType.DMA((2,)),
                pltpu.SMEM((2,), I32),
            ],
        ),
        out_shape=jax.ShapeDtypeStruct((2 * t * SLAB, sw), F32),
        compiler_params=_cparams(("arbitrary",)),
        name="experts",
    )(blk_e, nused, cnt_row, nvalid, *([codes] * (GATHER_AHEAD + 1)), h2, w_gate, w_up, w_down)


def _combine_kernel(gate_ref, x_ref, g2_ref, fw_ref, y0_ref, y1_ref, out_ref, *, final):
    tm = x_ref.shape[0]
    r = lax.broadcasted_iota(I32, (tm, tm), 0)
    c = lax.broadcasted_iota(I32, (tm, tm), 1)
    eye = r == c
    w0 = jnp.sum(jnp.where(eye, gate_ref[0:1, :], 0.0), axis=1, keepdims=True)
    w1 = jnp.sum(jnp.where(eye, gate_ref[1:2, :], 0.0), axis=1, keepdims=True)
    y = jnp.concatenate([c0 * w0 + c1 * w1 for c0, c1 in zip(_load_slab_rows(y0_ref, 0, tm),
                                                             _load_slab_rows(y1_ref, 0, tm))], axis=1)
    x = x_ref[...] + g2_ref[...] * y
    if final:
        ms = jnp.mean(x * x, axis=-1, keepdims=True)
        x = x * lax.rsqrt(ms + EPS) * fw_ref[...]
    out_ref[...] = x


def _combine(gate, xf, mod, final_w, yk, seq, final):
    t, d = xf.shape
    tm = min(512, seq)
    per_batch = seq // tm
    return pl.pallas_call(
        functools.partial(_combine_kernel, final=final),
        grid=(t // tm,),
        in_specs=[
            pl.BlockSpec((2, tm), lambda i: (0, i)),
            pl.BlockSpec((tm, d), lambda i: (i, 0)),
            pl.BlockSpec((None, None, 1, d), lambda i: (i // per_batch, 5, 0, 0)),
            pl.BlockSpec((1, d), lambda i: (0, 0)),
            pl.BlockSpec((tm * SLAB, yk.shape[1]), lambda i: (i, 0)),
            pl.BlockSpec((tm * SLAB, yk.shape[1]), lambda i: (i + t // tm, 0)),
        ],
        out_specs=pl.BlockSpec((tm, d), lambda i: (i, 0)),
        out_shape=jax.ShapeDtypeStruct((t, d), F32),
        compiler_params=_cparams(("parallel",)),
        name="combine",
    )(gate, xf, mod, final_w.reshape(1, d), yk, yk)


def _pad_lanes(v, n=LANES):
    return jnp.pad(v, (0, n - v.shape[0])).reshape(1, n)


def kernel(x, c, ada_w, ada_b, norm1_w, w_in, gm_ln_w, gm_ln_b, gm_ws, gm_bs, conv_w, conv_b, dt_bias, a_log,
           d_skip, ssd_norm_w, w_out, norm2_w, router_w, router_b, exp_w_gate, exp_w_up, exp_w_down,
           final_norm_w):
    batch, seq, d = x.shape
    t = batch * seq
    depth = ada_w.shape[0]
    assert batch <= 8 and seq % CHUNK == 0 and w_in.shape[2] == MAIN_PROJ + SSD_HEADS
    n_rows = (-(-(t * 2) // MOE_BLOCK) + N_EXPERTS) * MOE_BLOCK
    assert n_rows // MOE_BLOCK <= META_LANES

    ada = _ada(jnp.pad(c, ((0, 8 - batch), (0, 0))), ada_w, ada_b)
    rw_t = router_w.T.astype(BF16)
    w_in_t = jnp.swapaxes(w_in, 1, 2)
    xf = x.reshape(t, d)
    for l in range(depth):
        mod = ada[l, :batch].reshape(batch, 6, 1, d)
        proj, dt_raw = _inproj(xf, norm1_w[l], mod, w_in_t, l, seq)
        mixer_params = dict(
            lnw=gm_ln_w[l].reshape(1, GM_WIDTH), lnb=gm_ln_b[l].reshape(1, GM_WIDTH),
            ws=gm_ws[l], bst=gm_bs[l].T,
            cw=conv_w[l], cb=conv_b[l].reshape(1, CONV_DIM),
            dtb=_pad_lanes(dt_bias[l]), alog=_pad_lanes(a_log[l]),
            dsk=jnp.repeat(d_skip[l], SSD_WIDTH // SSD_HEADS).reshape(1, SSD_WIDTH),
            nw=ssd_norm_w[l].reshape(1, SSD_WIDTH))
        y_mix = _mixer(proj, dt_raw, mixer_params, batch, seq)
        xf, h2, eidx, gate, rank, cnt = _post(y_mix, w_out[l].astype(BF16), xf, mod, norm2_w[l], rw_t,
                                              router_b, seq)
        dest, meta = _meta(cnt, eidx, rank)
        n_blocks = n_rows // MOE_BLOCK
        yk = _experts(meta[ROW_BLK_E, :n_blocks], meta[ROW_NUSED, :1], meta[ROW_CNT, :N_EXPERTS],
                      meta[ROW_NVALID, :n_blocks], _invert(dest, n_rows), h2, exp_w_gate, exp_w_up,
                      exp_w_down, l)
        xf = _combine(gate, xf, mod, final_norm_w, yk, seq, final=(l == depth - 1))
    return xf.reshape(batch, seq, d)
```

```python
import functools

import jax
import jax.numpy as jnp
from jax import lax
from jax.experimental import pallas as pl
from jax.experimental.pallas import tpu as pltpu

F32 = jnp.float32
BF16 = jnp.bfloat16
I32 = jnp.int32

EPS = 1e-6
LANES = 128
CHUNK = 128
GM_HEADS = 8
GM_WIDTH = 1024
SSD_WIDTH = 1024
SSD_HEADS = 16
SSD_GROUPS = 2
SSD_STATE = 128
SSD_CONV = 4
CONV_DIM = SSD_WIDTH + 2 * SSD_GROUPS * SSD_STATE
MAIN_PROJ = 2 * GM_WIDTH + SSD_WIDTH + CONV_DIM
N_EXPERTS = 64
EXPERTS_PER_GROUP = 8
N_EXPERT_GROUPS = 8
MOE_BLOCK = 128
HALO = 8
SLAB = 16
VMEM_PITCH = 24
SUBLANES = 8
WEIGHT_QUEUE = 1
GATHER_QUEUES = (0,)
GATHER_AHEAD = 1
SCATTER_QUEUES = (0, 1)
VMEM_LIMIT = 56 * 1024 * 1024


def _cparams(sem, vmem=VMEM_LIMIT):
    return pltpu.CompilerParams(dimension_semantics=sem, vmem_limit_bytes=vmem)


def _silu(x):
    return x * (0.5 * (1.0 + jnp.tanh(0.5 * x)))


def _gelu(x):
    return 0.5 * x * (1.0 + lax.erf(x * 0.7071067811865476))


def _softplus(x):
    return jnp.maximum(x, 0.0) + jnp.log1p(jnp.exp(-jnp.abs(x)))


def _store_slab_rows(ref, base, x, pitch=SLAB):
    n = x.shape[0]
    w = ref.shape[1]
    for s in range(SLAB):
        ref[pl.ds(base + s, n, stride=pitch), :] = x[:, s * w:(s + 1) * w]


def _load_slab_rows(ref, base, n, pitch=SLAB):
    return [ref[pl.ds(base + s, n, stride=pitch), :] for s in range(SLAB)]


def _ada_kernel(c_ref, w_ref, b_ref, o_ref):
    sc = _silu(c_ref[...])
    o_ref[0] = jnp.dot(sc.astype(BF16), w_ref[0].astype(BF16), preferred_element_type=F32) + b_ref[0]


def _ada(c_pad, ada_w, ada_b):
    n_layers, d, n = ada_w.shape
    tn = 1024
    return pl.pallas_call(
        _ada_kernel,
        grid=(n_layers, n // tn),
        in_specs=[
            pl.BlockSpec((8, d), lambda l, j: (0, 0)),
            pl.BlockSpec((1, d, tn), lambda l, j: (l, 0, j)),
            pl.BlockSpec((1, 1, tn), lambda l, j: (l, 0, j)),
        ],
        out_specs=pl.BlockSpec((1, 8, tn), lambda l, j: (l, 0, j)),
        out_shape=jax.ShapeDtypeStruct((n_layers, 8, n), F32),
        compiler_params=_cparams(("parallel", "parallel")),
        name="ada",
    )(c_pad, ada_w, ada_b.reshape(n_layers, 1, n))


_NT = (((1,), (1,)), ((), ()))


def _inproj_kernel(x_ref, nw_ref, s_ref, sh_ref, wt_ref, wdt_ref, o_ref, dt_ref, h_scr, wdt_scr):
    @pl.when(pl.program_id(1) == 0)
    def _():
        x = x_ref[...]
        ms = jnp.mean(x * x, axis=-1, keepdims=True)
        y = x * lax.rsqrt(ms + EPS) * nw_ref[...]
        h = (y * (1.0 + s_ref[...]) + sh_ref[...]).astype(BF16)
        h_scr[...] = h
        wdt_scr[...] = jnp.zeros(wdt_scr.shape, wdt_scr.dtype)
        wdt_scr[0:SSD_HEADS, :] = wdt_ref[...].astype(BF16)
        dt_ref[...] = lax.dot_general(h, wdt_scr[...], _NT, preferred_element_type=F32)

    o_ref[...] = lax.dot_general(h_scr[...], wt_ref[...].astype(BF16), _NT, preferred_element_type=F32)


def _inproj(xf, norm_w, mod, w_in_t, layer, seq):
    t, d = xf.shape
    tm = min(1024, seq)
    tn = 1152
    per_batch = seq // tm
    return pl.pallas_call(
        _inproj_kernel,
        grid=(t // tm, MAIN_PROJ // tn),
        in_specs=[
            pl.BlockSpec((tm, d), lambda i, j: (i, 0)),
            pl.BlockSpec((1, d), lambda i, j: (0, 0)),
            pl.BlockSpec((None, None, 1, d), lambda i, j: (i // per_batch, 1, 0, 0)),
            pl.BlockSpec((None, None, 1, d), lambda i, j: (i // per_batch, 0, 0, 0)),
            pl.BlockSpec((None, tn, d), lambda i, j: (layer, j, 0)),
            pl.BlockSpec((None, SSD_HEADS, d), lambda i, j: (layer, MAIN_PROJ // SSD_HEADS, 0)),
        ],
        out_specs=[
            pl.BlockSpec((tm, tn), lambda i, j: (i, j)),
            pl.BlockSpec((tm, LANES), lambda i, j: (i, 0)),
        ],
        out_shape=[
            jax.ShapeDtypeStruct((t, MAIN_PROJ), F32),
            jax.ShapeDtypeStruct((t, LANES), F32),
        ],
        scratch_shapes=[
            pltpu.VMEM((tm, d), BF16),
            pltpu.VMEM((LANES, d), BF16),
        ],
        compiler_params=_cparams(("parallel", "arbitrary")),
        name="inproj",
    )(xf, norm_w.reshape(1, d), mod, mod, w_in_t, w_in_t)


def _mixer_kernel(u_ref, v_ref, z_ref, xbc_ref, dt_ref,
                  lnw_ref, lnb_ref, ws_ref, bst_ref, cw_ref, cb_ref, dtb_ref, alog_ref,
                  dsk_ref, nw_ref, y_ref, buf_scr, xa_scr, state_scr, ys_scr):
    @pl.when(pl.program_id(1) == 0)
    def _():
        buf_scr[0:HALO, :] = jnp.zeros((HALO, CONV_DIM), F32)
        state_scr[...] = jnp.zeros(state_scr.shape, F32)

    row = lax.broadcasted_iota(I32, (CHUNK, CHUNK), 0)
    col = lax.broadcasted_iota(I32, (CHUNK, CHUNK), 1)
    tril = row >= col
    lane_lo = col < (LANES // 2)
    lane_lo_row = lane_lo[0:1, :]

    for h in range(GM_HEADS):
        sl = slice(h * LANES, (h + 1) * LANES)
        gu = _gelu(u_ref[:, sl])
        gv = _gelu(v_ref[:, sl])
        mu = jnp.mean(gv, axis=-1, keepdims=True)
        dv = gv - mu
        var = jnp.mean(dv * dv, axis=-1, keepdims=True)
        vn = dv * lax.rsqrt(var + EPS) * lnw_ref[:, sl] + lnb_ref[:, sl]
        w = jnp.where(tril, ws_ref[h], 0.0).astype(BF16)
        s = jnp.dot(w, vn.astype(BF16), preferred_element_type=F32) + bst_ref[:, h:h + 1]
        y_ref[:, sl] = (gu * s).astype(y_ref.dtype)

    buf_scr[HALO:HALO + CHUNK, :] = xbc_ref[...]
    for cb in range(CONV_DIM // 256):
        cs_ = slice(cb * 256, (cb + 1) * 256)
        acc = cb_ref[:, cs_] + cw_ref[0:1, cs_] * buf_scr[HALO - 3:HALO - 3 + CHUNK, cs_]
        for k in range(1, SSD_CONV):
            acc = acc + cw_ref[k:k + 1, cs_] * buf_scr[HALO - 3 + k:HALO - 3 + k + CHUNK, cs_]
        xa_scr[:, cs_] = _silu(acc)
    buf_scr[0:HALO, :] = buf_scr[CHUNK:CHUNK + HALO, :]

    dt = _softplus(dt_ref[...] + dtb_ref[...])
    a = -jnp.exp(alog_ref[...])
    ad = dt * a
    cs = jnp.dot(tril.astype(F32), ad, precision=lax.Precision.HIGHEST, preferred_element_type=F32)
    cs_t = cs.T
    last = cs[CHUNK - 1:CHUNK, :]
    ds = jnp.exp(last - cs)
    ecs = jnp.exp(cs)
    cd = jnp.exp(last)

    def pair_expand(m, p):
        return jnp.where(lane_lo[0:m.shape[0], :], m[:, 2 * p:2 * p + 1], m[:, 2 * p + 1:2 * p + 2])

    pairs_per_group = SSD_HEADS // SSD_GROUPS // 2
    gw = SSD_WIDTH // SSD_GROUPS
    for g in range(SSD_GROUPS):
        bm_g = xa_scr[:, SSD_WIDTH + g * SSD_STATE:SSD_WIDTH + (g + 1) * SSD_STATE]
        cm_g = xa_scr[:, SSD_WIDTH + (SSD_GROUPS + g) * SSD_STATE:SSD_WIDTH + (SSD_GROUPS + g + 1) * SSD_STATE]
        cmb = cm_g.astype(BF16)
        bmb = bm_g.astype(BF16)
        cbm = lax.dot_general(cmb, bmb, (((1,), (1,)), ((), ())), preferred_element_type=F32)
        bm_t = bm_g.T.astype(BF16)
        st_prev = state_scr[:, g * gw:(g + 1) * gw]
        y_off = jnp.dot(cmb, st_prev.astype(BF16), preferred_element_type=F32)
        xdds = []
        cds = []
        for q in range(pairs_per_group):
            p = g * pairs_per_group + q
            sl = slice(p * LANES, (p + 1) * LANES)
            xs_p = xa_scr[:, sl]
            xd = xs_p * pair_expand(dt, p)
            xdb = xd.astype(BF16)
            ys = []
            for hh in (2 * p, 2 * p + 1):
                diff = cs[:, hh:hh + 1] - cs_t[hh:hh + 1, :]
                lm = jnp.where(tril, jnp.exp(jnp.where(tril, diff, 0.0)), 0.0)
                wmat = (cbm * lm).astype(BF16)
                ys.append(jnp.dot(wmat, xdb, preferred_element_type=F32))
            y_diag = jnp.where(lane_lo, ys[0], ys[1])
            y = y_diag + y_off[:, q * LANES:(q + 1) * LANES] * pair_expand(ecs, p)
            ys_scr[:, sl] = y + dsk_ref[:, sl] * xs_p
            xdds.append((xd * pair_expand(ds, p)).astype(BF16))
            cds.append(jnp.where(lane_lo_row, cd[:, 2 * p:2 * p + 1], cd[:, 2 * p + 1:2 * p + 2]))
        st_new = jnp.dot(bm_t, jnp.concatenate(xdds, axis=1), preferred_element_type=F32)
        state_scr[:, g * gw:(g + 1) * gw] = st_prev * jnp.concatenate(cds, axis=1) + st_new

    for g in range(SSD_GROUPS):
        sl = slice(g * gw, (g + 1) * gw)
        gg = ys_scr[:, sl] * _silu(z_ref[:, sl])
        ms = jnp.mean(gg * gg, axis=-1, keepdims=True)
        y_ref[:, GM_WIDTH + g * gw:GM_WIDTH + (g + 1) * gw] = (
            gg * lax.rsqrt(ms + EPS) * nw_ref[:, sl]).astype(y_ref.dtype)


def _mixer(proj, dt_raw, p, batch, seq):
    t = proj.shape[0]
    nc = seq // CHUNK
    rows = lambda b, c: b * nc + c
    full = lambda shape: pl.BlockSpec(shape, lambda b, c: (0,) * len(shape))
    return pl.pallas_call(
        _mixer_kernel,
        grid=(batch, nc),
        in_specs=[
            pl.BlockSpec((CHUNK, GM_WIDTH), lambda b, c: (rows(b, c), 0)),
            pl.BlockSpec((CHUNK, GM_WIDTH), lambda b, c: (rows(b, c), 1)),
            pl.BlockSpec((CHUNK, SSD_WIDTH), lambda b, c: (rows(b, c), 2)),
            pl.BlockSpec((CHUNK, CONV_DIM), lambda b, c: (rows(b, c), 2)),
            pl.BlockSpec((CHUNK, LANES), lambda b, c: (rows(b, c), 0)),
            full((1, GM_WIDTH)), full((1, GM_WIDTH)),
            full((GM_HEADS, CHUNK, CHUNK)), full((CHUNK, GM_HEADS)),
            full((SSD_CONV, CONV_DIM)), full((1, CONV_DIM)),
            full((1, LANES)), full((1, LANES)),
            full((1, SSD_WIDTH)), full((1, SSD_WIDTH)),
        ],
        out_specs=pl.BlockSpec((CHUNK, GM_WIDTH + SSD_WIDTH), lambda b, c: (rows(b, c), 0)),
        out_shape=jax.ShapeDtypeStruct((t, GM_WIDTH + SSD_WIDTH), BF16),
        scratch_shapes=[
            pltpu.VMEM((HALO + CHUNK, CONV_DIM), F32),
            pltpu.VMEM((CHUNK, CONV_DIM), F32),
            pltpu.VMEM((SSD_STATE, SSD_WIDTH), F32),
            pltpu.VMEM((CHUNK, SSD_WIDTH), F32),
        ],
        compiler_params=_cparams(("parallel", "arbitrary")),
        name="mixer",
    )(proj, proj, proj, proj, dt_raw,
      p["lnw"], p["lnb"], p["ws"], p["bst"], p["cw"], p["cb"], p["dtb"], p["alog"], p["dsk"], p["nw"])


def _first_max(vals, axis_iota, n):
    m = jnp.max(vals, axis=0, keepdims=True)
    idx = jnp.min(jnp.where(vals == m, axis_iota, n), axis=0, keepdims=True)
    return m, idx


def _post_kernel(y_ref, wout_ref, x_ref, g1_ref, n2w_ref, s2_ref, sh2_ref, rwt_ref, rb_ref,
                 xo_ref, h2_ref, eidx_ref, gate_ref, rank_ref, cnt_ref, carry_scr):
    @pl.when(pl.program_id(0) == 0)
    def _():
        carry_scr[...] = jnp.zeros(carry_scr.shape, F32)

    tm = x_ref.shape[0]
    mix = jnp.dot(y_ref[...], wout_ref[...], preferred_element_type=F32)
    x = x_ref[...] + g1_ref[...] * mix
    xo_ref[...] = x
    ms = jnp.mean(x * x, axis=-1, keepdims=True)
    h = x * lax.rsqrt(ms + EPS) * n2w_ref[...] * (1.0 + s2_ref[...]) + sh2_ref[...]
    _store_slab_rows(h2_ref, 0, h)

    logits_t = lax.dot_general(rwt_ref[...], h.astype(BF16), _NT, preferred_element_type=F32)
    scores = jax.nn.sigmoid(logits_t)
    biased = scores + rb_ref[...]

    sub = lax.broadcasted_iota(I32, (EXPERTS_PER_GROUP, tm), 0)
    neg = jnp.float32(-jnp.inf)
    best = None
    for g in range(N_EXPERT_GROUPS):
        grp = biased[g * EXPERTS_PER_GROUP:(g + 1) * EXPERTS_PER_GROUP, :]
        m1, i1 = _first_max(grp, sub, EXPERTS_PER_GROUP)
        m2, i2 = _first_max(jnp.where(sub == i1, neg, grp), sub, EXPERTS_PER_GROUP)
        gs = m1 + m2
        if best is None:
            best, bi, l1, l2 = gs, jnp.zeros((1, tm), I32), i1, i2
        else:
            upd = gs > best
            best = jnp.where(upd, gs, best)
            bi = jnp.where(upd, g, bi)
            l1 = jnp.where(upd, i1, l1)
            l2 = jnp.where(upd, i2, l2)
    e0 = bi * EXPERTS_PER_GROUP + l1
    e1 = bi * EXPERTS_PER_GROUP + l2

    eio = lax.broadcasted_iota(I32, (N_EXPERTS, tm), 0)
    oh0 = eio == e0
    oh1 = eio == e1
    s0 = jnp.sum(jnp.where(oh0, scores, 0.0), axis=0, keepdims=True)
    s1 = jnp.sum(jnp.where(oh1, scores, 0.0), axis=0, keepdims=True)
    tot = s0 + s1
    eidx_ref[0:1, :] = e0
    eidx_ref[1:2, :] = e1
    gate_ref[0:1, :] = s0 / tot
    gate_ref[1:2, :] = s1 / tot

    ohs = oh0.astype(F32) + oh1.astype(F32)
    tr = lax.broadcasted_iota(I32, (tm, tm), 0)
    tc = lax.broadcasted_iota(I32, (tm, tm), 1)
    before = (tr < tc).astype(BF16)
    prefix = jnp.dot(ohs.astype(BF16), before, preferred_element_type=F32)
    base = carry_scr[:, 0:1] + prefix
    rank_ref[0:1, :] = jnp.sum(jnp.where(oh0, base, 0.0), axis=0, keepdims=True).astype(I32)
    rank_ref[1:2, :] = jnp.sum(jnp.where(oh1, base, 0.0), axis=0, keepdims=True).astype(I32)
    carry_scr[...] = carry_scr[...] + jnp.sum(ohs, axis=1, keepdims=True)
    cnt_ref[...] = carry_scr[...]


def _post(y_mix, w_out, xf, mod, norm2_w, rw_t, rb, seq):
    t, d = xf.shape
    dm = y_mix.shape[1]
    tm = min(512, seq)
    per_batch = seq // tm
    modspec = lambda k: pl.BlockSpec((None, None, 1, d), lambda i: (i // per_batch, k, 0, 0))
    tok = pl.BlockSpec((2, tm), lambda i: (0, i))
    return pl.pallas_call(
        _post_kernel,
        grid=(t // tm,),
        in_specs=[
            pl.BlockSpec((tm, dm), lambda i: (i, 0)),
            pl.BlockSpec((dm, d), lambda i: (0, 0)),
            pl.BlockSpec((tm, d), lambda i: (i, 0)),
            modspec(2),
            pl.BlockSpec((1, d), lambda i: (0, 0)),
            modspec(4),
            modspec(3),
            pl.BlockSpec((N_EXPERTS, d), lambda i: (0, 0)),
            pl.BlockSpec((N_EXPERTS, 1), lambda i: (0, 0)),
        ],
        out_specs=[
            pl.BlockSpec((tm, d), lambda i: (i, 0)),
            pl.BlockSpec((tm * SLAB, d // SLAB), lambda i: (i, 0)),
            tok, tok, tok,
            pl.BlockSpec((N_EXPERTS, LANES), lambda i: (0, 0)),
        ],
        out_shape=[
            jax.ShapeDtypeStruct((t, d), F32),
            jax.ShapeDtypeStruct((t * SLAB, d // SLAB), F32),
            jax.ShapeDtypeStruct((2, t), I32),
            jax.ShapeDtypeStruct((2, t), F32),
            jax.ShapeDtypeStruct((2, t), I32),
            jax.ShapeDtypeStruct((N_EXPERTS, LANES), F32),
        ],
        scratch_shapes=[pltpu.VMEM((N_EXPERTS, LANES), F32)],
        compiler_params=_cparams(("arbitrary",)),
        name="post",
    )(y_mix, w_out, xf, mod, norm2_w.reshape(1, d), mod, mod, rw_t, rb.reshape(N_EXPERTS, 1))


META_ROWS = 8
META_LANES = 256
ROW_BLK_E, ROW_CNT, ROW_PSTART, ROW_NUSED, ROW_NVALID = 0, 1, 2, 3, 4


def _col_to_row(colv):
    n = colv.shape[0]
    r = lax.broadcasted_iota(I32, (n, n), 0)
    c = lax.broadcasted_iota(I32, (n, n), 1)
    return jnp.sum(jnp.where(r == c, colv, 0.0), axis=0, keepdims=True)


def _meta_kernel(cnt_ref, eidx_ref, rank_ref, dest_ref, meta_ref):
    t = eidx_ref.shape[1]
    cnt = cnt_ref[...]
    nblk = jnp.floor((cnt + (MOE_BLOCK - 1)) * (1.0 / MOE_BLOCK))
    r = lax.broadcasted_iota(I32, (N_EXPERTS, N_EXPERTS), 0)
    c = lax.broadcasted_iota(I32, (N_EXPERTS, N_EXPERTS), 1)
    lower = (c < r).astype(BF16)
    pstart = jnp.dot(lower, nblk.astype(BF16), preferred_element_type=F32)
    pend = pstart + nblk

    chunk = min(1024, t)
    for j in range(t // chunk):
        sl = slice(j * chunk, (j + 1) * chunk)
        eio = lax.broadcasted_iota(I32, (N_EXPERTS, chunk), 0)
        for k in range(2):
            oh = eio == eidx_ref[k:k + 1, sl]
            ps = jnp.sum(jnp.where(oh, pstart[:, 0:1], 0.0), axis=0, keepdims=True)
            dest_ref[k:k + 1, sl] = (ps * MOE_BLOCK).astype(I32) + rank_ref[k:k + 1, sl]

    bl = lax.broadcasted_iota(I32, (N_EXPERTS, META_LANES), 1).astype(F32)
    raw = jnp.sum((pend[:, 0:1] <= bl).astype(F32), axis=0, keepdims=True)
    raw = jnp.minimum(raw, N_EXPERTS - 1.0)
    nused = pend[N_EXPERTS - 1:N_EXPERTS, 0:1]
    used = bl[0:1, :] < nused
    last_e = jnp.max(jnp.where(used, raw, 0.0), axis=1, keepdims=True)
    meta_ref[...] = jnp.zeros(meta_ref.shape, I32)
    meta_ref[ROW_BLK_E:ROW_BLK_E + 1, :] = jnp.where(used, raw, last_e).astype(I32)
    meta_ref[ROW_CNT:ROW_CNT + 1, 0:N_EXPERTS] = _col_to_row(cnt[:, 0:1]).astype(I32)
    meta_ref[ROW_PSTART:ROW_PSTART + 1, 0:N_EXPERTS] = (_col_to_row(pstart[:, 0:1]) * MOE_BLOCK).astype(I32)
    meta_ref[ROW_NUSED:ROW_NUSED + 1, :] = jnp.broadcast_to(nused, (1, META_LANES)).astype(I32)
    mine = lax.broadcasted_iota(I32, (N_EXPERTS, META_LANES), 0).astype(F32) == raw
    cnt_b = jnp.sum(jnp.where(mine, cnt[:, 0:1], 0.0), axis=0, keepdims=True)
    first_b = jnp.sum(jnp.where(mine, pstart[:, 0:1], 0.0), axis=0, keepdims=True)
    nvalid = jnp.clip(cnt_b - (bl[0:1, :] - first_b) * MOE_BLOCK, 0.0, float(MOE_BLOCK))
    meta_ref[ROW_NVALID:ROW_NVALID + 1, :] = jnp.where(used, nvalid, 0.0).astype(I32)


def _meta(cnt, eidx, rank):
    t = eidx.shape[1]
    full = lambda shape: pl.BlockSpec(shape, lambda: (0,) * len(shape))
    return pl.pallas_call(
        _meta_kernel,
        in_specs=[full((N_EXPERTS, LANES)), full((2, t)), full((2, t))],
        out_specs=[full((2, t)), full((META_ROWS, META_LANES))],
        out_shape=[jax.ShapeDtypeStruct((2, t), I32), jax.ShapeDtypeStruct((META_ROWS, META_LANES), I32)],
        name="meta",
    )(cnt, eidx, rank)


def _invert_kernel(dest0_ref, dest1_ref, code_ref):
    i = pl.program_id(0)
    tm = dest0_ref.shape[0]
    t = tm * pl.num_programs(0)

    @pl.when(i == 0)
    def _():
        def clear(p, carry):
            code_ref[p] = 0
            return carry
        lax.fori_loop(0, code_ref.shape[0], clear, 0, unroll=16)

    def put(tok, carry):
        code_ref[dest0_ref[tok]] = i * tm + tok
        code_ref[dest1_ref[tok]] = t + i * tm + tok
        return carry
    lax.fori_loop(0, tm, put, 0, unroll=8)


def _invert(dest, n_rows):
    t = dest.shape[1]
    tm = min(1024, t)
    slots = pl.BlockSpec((tm,), lambda i: (i,), memory_space=pltpu.SMEM)
    return pl.pallas_call(
        _invert_kernel,
        grid=(t // tm,),
        in_specs=[slots, slots],
        out_specs=pl.BlockSpec((n_rows,), lambda i: (0,), memory_space=pltpu.SMEM),
        out_shape=jax.ShapeDtypeStruct((n_rows,), I32),
        compiler_params=_cparams(("arbitrary",)),
        name="invert",
    )(dest[0], dest[1])


def _expert_kernel(be_sm, nu_sm, cnt_sm, nv_sm, *refs, layer):
    code_refs = refs[:GATHER_AHEAD + 1]
    code0_ref = code_refs[0]
    (h2_hbm, wg_hbm, wu_hbm, wd_hbm, yk_hbm, xbuf, obuf, land_g, land_u, land_d, wg_scr, wu_scr, wd_scr,
     sems, gsem, ssem, st_sm) = refs[GATHER_AHEAD + 1:]
    b = pl.program_id(0)
    nu = nu_sm[0]
    par = b & 1
    t = h2_hbm.shape[0] // SLAB
    n_blocks = pl.num_programs(0)
    blk_rows = MOE_BLOCK * VMEM_PITCH

    def hbm_rows(ref, index, n=1):
        return ref.at[pl.ds(pl.multiple_of(index * SLAB, SUBLANES), n * SLAB)]

    def vmem_row(ref, slot, r):
        return ref.at[pl.ds(pl.multiple_of((slot * MOE_BLOCK + r) * VMEM_PITCH, SUBLANES), SLAB)]

    def vmem_span(ref, slot, n):
        return ref.at[pl.ds(pl.multiple_of(slot * blk_rows, SUBLANES), n * SLAB)]

    def gather_row(ahead, r, slot):
        code = code_refs[ahead][0, r]
        if t & (t - 1) == 0:
            tok = code & (t - 1)
        else:
            tok = jnp.where(code >= t, code - t, code)
        return pltpu.make_async_copy(hbm_rows(h2_hbm, tok), vmem_row(xbuf, slot, r), gsem.at[slot])

    def scatter_row(r, slot):
        return pltpu.make_async_copy(vmem_row(obuf, slot, r), hbm_rows(yk_hbm, code0_ref[0, r]), ssem.at[slot])

    def gather_sized(slot):
        return lambda n: pltpu.make_async_copy(hbm_rows(h2_hbm, 0, n), vmem_span(xbuf, slot, n), gsem.at[slot])

    def scatter_sized(slot):
        return lambda n: pltpu.make_async_copy(vmem_span(obuf, slot, n), hbm_rows(yk_hbm, 0, n), ssem.at[slot])

    def start_rows(n, row_copy, queues):
        groups = n // SUBLANES

        def eight(g, c):
            for u in range(SUBLANES):
                row_copy(g * SUBLANES + u).start(priority=queues[u % len(queues)])
            return c
        lax.fori_loop(0, groups, eight, 0)
        lax.fori_loop(groups * SUBLANES, n, lambda r, c: (row_copy(r).start(priority=queues[0]), c)[1], 0)

    def wait_rows(n, row_copy, sized_copy):
        @pl.when(n == MOE_BLOCK)
        def _():
            sized_copy(MOE_BLOCK).wait()

        @pl.when(n != MOE_BLOCK)
        def _():
            groups = n // SUBLANES
            lax.fori_loop(0, groups, lambda g, c: (sized_copy(SUBLANES).wait(), c)[1], 0)
            lax.fori_loop(groups * SUBLANES, n, lambda r, c: (row_copy(r).wait(), c)[1], 0)

    def fetch(e, slot):
        return (pltpu.make_async_copy(wg_hbm.at[layer, e], land_g.at[slot], sems.at[slot, 0]),
                pltpu.make_async_copy(wu_hbm.at[layer, e], land_u.at[slot], sems.at[slot, 1]),
                pltpu.make_async_copy(wd_hbm.at[layer, e], land_d.at[slot], sems.at[slot, 2]))

    def next_used(e):
        return lax.while_loop(lambda n: (n < N_EXPERTS) & (cnt_sm[jnp.minimum(n, N_EXPERTS - 1)] == 0),
                              lambda n: n + 1, e + 1)

    def start_next(slot):
        nxt = next_used(st_sm[1])

        @pl.when(nxt < N_EXPERTS)
        def _():
            for cp in fetch(nxt, slot):
                cp.start(priority=WEIGHT_QUEUE)
        st_sm[1] = nxt

    @pl.when(b == 0)
    def _():
        st_sm[0] = 0
        st_sm[1] = -1
        start_next(0)
        start_next(1)
        xbuf[...] = jnp.zeros(xbuf.shape, xbuf.dtype)
        for ahead in range(GATHER_AHEAD):
            @pl.when(ahead < nu)
            def _():
                start_rows(nv_sm[ahead], lambda r: gather_row(ahead, r, ahead), GATHER_QUEUES)

    xslot = lax.rem(b, GATHER_AHEAD + 1)

    @pl.when(b < nu)
    def _():
        @pl.when(b + GATHER_AHEAD < nu)
        def _():
            start_rows(nv_sm[jnp.minimum(b + GATHER_AHEAD, n_blocks - 1)],
                       lambda r: gather_row(GATHER_AHEAD, r, lax.rem(b + GATHER_AHEAD, GATHER_AHEAD + 1)),
                       GATHER_QUEUES)

        prev = be_sm[jnp.maximum(b - 1, 0)]

        @pl.when((b == 0) | (be_sm[b] != prev))
        def _():
            slot = st_sm[0] & 1
            for cp in fetch(be_sm[b], slot):
                cp.wait()
            wg_scr[...] = land_g[slot].astype(BF16)
            wu_scr[...] = land_u[slot].astype(BF16)
            wd_scr[...] = land_d[slot].astype(BF16)
            st_sm[0] = st_sm[0] + 1
            start_next(slot)

        wait_rows(nv_sm[b], lambda r: gather_row(0, r, xslot), gather_sized(xslot))

        xb = jnp.concatenate(_load_slab_rows(xbuf, xslot * blk_rows, MOE_BLOCK, VMEM_PITCH),
                             axis=1).astype(BF16)
        gate = jnp.dot(xb, wg_scr[...], preferred_element_type=F32)
        up = jnp.dot(xb, wu_scr[...], preferred_element_type=F32)
        act = (_silu(gate) * up).astype(BF16)
        out = jnp.dot(act, wd_scr[...], preferred_element_type=F32)

        @pl.when(b >= 2)
        def _():
            wait_rows(nv_sm[jnp.maximum(b - 2, 0)], lambda r: scatter_row(r, par), scatter_sized(par))

        _store_slab_rows(obuf, par * blk_rows, out, VMEM_PITCH)
        start_rows(nv_sm[b], lambda r: scatter_row(r, par), SCATTER_QUEUES)

        @pl.when(b == nu - 1)
        def _():
            @pl.when(b >= 1)
            def _():
                wait_rows(nv_sm[jnp.maximum(b - 1, 0)], lambda r: scatter_row(r, 1 - par),
                          scatter_sized(1 - par))
            wait_rows(nv_sm[b], lambda r: scatter_row(r, par), scatter_sized(par))


def _experts(blk_e, nused, cnt_row, nvalid, code_sorted, h2, w_gate, w_up, w_down, layer):
    sw = h2.shape[1]
    t = h2.shape[0] // SLAB
    d = SLAB * sw
    de = w_gate.shape[3]
    n_blocks = code_sorted.shape[0] // MOE_BLOCK
    codes = code_sorted.reshape(n_blocks, 1, MOE_BLOCK)
    hbm = pl.BlockSpec(memory_space=pl.ANY)
    id_spec = lambda ahead: pl.BlockSpec(
        (None, 1, MOE_BLOCK), lambda b, be, nu, cnt, nv: (jnp.minimum(b + ahead, nu[0] - 1), 0, 0),
        memory_space=pltpu.SMEM)
    return pl.pallas_call(
        functools.partial(_expert_kernel, layer=layer),
        grid_spec=pltpu.PrefetchScalarGridSpec(
            num_scalar_prefetch=4,
            grid=(n_blocks,),
            in_specs=[id_spec(ahead) for ahead in range(GATHER_AHEAD + 1)] + [hbm, hbm, hbm, hbm],
            out_specs=hbm,
            scratch_shapes=[
                pltpu.VMEM(((GATHER_AHEAD + 1) * MOE_BLOCK * VMEM_PITCH, sw), F32),
                pltpu.VMEM((2 * MOE_BLOCK * VMEM_PITCH, sw), F32),
                pltpu.VMEM((2, d, de), F32),
                pltpu.VMEM((2, d, de), F32),
                pltpu.VMEM((2, de, d), F32),
                pltpu.VMEM((d, de), BF16),
                pltpu.VMEM((d, de), BF16),
                pltpu.VMEM((de, d), BF16),
                pltpu.SemaphoreType.DMA((2, 3)),
                pltpu.SemaphoreType.DMA((GATHER_AHEAD + 1,)),
                pltpu.SemaphoreType.DMA((2,)),
                pltpu.SMEM((2,), I32),
            ],
        ),
        out_shape=jax.ShapeDtypeStruct((2 * t * SLAB, sw), F32),
        compiler_params=_cparams(("arbitrary",)),
        name="experts",
    )(blk_e, nused, cnt_row, nvalid, *([codes] * (GATHER_AHEAD + 1)), h2, w_gate, w_up, w_down)


def _combine_kernel(gate_ref, x_ref, g2_ref, fw_ref, y0_ref, y1_ref, out_ref, *, final):
    tm = x_ref.shape[0]
    r = lax.broadcasted_iota(I32, (tm, tm), 0)
    c = lax.broadcasted_iota(I32, (tm, tm), 1)
    eye = r == c
    w0 = jnp.sum(jnp.where(eye, gate_ref[0:1, :], 0.0), axis=1, keepdims=True)
    w1 = jnp.sum(jnp.where(eye, gate_ref[1:2, :], 0.0), axis=1, keepdims=True)
    y = jnp.concatenate([c0 * w0 + c1 * w1 for c0, c1 in zip(_load_slab_rows(y0_ref, 0, tm),
                                                             _load_slab_rows(y1_ref, 0, tm))], axis=1)
    x = x_ref[...] + g2_ref[...] * y
    if final:
        ms = jnp.mean(x * x, axis=-1, keepdims=True)
        x = x * lax.rsqrt(ms + EPS) * fw_ref[...]
    out_ref[...] = x


def _combine(gate, xf, mod, final_w, yk, seq, final):
    t, d = xf.shape
    tm = min(512, seq)
    per_batch = seq // tm
    return pl.pallas_call(
        functools.partial(_combine_kernel, final=final),
        grid=(t // tm,),
        in_specs=[
            pl.BlockSpec((2, tm), lambda i: (0, i)),
            pl.BlockSpec((tm, d), lambda i: (i, 0)),
            pl.BlockSpec((None, None, 1, d), lambda i: (i // per_batch, 5, 0, 0)),
            pl.BlockSpec((1, d), lambda i: (0, 0)),
            pl.BlockSpec((tm * SLAB, yk.shape[1]), lambda i: (i, 0)),
            pl.BlockSpec((tm * SLAB, yk.shape[1]), lambda i: (i + t // tm, 0)),
        ],
        out_specs=pl.BlockSpec((tm, d), lambda i: (i, 0)),
        out_shape=jax.ShapeDtypeStruct((t, d), F32),
        compiler_params=_cparams(("parallel",)),
        name="combine",
    )(gate, xf, mod, final_w.reshape(1, d), yk, yk)


def _pad_lanes(v, n=LANES):
    return jnp.pad(v, (0, n - v.shape[0])).reshape(1, n)


def kernel(x, c, ada_w, ada_b, norm1_w, w_in, gm_ln_w, gm_ln_b, gm_ws, gm_bs, conv_w, conv_b, dt_bias, a_log,
           d_skip, ssd_norm_w, w_out, norm2_w, router_w, router_b, exp_w_gate, exp_w_up, exp_w_down,
           final_norm_w):
    batch, seq, d = x.shape
    t = batch * seq
    depth = ada_w.shape[0]
    assert batch <= 8 and seq % CHUNK == 0 and w_in.shape[2] == MAIN_PROJ + SSD_HEADS
    n_rows = (-(-(t * 2) // MOE_BLOCK) + N_EXPERTS) * MOE_BLOCK
    assert n_rows // MOE_BLOCK <= META_LANES

    ada = _ada(jnp.pad(c, ((0, 8 - batch), (0, 0))), ada_w, ada_b)
    rw_t = router_w.T.astype(BF16)
    w_in_t = jnp.swapaxes(w_in, 1, 2)
    xf = x.reshape(t, d)
    for l in range(depth):
        mod = ada[l, :batch].reshape(batch, 6, 1, d)
        proj, dt_raw = _inproj(xf, norm1_w[l], mod, w_in_t, l, seq)
        mixer_params = dict(
            lnw=gm_ln_w[l].reshape(1, GM_WIDTH), lnb=gm_ln_b[l].reshape(1, GM_WIDTH),
            ws=gm_ws[l], bst=gm_bs[l].T,
            cw=conv_w[l], cb=conv_b[l].reshape(1, CONV_DIM),
            dtb=_pad_lanes(dt_bias[l]), alog=_pad_lanes(a_log[l]),
            dsk=jnp.repeat(d_skip[l], SSD_WIDTH // SSD_HEADS).reshape(1, SSD_WIDTH),
            nw=ssd_norm_w[l].reshape(1, SSD_WIDTH))
        y_mix = _mixer(proj, dt_raw, mixer_params, batch, seq)
        xf, h2, eidx, gate, rank, cnt = _post(y_mix, w_out[l].astype(BF16), xf, mod, norm2_w[l], rw_t,
                                              router_b, seq)
        dest, meta = _meta(cnt, eidx, rank)
        n_blocks = n_rows // MOE_BLOCK
        yk = _experts(meta[ROW_BLK_E, :n_blocks], meta[ROW_NUSED, :1], meta[ROW_CNT, :N_EXPERTS],
                      meta[ROW_NVALID, :n_blocks], _invert(dest, n_rows), h2, exp_w_gate, exp_w_up,
                      exp_w_down, l)
        xf = _combine(gate, xf, mod, final_norm_w, yk, seq, final=(l == depth - 1))
    return xf.reshape(batch, seq, d)
```

```python
import functools

import jax
import jax.numpy as jnp
from jax import lax
from jax.experimental import pallas as pl
from jax.experimental.pallas import tpu as pltpu

F32 = jnp.float32
BF16 = jnp.bfloat16
I32 = jnp.int32

EPS = 1e-6
LANES = 128
CHUNK = 128
GM_HEADS = 8
GM_WIDTH = 1024
SSD_WIDTH = 1024
SSD_HEADS = 16
SSD_GROUPS = 2
SSD_STATE = 128
SSD_CONV = 4
CONV_DIM = SSD_WIDTH + 2 * SSD_GROUPS * SSD_STATE
MAIN_PROJ = 2 * GM_WIDTH + SSD_WIDTH + CONV_DIM
N_EXPERTS = 64
EXPERTS_PER_GROUP = 8
N_EXPERT_GROUPS = 8
MOE_BLOCK = 128
HALO = 8
SLAB = 16
VMEM_PITCH = 24
SUBLANES = 8
WEIGHT_QUEUE = 1
GATHER_QUEUES = (0,)
GATHER_AHEAD = 1
COMBINE_ROWS = 128
SCATTER_QUEUES = (0, 1)
VMEM_LIMIT = 56 * 1024 * 1024


def _cparams(sem, vmem=VMEM_LIMIT):
    return pltpu.CompilerParams(dimension_semantics=sem, vmem_limit_bytes=vmem)


def _silu(x):
    half = 0.5 * x
    return half * (1.0 + jnp.tanh(half))


def _gelu(x):
    return 0.5 * x * (1.0 + lax.erf(x * 0.7071067811865476))


def _softplus(x):
    return jnp.maximum(x, 0.0) + jnp.log1p(jnp.exp(-jnp.abs(x)))


def _store_slab_rows(ref, base, x, pitch=SLAB):
    n = x.shape[0]
    w = ref.shape[1]
    for s in range(SLAB):
        ref[pl.ds(base + s, n, stride=pitch), :] = x[:, s * w:(s + 1) * w]


def _load_slab_rows(ref, base, n, pitch=SLAB):
    return [ref[pl.ds(base + s, n, stride=pitch), :] for s in range(SLAB)]


def _ada_kernel(c_ref, w_ref, b_ref, o_ref):
    sc = _silu(c_ref[...])
    o_ref[0] = jnp.dot(sc.astype(BF16), w_ref[0].astype(BF16), preferred_element_type=F32) + b_ref[0]


def _ada(c_pad, ada_w, ada_b):
    n_layers, d, n = ada_w.shape
    tn = 1024
    return pl.pallas_call(
        _ada_kernel,
        grid=(n_layers, n // tn),
        in_specs=[
            pl.BlockSpec((8, d), lambda l, j: (0, 0)),
            pl.BlockSpec((1, d, tn), lambda l, j: (l, 0, j)),
            pl.BlockSpec((1, 1, tn), lambda l, j: (l, 0, j)),
        ],
        out_specs=pl.BlockSpec((1, 8, tn), lambda l, j: (l, 0, j)),
        out_shape=jax.ShapeDtypeStruct((n_layers, 8, n), F32),
        compiler_params=_cparams(("parallel", "parallel")),
        name="ada",
    )(c_pad, ada_w, ada_b.reshape(n_layers, 1, n))


_NT = (((1,), (1,)), ((), ()))


def _inproj_kernel(x_ref, nw_ref, s_ref, sh_ref, wt_ref, wdt_ref, o_ref, dt_ref, h_scr, wdt_scr):
    @pl.when(pl.program_id(1) == 0)
    def _():
        x = x_ref[...]
        ms = jnp.mean(x * x, axis=-1, keepdims=True)
        y = x * lax.rsqrt(ms + EPS) * nw_ref[...]
        h = (y * (1.0 + s_ref[...]) + sh_ref[...]).astype(BF16)
        h_scr[...] = h
        wdt_scr[...] = jnp.zeros(wdt_scr.shape, wdt_scr.dtype)
        wdt_scr[0:SSD_HEADS, :] = wdt_ref[...].astype(BF16)
        dt_ref[...] = lax.dot_general(h, wdt_scr[...], _NT, preferred_element_type=F32)

    o_ref[...] = lax.dot_general(h_scr[...], wt_ref[...].astype(BF16), _NT, preferred_element_type=F32)


def _inproj(xf, norm_w, mod, w_in_t, layer, seq):
    t, d = xf.shape
    tm = min(1024, seq)
    tn = 1152
    per_batch = seq // tm
    return pl.pallas_call(
        _inproj_kernel,
        grid=(t // tm, MAIN_PROJ // tn),
        in_specs=[
            pl.BlockSpec((tm, d), lambda i, j: (i, 0)),
            pl.BlockSpec((1, d), lambda i, j: (0, 0)),
            pl.BlockSpec((None, None, 1, d), lambda i, j: (i // per_batch, 1, 0, 0)),
            pl.BlockSpec((None, None, 1, d), lambda i, j: (i // per_batch, 0, 0, 0)),
            pl.BlockSpec((None, tn, d), lambda i, j: (layer, j, 0)),
            pl.BlockSpec((None, SSD_HEADS, d), lambda i, j: (layer, MAIN_PROJ // SSD_HEADS, 0)),
        ],
        out_specs=[
            pl.BlockSpec((tm, tn), lambda i, j: (i, j)),
            pl.BlockSpec((tm, LANES), lambda i, j: (i, 0)),
        ],
        out_shape=[
            jax.ShapeDtypeStruct((t, MAIN_PROJ), F32),
            jax.ShapeDtypeStruct((t, LANES), F32),
        ],
        scratch_shapes=[
            pltpu.VMEM((tm, d), BF16),
            pltpu.VMEM((LANES, d), BF16),
        ],
        compiler_params=_cparams(("parallel", "arbitrary")),
        name="inproj",
    )(xf, norm_w.reshape(1, d), mod, mod, w_in_t, w_in_t)


def _mixer_kernel(u_ref, v_ref, z_ref, xbc_ref, dt_ref,
                  lnw_ref, lnb_ref, ws_ref, bst_ref, cw_ref, cb_ref, dtb_ref, alog_ref,
                  dsk_ref, nw_ref, y_ref, buf_scr, xa_scr, state_scr, ys_scr):
    @pl.when(pl.program_id(1) == 0)
    def _():
        buf_scr[0:HALO, :] = jnp.zeros((HALO, CONV_DIM), F32)
        state_scr[...] = jnp.zeros(state_scr.shape, F32)

    row = lax.broadcasted_iota(I32, (CHUNK, CHUNK), 0)
    col = lax.broadcasted_iota(I32, (CHUNK, CHUNK), 1)
    tril = row >= col
    lane_lo = col < (LANES // 2)
    lane_lo_row = lane_lo[0:1, :]

    for h in range(GM_HEADS):
        sl = slice(h * LANES, (h + 1) * LANES)
        gu = _gelu(u_ref[:, sl])
        gv = _gelu(v_ref[:, sl])
        mu = jnp.mean(gv, axis=-1, keepdims=True)
        dv = gv - mu
        var = jnp.mean(dv * dv, axis=-1, keepdims=True)
        vn = dv * lax.rsqrt(var + EPS) * lnw_ref[:, sl] + lnb_ref[:, sl]
        w = jnp.where(tril, ws_ref[h], 0.0).astype(BF16)
        s = jnp.dot(w, vn.astype(BF16), preferred_element_type=F32) + bst_ref[:, h:h + 1]
        y_ref[:, sl] = (gu * s).astype(y_ref.dtype)

    buf_scr[HALO:HALO + CHUNK, :] = xbc_ref[...]
    for cb in range(CONV_DIM // 256):
        cs_ = slice(cb * 256, (cb + 1) * 256)
        acc = cb_ref[:, cs_] + cw_ref[0:1, cs_] * buf_scr[HALO - 3:HALO - 3 + CHUNK, cs_]
        for k in range(1, SSD_CONV):
            acc = acc + cw_ref[k:k + 1, cs_] * buf_scr[HALO - 3 + k:HALO - 3 + k + CHUNK, cs_]
        xa_scr[:, cs_] = _silu(acc)
    buf_scr[0:HALO, :] = buf_scr[CHUNK:CHUNK + HALO, :]

    dt = _softplus(dt_ref[...] + dtb_ref[...])
    a = -jnp.exp(alog_ref[...])
    ad = dt * a
    cs = jnp.dot(tril.astype(F32), ad, precision=lax.Precision.HIGHEST, preferred_element_type=F32)
    cs_t = cs.T
    last = cs[CHUNK - 1:CHUNK, :]
    ds = jnp.exp(last - cs)
    ecs = jnp.exp(cs)
    cd = jnp.exp(last)

    def pair_expand(m, p):
        return jnp.where(lane_lo[0:m.shape[0], :], m[:, 2 * p:2 * p + 1], m[:, 2 * p + 1:2 * p + 2])

    pairs_per_group = SSD_HEADS // SSD_GROUPS // 2
    gw = SSD_WIDTH // SSD_GROUPS
    for g in range(SSD_GROUPS):
        bm_g = xa_scr[:, SSD_WIDTH + g * SSD_STATE:SSD_WIDTH + (g + 1) * SSD_STATE]
        cm_g = xa_scr[:, SSD_WIDTH + (SSD_GROUPS + g) * SSD_STATE:SSD_WIDTH + (SSD_GROUPS + g + 1) * SSD_STATE]
        cmb = cm_g.astype(BF16)
        bmb = bm_g.astype(BF16)
        cbm = lax.dot_general(cmb, bmb, (((1,), (1,)), ((), ())), preferred_element_type=F32)
        bm_t = bm_g.T.astype(BF16)
        st_prev = state_scr[:, g * gw:(g + 1) * gw]
        y_off = jnp.dot(cmb, st_prev.astype(BF16), preferred_element_type=F32)
        xdds = []
        cds = []
        for q in range(pairs_per_group):
            p = g * pairs_per_group + q
            sl = slice(p * LANES, (p + 1) * LANES)
            xs_p = xa_scr[:, sl]
            xd = xs_p * pair_expand(dt, p)
            xdb = xd.astype(BF16)
            ys = []
            for hh in (2 * p, 2 * p + 1):
                diff = cs[:, hh:hh + 1] - cs_t[hh:hh + 1, :]
                lm = jnp.where(tril, jnp.exp(jnp.where(tril, diff, 0.0)), 0.0)
                wmat = (cbm * lm).astype(BF16)
                ys.append(jnp.dot(wmat, xdb, preferred_element_type=F32))
            y_diag = jnp.where(lane_lo, ys[0], ys[1])
            y = y_diag + y_off[:, q * LANES:(q + 1) * LANES] * pair_expand(ecs, p)
            ys_scr[:, sl] = y + dsk_ref[:, sl] * xs_p
            xdds.append((xd * pair_expand(ds, p)).astype(BF16))
            cds.append(jnp.where(lane_lo_row, cd[:, 2 * p:2 * p + 1], cd[:, 2 * p + 1:2 * p + 2]))
        st_new = jnp.dot(bm_t, jnp.concatenate(xdds, axis=1), preferred_element_type=F32)
        state_scr[:, g * gw:(g + 1) * gw] = st_prev * jnp.concatenate(cds, axis=1) + st_new

    for g in range(SSD_GROUPS):
        sl = slice(g * gw, (g + 1) * gw)
        gg = ys_scr[:, sl] * _silu(z_ref[:, sl])
        ms = jnp.mean(gg * gg, axis=-1, keepdims=True)
        y_ref[:, GM_WIDTH + g * gw:GM_WIDTH + (g + 1) * gw] = (
            gg * lax.rsqrt(ms + EPS) * nw_ref[:, sl]).astype(y_ref.dtype)


def _mixer(proj, dt_raw, p, batch, seq):
    t = proj.shape[0]
    nc = seq // CHUNK
    rows = lambda b, c: b * nc + c
    full = lambda shape: pl.BlockSpec(shape, lambda b, c: (0,) * len(shape))
    return pl.pallas_call(
        _mixer_kernel,
        grid=(batch, nc),
        in_specs=[
            pl.BlockSpec((CHUNK, GM_WIDTH), lambda b, c: (rows(b, c), 0)),
            pl.BlockSpec((CHUNK, GM_WIDTH), lambda b, c: (rows(b, c), 1)),
            pl.BlockSpec((CHUNK, SSD_WIDTH), lambda b, c: (rows(b, c), 2)),
            pl.BlockSpec((CHUNK, CONV_DIM), lambda b, c: (rows(b, c), 2)),
            pl.BlockSpec((CHUNK, LANES), lambda b, c: (rows(b, c), 0)),
            full((1, GM_WIDTH)), full((1, GM_WIDTH)),
            full((GM_HEADS, CHUNK, CHUNK)), full((CHUNK, GM_HEADS)),
            full((SSD_CONV, CONV_DIM)), full((1, CONV_DIM)),
            full((1, LANES)), full((1, LANES)),
            full((1, SSD_WIDTH)), full((1, SSD_WIDTH)),
        ],
        out_specs=pl.BlockSpec((CHUNK, GM_WIDTH + SSD_WIDTH), lambda b, c: (rows(b, c), 0)),
        out_shape=jax.ShapeDtypeStruct((t, GM_WIDTH + SSD_WIDTH), BF16),
        scratch_shapes=[
            pltpu.VMEM((HALO + CHUNK, CONV_DIM), F32),
            pltpu.VMEM((CHUNK, CONV_DIM), F32),
            pltpu.VMEM((SSD_STATE, SSD_WIDTH), F32),
            pltpu.VMEM((CHUNK, SSD_WIDTH), F32),
        ],
        compiler_params=_cparams(("parallel", "arbitrary")),
        name="mixer",
    )(proj, proj, proj, proj, dt_raw,
      p["lnw"], p["lnb"], p["ws"], p["bst"], p["cw"], p["cb"], p["dtb"], p["alog"], p["dsk"], p["nw"])


def _first_max(vals, axis_iota, n):
    m = jnp.max(vals, axis=0, keepdims=True)
    idx = jnp.min(jnp.where(vals == m, axis_iota, n), axis=0, keepdims=True)
    return m, idx


def _post_kernel(y_ref, wout_ref, x_ref, g1_ref, n2w_ref, s2_ref, sh2_ref, rwt_ref, rb_ref,
                 xo_ref, h2_ref, eidx_ref, gate_ref, rank_ref, cnt_ref, carry_scr):
    @pl.when(pl.program_id(0) == 0)
    def _():
        carry_scr[...] = jnp.zeros(carry_scr.shape, F32)

    tm = x_ref.shape[0]
    mix = jnp.dot(y_ref[...], wout_ref[...], preferred_element_type=F32)
    x = x_ref[...] + g1_ref[...] * mix
    xo_ref[...] = x
    ms = jnp.mean(x * x, axis=-1, keepdims=True)
    h = x * lax.rsqrt(ms + EPS) * n2w_ref[...] * (1.0 + s2_ref[...]) + sh2_ref[...]
    _store_slab_rows(h2_ref, 0, h)

    logits_t = lax.dot_general(rwt_ref[...], h.astype(BF16), _NT, preferred_element_type=F32)
    scores = jax.nn.sigmoid(logits_t)
    biased = scores + rb_ref[...]

    sub = lax.broadcasted_iota(I32, (EXPERTS_PER_GROUP, tm), 0)
    neg = jnp.float32(-jnp.inf)
    best = None
    for g in range(N_EXPERT_GROUPS):
        grp = biased[g * EXPERTS_PER_GROUP:(g + 1) * EXPERTS_PER_GROUP, :]
        m1, i1 = _first_max(grp, sub, EXPERTS_PER_GROUP)
        m2, i2 = _first_max(jnp.where(sub == i1, neg, grp), sub, EXPERTS_PER_GROUP)
        gs = m1 + m2
        if best is None:
            best, bi, l1, l2 = gs, jnp.zeros((1, tm), I32), i1, i2
        else:
            upd = gs > best
            best = jnp.where(upd, gs, best)
            bi = jnp.where(upd, g, bi)
            l1 = jnp.where(upd, i1, l1)
            l2 = jnp.where(upd, i2, l2)
    e0 = bi * EXPERTS_PER_GROUP + l1
    e1 = bi * EXPERTS_PER_GROUP + l2

    eio = lax.broadcasted_iota(I32, (N_EXPERTS, tm), 0)
    oh0 = eio == e0
    oh1 = eio == e1
    s0 = jnp.sum(jnp.where(oh0, scores, 0.0), axis=0, keepdims=True)
    s1 = jnp.sum(jnp.where(oh1, scores, 0.0), axis=0, keepdims=True)
    tot = s0 + s1
    eidx_ref[0:1, :] = e0
    eidx_ref[1:2, :] = e1
    gate_ref[0:1, :] = s0 / tot
    gate_ref[1:2, :] = s1 / tot

    ohs = oh0.astype(F32) + oh1.astype(F32)
    tr = lax.broadcasted_iota(I32, (tm, tm), 0)
    tc = lax.broadcasted_iota(I32, (tm, tm), 1)
    before = (tr < tc).astype(BF16)
    prefix = jnp.dot(ohs.astype(BF16), before, preferred_element_type=F32)
    base = carry_scr[:, 0:1] + prefix
    rank_ref[0:1, :] = jnp.sum(jnp.where(oh0, base, 0.0), axis=0, keepdims=True).astype(I32)
    rank_ref[1:2, :] = jnp.sum(jnp.where(oh1, base, 0.0), axis=0, keepdims=True).astype(I32)
    carry_scr[...] = carry_scr[...] + jnp.sum(ohs, axis=1, keepdims=True)
    cnt_ref[...] = carry_scr[...]


def _post(y_mix, w_out, xf, mod, norm2_w, rw_t, rb, seq):
    t, d = xf.shape
    dm = y_mix.shape[1]
    tm = min(512, seq)
    per_batch = seq // tm
    modspec = lambda k: pl.BlockSpec((None, None, 1, d), lambda i: (i // per_batch, k, 0, 0))
    tok = pl.BlockSpec((2, tm), lambda i: (0, i))
    return pl.pallas_call(
        _post_kernel,
        grid=(t // tm,),
        in_specs=[
            pl.BlockSpec((tm, dm), lambda i: (i, 0)),
            pl.BlockSpec((dm, d), lambda i: (0, 0)),
            pl.BlockSpec((tm, d), lambda i: (i, 0)),
            modspec(2),
            pl.BlockSpec((1, d), lambda i: (0, 0)),
            modspec(4),
            modspec(3),
            pl.BlockSpec((N_EXPERTS, d), lambda i: (0, 0)),
            pl.BlockSpec((N_EXPERTS, 1), lambda i: (0, 0)),
        ],
        out_specs=[
            pl.BlockSpec((tm, d), lambda i: (i, 0)),
            pl.BlockSpec((tm * SLAB, d // SLAB), lambda i: (i, 0)),
            tok, tok, tok,
            pl.BlockSpec((N_EXPERTS, LANES), lambda i: (0, 0)),
        ],
        out_shape=[
            jax.ShapeDtypeStruct((t, d), F32),
            jax.ShapeDtypeStruct((t * SLAB, d // SLAB), F32),
            jax.ShapeDtypeStruct((2, t), I32),
            jax.ShapeDtypeStruct((2, t), F32),
            jax.ShapeDtypeStruct((2, t), I32),
            jax.ShapeDtypeStruct((N_EXPERTS, LANES), F32),
        ],
        scratch_shapes=[pltpu.VMEM((N_EXPERTS, LANES), F32)],
        compiler_params=_cparams(("arbitrary",)),
        name="post",
    )(y_mix, w_out, xf, mod, norm2_w.reshape(1, d), mod, mod, rw_t, rb.reshape(N_EXPERTS, 1))


META_ROWS = 8
META_LANES = 256
ROW_BLK_E, ROW_CNT, ROW_PSTART, ROW_NUSED, ROW_NVALID = 0, 1, 2, 3, 4


def _col_to_row(colv):
    n = colv.shape[0]
    r = lax.broadcasted_iota(I32, (n, n), 0)
    c = lax.broadcasted_iota(I32, (n, n), 1)
    return jnp.sum(jnp.where(r == c, colv, 0.0), axis=0, keepdims=True)


def _meta_kernel(cnt_ref, eidx_ref, rank_ref, dest_ref, meta_ref):
    t = eidx_ref.shape[1]
    cnt = cnt_ref[...]
    nblk = jnp.floor((cnt + (MOE_BLOCK - 1)) * (1.0 / MOE_BLOCK))
    r = lax.broadcasted_iota(I32, (N_EXPERTS, N_EXPERTS), 0)
    c = lax.broadcasted_iota(I32, (N_EXPERTS, N_EXPERTS), 1)
    lower = (c < r).astype(BF16)
    pstart = jnp.dot(lower, nblk.astype(BF16), preferred_element_type=F32)
    pend = pstart + nblk

    chunk = min(1024, t)
    for j in range(t // chunk):
        sl = slice(j * chunk, (j + 1) * chunk)
        eio = lax.broadcasted_iota(I32, (N_EXPERTS, chunk), 0)
        for k in range(2):
            oh = eio == eidx_ref[k:k + 1, sl]
            ps = jnp.sum(jnp.where(oh, pstart[:, 0:1], 0.0), axis=0, keepdims=True)
            dest_ref[k:k + 1, sl] = (ps * MOE_BLOCK).astype(I32) + rank_ref[k:k + 1, sl]

    bl = lax.broadcasted_iota(I32, (N_EXPERTS, META_LANES), 1).astype(F32)
    raw = jnp.sum((pend[:, 0:1] <= bl).astype(F32), axis=0, keepdims=True)
    raw = jnp.minimum(raw, N_EXPERTS - 1.0)
    nused = pend[N_EXPERTS - 1:N_EXPERTS, 0:1]
    used = bl[0:1, :] < nused
    last_e = jnp.max(jnp.where(used, raw, 0.0), axis=1, keepdims=True)
    meta_ref[...] = jnp.zeros(meta_ref.shape, I32)
    meta_ref[ROW_BLK_E:ROW_BLK_E + 1, :] = jnp.where(used, raw, last_e).astype(I32)
    meta_ref[ROW_CNT:ROW_CNT + 1, 0:N_EXPERTS] = _col_to_row(cnt[:, 0:1]).astype(I32)
    meta_ref[ROW_PSTART:ROW_PSTART + 1, 0:N_EXPERTS] = (_col_to_row(pstart[:, 0:1]) * MOE_BLOCK).astype(I32)
    meta_ref[ROW_NUSED:ROW_NUSED + 1, :] = jnp.broadcast_to(nused, (1, META_LANES)).astype(I32)
    mine = lax.broadcasted_iota(I32, (N_EXPERTS, META_LANES), 0).astype(F32) == raw
    cnt_b = jnp.sum(jnp.where(mine, cnt[:, 0:1], 0.0), axis=0, keepdims=True)
    first_b = jnp.sum(jnp.where(mine, pstart[:, 0:1], 0.0), axis=0, keepdims=True)
    nvalid = jnp.clip(cnt_b - (bl[0:1, :] - first_b) * MOE_BLOCK, 0.0, float(MOE_BLOCK))
    meta_ref[ROW_NVALID:ROW_NVALID + 1, :] = jnp.where(used, nvalid, 0.0).astype(I32)


def _meta(cnt, eidx, rank):
    t = eidx.shape[1]
    full = lambda shape: pl.BlockSpec(shape, lambda: (0,) * len(shape))
    return pl.pallas_call(
        _meta_kernel,
        in_specs=[full((N_EXPERTS, LANES)), full((2, t)), full((2, t))],
        out_specs=[full((2, t)), full((META_ROWS, META_LANES))],
        out_shape=[jax.ShapeDtypeStruct((2, t), I32), jax.ShapeDtypeStruct((META_ROWS, META_LANES), I32)],
        name="meta",
    )(cnt, eidx, rank)


def _invert_kernel(dest0_ref, dest1_ref, code_ref):
    i = pl.program_id(0)
    tm = dest0_ref.shape[0]
    t = tm * pl.num_programs(0)

    @pl.when(i == 0)
    def _():
        def clear(p, carry):
            code_ref[p] = 0
            return carry
        lax.fori_loop(0, code_ref.shape[0], clear, 0, unroll=16)

    def put(tok, carry):
        code_ref[dest0_ref[tok]] = i * tm + tok
        code_ref[dest1_ref[tok]] = t + i * tm + tok
        return carry
    lax.fori_loop(0, tm, put, 0, unroll=8)


def _invert(dest, n_rows):
    t = dest.shape[1]
    tm = min(1024, t)
    slots = pl.BlockSpec((tm,), lambda i: (i,), memory_space=pltpu.SMEM)
    return pl.pallas_call(
        _invert_kernel,
        grid=(t // tm,),
        in_specs=[slots, slots],
        out_specs=pl.BlockSpec((n_rows,), lambda i: (0,), memory_space=pltpu.SMEM),
        out_shape=jax.ShapeDtypeStruct((n_rows,), I32),
        compiler_params=_cparams(("arbitrary",)),
        name="invert",
    )(dest[0], dest[1])


def _expert_kernel(be_sm, nu_sm, cnt_sm, nv_sm, code_sm, h2_hbm, wg_hbm, wu_hbm, wd_hbm, yk_hbm, xbuf, obuf,
                   land_g, land_u, land_d, wg_scr, wu_scr, wd_scr, sems, gsem, ssem, st_sm, *, layer):
    b = pl.program_id(0)
    nu = nu_sm[0]
    par = b & 1
    t = h2_hbm.shape[0] // SLAB
    n_blocks = pl.num_programs(0)
    blk_rows = MOE_BLOCK * VMEM_PITCH

    def hbm_rows(ref, index, n=1):
        return ref.at[pl.ds(pl.multiple_of(index * SLAB, SUBLANES), n * SLAB)]

    def vmem_row(ref, slot, r):
        return ref.at[pl.ds(pl.multiple_of((slot * MOE_BLOCK + r) * VMEM_PITCH, SUBLANES), SLAB)]

    def vmem_span(ref, slot, n):
        return ref.at[pl.ds(pl.multiple_of(slot * blk_rows, SUBLANES), n * SLAB)]

    def gather_row(ahead, r, slot):
        code = code_sm[(b + ahead) * MOE_BLOCK + r]
        if t & (t - 1) == 0:
            tok = code & (t - 1)
        else:
            tok = jnp.where(code >= t, code - t, code)
        return pltpu.make_async_copy(hbm_rows(h2_hbm, tok), vmem_row(xbuf, slot, r), gsem.at[slot])

    def scatter_row(r, slot):
        return pltpu.make_async_copy(vmem_row(obuf, slot, r), hbm_rows(yk_hbm, code_sm[b * MOE_BLOCK + r]),
                                     ssem.at[slot])

    def gather_sized(slot):
        return lambda n: pltpu.make_async_copy(hbm_rows(h2_hbm, 0, n), vmem_span(xbuf, slot, n), gsem.at[slot])

    def scatter_sized(slot):
        return lambda n: pltpu.make_async_copy(vmem_span(obuf, slot, n), hbm_rows(yk_hbm, 0, n), ssem.at[slot])

    def start_rows(n, row_copy, queues):
        groups = n // SUBLANES

        def eight(g, c):
            for u in range(SUBLANES):
                row_copy(g * SUBLANES + u).start(priority=queues[u % len(queues)])
            return c
        lax.fori_loop(0, groups, eight, 0)
        lax.fori_loop(groups * SUBLANES, n, lambda r, c: (row_copy(r).start(priority=queues[0]), c)[1], 0)

    def wait_rows(n, row_copy, sized_copy):
        @pl.when(n == MOE_BLOCK)
        def _():
            sized_copy(MOE_BLOCK).wait()

        @pl.when(n != MOE_BLOCK)
        def _():
            groups = n // SUBLANES
            lax.fori_loop(0, groups, lambda g, c: (sized_copy(SUBLANES).wait(), c)[1], 0)
            lax.fori_loop(groups * SUBLANES, n, lambda r, c: (row_copy(r).wait(), c)[1], 0)

    def fetch(e, slot):
        return (pltpu.make_async_copy(wg_hbm.at[layer, e], land_g.at[slot], sems.at[slot, 0]),
                pltpu.make_async_copy(wu_hbm.at[layer, e], land_u.at[slot], sems.at[slot, 1]),
                pltpu.make_async_copy(wd_hbm.at[layer, e], land_d.at[slot], sems.at[slot, 2]))

    def next_used(e):
        return lax.while_loop(lambda n: (n < N_EXPERTS) & (cnt_sm[jnp.minimum(n, N_EXPERTS - 1)] == 0),
                              lambda n: n + 1, e + 1)

    def start_next(slot):
        nxt = next_used(st_sm[1])

        @pl.when(nxt < N_EXPERTS)
        def _():
            for cp in fetch(nxt, slot):
                cp.start(priority=WEIGHT_QUEUE)
        st_sm[1] = nxt

    @pl.when(b == 0)
    def _():
        st_sm[0] = 0
        st_sm[1] = -1
        start_next(0)
        start_next(1)
        xbuf[...] = jnp.zeros(xbuf.shape, xbuf.dtype)
        for ahead in range(GATHER_AHEAD):
            @pl.when(ahead < nu)
            def _():
                start_rows(nv_sm[ahead], lambda r: gather_row(ahead, r, ahead), GATHER_QUEUES)

    xslot = lax.rem(b, GATHER_AHEAD + 1)

    @pl.when(b < nu)
    def _():
        @pl.when(b + GATHER_AHEAD < nu)
        def _():
            start_rows(nv_sm[jnp.minimum(b + GATHER_AHEAD, n_blocks - 1)],
                       lambda r: gather_row(GATHER_AHEAD, r, lax.rem(b + GATHER_AHEAD, GATHER_AHEAD + 1)),
                       GATHER_QUEUES)

        prev = be_sm[jnp.maximum(b - 1, 0)]

        @pl.when((b == 0) | (be_sm[b] != prev))
        def _():
            slot = st_sm[0] & 1
            for cp in fetch(be_sm[b], slot):
                cp.wait()
            wg_scr[...] = land_g[slot].astype(BF16)
            wu_scr[...] = land_u[slot].astype(BF16)
            wd_scr[...] = land_d[slot].astype(BF16)
            st_sm[0] = st_sm[0] + 1
            start_next(slot)

        wait_rows(nv_sm[b], lambda r: gather_row(0, r, xslot), gather_sized(xslot))

        xb = jnp.concatenate(_load_slab_rows(xbuf, xslot * blk_rows, MOE_BLOCK, VMEM_PITCH),
                             axis=1).astype(BF16)
        gate = jnp.dot(xb, wg_scr[...], preferred_element_type=F32)
        up = jnp.dot(xb, wu_scr[...], preferred_element_type=F32)
        act = (_silu(gate) * up).astype(BF16)
        out = jnp.dot(act, wd_scr[...], preferred_element_type=F32)

        @pl.when(b >= 2)
        def _():
            wait_rows(nv_sm[jnp.maximum(b - 2, 0)], lambda r: scatter_row(r, par), scatter_sized(par))

        _store_slab_rows(obuf, par * blk_rows, out, VMEM_PITCH)
        start_rows(nv_sm[b], lambda r: scatter_row(r, par), SCATTER_QUEUES)

        @pl.when(b == nu - 1)
        def _():
            @pl.when(b >= 1)
            def _():
                wait_rows(nv_sm[jnp.maximum(b - 1, 0)], lambda r: scatter_row(r, 1 - par),
                          scatter_sized(1 - par))
            wait_rows(nv_sm[b], lambda r: scatter_row(r, par), scatter_sized(par))


def _experts(blk_e, nused, cnt_row, nvalid, code_sorted, h2, w_gate, w_up, w_down, layer):
    sw = h2.shape[1]
    t = h2.shape[0] // SLAB
    d = SLAB * sw
    de = w_gate.shape[3]
    n_blocks = code_sorted.shape[0] // MOE_BLOCK
    hbm = pl.BlockSpec(memory_space=pl.ANY)
    return pl.pallas_call(
        functools.partial(_expert_kernel, layer=layer),
        grid_spec=pltpu.PrefetchScalarGridSpec(
            num_scalar_prefetch=5,
            grid=(n_blocks,),
            in_specs=[hbm, hbm, hbm, hbm],
            out_specs=hbm,
            scratch_shapes=[
                pltpu.VMEM(((GATHER_AHEAD + 1) * MOE_BLOCK * VMEM_PITCH, sw), F32),
                pltpu.VMEM((2 * MOE_BLOCK * VMEM_PITCH, sw), F32),
                pltpu.VMEM((2, d, de), F32),
                pltpu.VMEM((2, d, de), F32),
                pltpu.VMEM((2, de, d), F32),
                pltpu.VMEM((d, de), BF16),
                pltpu.VMEM((d, de), BF16),
                pltpu.VMEM((de, d), BF16),
                pltpu.SemaphoreType.DMA((2, 3)),
                pltpu.SemaphoreType.DMA((GATHER_AHEAD + 1,)),
                pltpu.SemaphoreType.DMA((2,)),
                pltpu.SMEM((2,), I32),
            ],
        ),
        out_shape=jax.ShapeDtypeStruct((2 * t * SLAB, sw), F32),
        compiler_params=_cparams(("arbitrary",)),
        name="experts",
    )(blk_e, nused, cnt_row, nvalid, code_sorted, h2, w_gate, w_up, w_down)


def _combine_kernel(gate_ref, x_ref, g2_ref, fw_ref, y0_ref, y1_ref, out_ref, *, final):
    tm = x_ref.shape[0]
    r = lax.broadcasted_iota(I32, (tm, tm), 0)
    c = lax.broadcasted_iota(I32, (tm, tm), 1)
    eye = r == c
    w0 = jnp.sum(jnp.where(eye, gate_ref[0:1, :], 0.0), axis=1, keepdims=True)
    w1 = jnp.sum(jnp.where(eye, gate_ref[1:2, :], 0.0), axis=1, keepdims=True)
    for q in range(tm // COMBINE_ROWS):
        rs = slice(q * COMBINE_ROWS, (q + 1) * COMBINE_ROWS)
        base = q * COMBINE_ROWS * SLAB
        chunks = [c0 * w0[rs] + c1 * w1[rs] for c0, c1 in zip(_load_slab_rows(y0_ref, base, COMBINE_ROWS),
                                                              _load_slab_rows(y1_ref, base, COMBINE_ROWS))]
        if final:
            x = x_ref[rs, :] + g2_ref[...] * jnp.concatenate(chunks, axis=1)
            ms = jnp.mean(x * x, axis=-1, keepdims=True)
            out_ref[rs, :] = x * lax.rsqrt(ms + EPS) * fw_ref[...]
        else:
            sw = y0_ref.shape[1]
            for s, y in enumerate(chunks):
                sl = slice(s * sw, (s + 1) * sw)
                out_ref[rs, sl] = x_ref[rs, sl] + g2_ref[:, sl] * y


def _combine(gate, xf, mod, final_w, yk, seq, final):
    t, d = xf.shape
    tm = min(512, seq)
    per_batch = seq // tm
    return pl.pallas_call(
        functools.partial(_combine_kernel, final=final),
        grid=(t // tm,),
        in_specs=[
            pl.BlockSpec((2, tm), lambda i: (0, i)),
            pl.BlockSpec((tm, d), lambda i: (i, 0)),
            pl.BlockSpec((None, None, 1, d), lambda i: (i // per_batch, 5, 0, 0)),
            pl.BlockSpec((1, d), lambda i: (0, 0)),
            pl.BlockSpec((tm * SLAB, yk.shape[1]), lambda i: (i, 0)),
            pl.BlockSpec((tm * SLAB, yk.shape[1]), lambda i: (i + t // tm, 0)),
        ],
        out_specs=pl.BlockSpec((tm, d), lambda i: (i, 0)),
        out_shape=jax.ShapeDtypeStruct((t, d), F32),
        compiler_params=_cparams(("parallel",)),
        name="combine",
    )(gate, xf, mod, final_w.reshape(1, d), yk, yk)


def _pad_lanes(v, n=LANES):
    return jnp.pad(v, (0, n - v.shape[0])).reshape(1, n)


def kernel(x, c, ada_w, ada_b, norm1_w, w_in, gm_ln_w, gm_ln_b, gm_ws, gm_bs, conv_w, conv_b, dt_bias, a_log,
           d_skip, ssd_norm_w, w_out, norm2_w, router_w, router_b, exp_w_gate, exp_w_up, exp_w_down,
           final_norm_w):
    batch, seq, d = x.shape
    t = batch * seq
    depth = ada_w.shape[0]
    assert batch <= 8 and seq % CHUNK == 0 and w_in.shape[2] == MAIN_PROJ + SSD_HEADS
    n_rows = (-(-(t * 2) // MOE_BLOCK) + N_EXPERTS) * MOE_BLOCK
    assert n_rows // MOE_BLOCK <= META_LANES

    ada = _ada(jnp.pad(c, ((0, 8 - batch), (0, 0))), ada_w, ada_b)
    rw_t = router_w.T.astype(BF16)
    w_in_t = jnp.swapaxes(w_in, 1, 2)
    xf = x.reshape(t, d)
    for l in range(depth):
        mod = ada[l, :batch].reshape(batch, 6, 1, d)
        proj, dt_raw = _inproj(xf, norm1_w[l], mod, w_in_t, l, seq)
        mixer_params = dict(
            lnw=gm_ln_w[l].reshape(1, GM_WIDTH), lnb=gm_ln_b[l].reshape(1, GM_WIDTH),
            ws=gm_ws[l], bst=gm_bs[l].T,
            cw=conv_w[l], cb=conv_b[l].reshape(1, CONV_DIM),
            dtb=_pad_lanes(dt_bias[l]), alog=_pad_lanes(a_log[l]),
            dsk=jnp.repeat(d_skip[l], SSD_WIDTH // SSD_HEADS).reshape(1, SSD_WIDTH),
            nw=ssd_norm_w[l].reshape(1, SSD_WIDTH))
        y_mix = _mixer(proj, dt_raw, mixer_params, batch, seq)
        xf, h2, eidx, gate, rank, cnt = _post(y_mix, w_out[l].astype(BF16), xf, mod, norm2_w[l], rw_t,
                                              router_b, seq)
        dest, meta = _meta(cnt, eidx, rank)
        n_blocks = n_rows // MOE_BLOCK
        yk = _experts(meta[ROW_BLK_E, :n_blocks], meta[ROW_NUSED, :1], meta[ROW_CNT, :N_EXPERTS],
                      meta[ROW_NVALID, :n_blocks], _invert(dest, n_rows), h2, exp_w_gate, exp_w_up,
                      exp_w_down, l)
        xf = _combine(gate, xf, mod, final_norm_w, yk, seq, final=(l == depth - 1))
    return xf.reshape(batch, seq, d)
```

```python
import functools

import jax
import jax.numpy as jnp
from jax import lax
from jax.experimental import pallas as pl
from jax.experimental.pallas import tpu as pltpu

F32 = jnp.float32
BF16 = jnp.bfloat16
I32 = jnp.int32

EPS = 1e-6
LANES = 128
CHUNK = 128
GM_HEADS = 8
GM_WIDTH = 1024
SSD_WIDTH = 1024
SSD_HEADS = 16
SSD_GROUPS = 2
SSD_STATE = 128
SSD_CONV = 4
CONV_DIM = SSD_WIDTH + 2 * SSD_GROUPS * SSD_STATE
MAIN_PROJ = 2 * GM_WIDTH + SSD_WIDTH + CONV_DIM
N_EXPERTS = 64
EXPERTS_PER_GROUP = 8
N_EXPERT_GROUPS = 8
MOE_BLOCK = 128
HALO = 8
SLAB = 16
VMEM_PITCH = 24
SUBLANES = 8
WEIGHT_QUEUE = 1
GATHER_QUEUES = (0,)
SCATTER_QUEUES = (0, 1)
GATHER_AHEAD = 1
COMBINE_ROWS = 128
VMEM_LIMIT = 56 * 1024 * 1024


def _cparams(sem, vmem=VMEM_LIMIT):
    return pltpu.CompilerParams(dimension_semantics=sem, vmem_limit_bytes=vmem)


def _silu(x):
    half = 0.5 * x
    return half * (1.0 + jnp.tanh(half))


def _gelu(x):
    return 0.5 * x * (1.0 + lax.erf(x * 0.7071067811865476))


def _softplus(x):
    return jnp.maximum(x, 0.0) + jnp.log1p(jnp.exp(-jnp.abs(x)))


def _store_slab_rows(ref, base, x, pitch=SLAB):
    n = x.shape[0]
    w = ref.shape[1]
    for s in range(SLAB):
        ref[pl.ds(base + s, n, stride=pitch), :] = x[:, s * w:(s + 1) * w]


def _load_slab_rows(ref, base, n, pitch=SLAB):
    return [ref[pl.ds(base + s, n, stride=pitch), :] for s in range(SLAB)]


def _ada_kernel(c_ref, w_ref, b_ref, o_ref):
    sc = _silu(c_ref[...])
    o_ref[0] = jnp.dot(sc.astype(BF16), w_ref[0].astype(BF16), preferred_element_type=F32) + b_ref[0]


def _ada(c_pad, ada_w, ada_b):
    n_layers, d, n = ada_w.shape
    tn = 1024
    return pl.pallas_call(
        _ada_kernel,
        grid=(n_layers, n // tn),
        in_specs=[
            pl.BlockSpec((8, d), lambda l, j: (0, 0)),
            pl.BlockSpec((1, d, tn), lambda l, j: (l, 0, j)),
            pl.BlockSpec((1, 1, tn), lambda l, j: (l, 0, j)),
        ],
        out_specs=pl.BlockSpec((1, 8, tn), lambda l, j: (l, 0, j)),
        out_shape=jax.ShapeDtypeStruct((n_layers, 8, n), F32),
        compiler_params=_cparams(("parallel", "parallel")),
        name="ada",
    )(c_pad, ada_w, ada_b.reshape(n_layers, 1, n))


_NT = (((1,), (1,)), ((), ()))


def _inproj_kernel(x_ref, nw_ref, s_ref, sh_ref, wt_ref, wdt_ref, o_ref, dt_ref, h_scr, wdt_scr):
    @pl.when(pl.program_id(1) == 0)
    def _():
        x = x_ref[...]
        ms = jnp.mean(x * x, axis=-1, keepdims=True)
        y = x * lax.rsqrt(ms + EPS) * nw_ref[...]
        h = (y * (1.0 + s_ref[...]) + sh_ref[...]).astype(BF16)
        h_scr[...] = h
        wdt_scr[...] = jnp.zeros(wdt_scr.shape, wdt_scr.dtype)
        wdt_scr[0:SSD_HEADS, :] = wdt_ref[...].astype(BF16)
        dt_ref[...] = lax.dot_general(h, wdt_scr[...], _NT, preferred_element_type=F32)

    o_ref[...] = lax.dot_general(h_scr[...], wt_ref[...].astype(BF16), _NT, preferred_element_type=F32)


def _inproj(xf, norm_w, mod, w_in_t, layer, seq):
    t, d = xf.shape
    tm = min(1024, seq)
    tn = 1152
    per_batch = seq // tm
    return pl.pallas_call(
        _inproj_kernel,
        grid=(t // tm, MAIN_PROJ // tn),
        in_specs=[
            pl.BlockSpec((tm, d), lambda i, j: (i, 0)),
            pl.BlockSpec((1, d), lambda i, j: (0, 0)),
            pl.BlockSpec((None, None, 1, d), lambda i, j: (i // per_batch, 1, 0, 0)),
            pl.BlockSpec((None, None, 1, d), lambda i, j: (i // per_batch, 0, 0, 0)),
            pl.BlockSpec((None, tn, d), lambda i, j: (layer, j, 0)),
            pl.BlockSpec((None, SSD_HEADS, d), lambda i, j: (layer, MAIN_PROJ // SSD_HEADS, 0)),
        ],
        out_specs=[
            pl.BlockSpec((tm, tn), lambda i, j: (i, j)),
            pl.BlockSpec((tm, LANES), lambda i, j: (i, 0)),
        ],
        out_shape=[
            jax.ShapeDtypeStruct((t, MAIN_PROJ), F32),
            jax.ShapeDtypeStruct((t, LANES), F32),
        ],
        scratch_shapes=[
            pltpu.VMEM((tm, d), BF16),
            pltpu.VMEM((LANES, d), BF16),
        ],
        compiler_params=_cparams(("parallel", "arbitrary")),
        name="inproj",
    )(xf, norm_w.reshape(1, d), mod, mod, w_in_t, w_in_t)


def _mixer_kernel(u_ref, v_ref, z_ref, xbc_ref, dt_ref,
                  lnw_ref, lnb_ref, ws_ref, bst_ref, cw_ref, cb_ref, dtb_ref, alog_ref,
                  dsk_ref, nw_ref, y_ref, buf_scr, xa_scr, state_scr, ys_scr):
    @pl.when(pl.program_id(1) == 0)
    def _():
        buf_scr[0:HALO, :] = jnp.zeros((HALO, CONV_DIM), F32)
        state_scr[...] = jnp.zeros(state_scr.shape, F32)

    row = lax.broadcasted_iota(I32, (CHUNK, CHUNK), 0)
    col = lax.broadcasted_iota(I32, (CHUNK, CHUNK), 1)
    tril = row >= col
    lane_lo = col < (LANES // 2)
    lane_lo_row = lane_lo[0:1, :]

    for h in range(GM_HEADS):
        sl = slice(h * LANES, (h + 1) * LANES)
        gu = _gelu(u_ref[:, sl])
        gv = _gelu(v_ref[:, sl])
        mu = jnp.mean(gv, axis=-1, keepdims=True)
        dv = gv - mu
        var = jnp.mean(dv * dv, axis=-1, keepdims=True)
        vn = dv * lax.rsqrt(var + EPS) * lnw_ref[:, sl] + lnb_ref[:, sl]
        w = jnp.where(tril, ws_ref[h], 0.0).astype(BF16)
        s = jnp.dot(w, vn.astype(BF16), preferred_element_type=F32) + bst_ref[:, h:h + 1]
        y_ref[:, sl] = (gu * s).astype(y_ref.dtype)

    buf_scr[HALO:HALO + CHUNK, :] = xbc_ref[...]
    for cb in range(CONV_DIM // 256):
        cs_ = slice(cb * 256, (cb + 1) * 256)
        first = HALO - (SSD_CONV - 1)
        acc = cb_ref[:, cs_] + cw_ref[0:1, cs_] * buf_scr[first:first + CHUNK, cs_]
        for k in range(1, SSD_CONV):
            acc = acc + cw_ref[k:k + 1, cs_] * buf_scr[first + k:first + k + CHUNK, cs_]
        xa_scr[:, cs_] = _silu(acc)
    buf_scr[0:HALO, :] = buf_scr[CHUNK:CHUNK + HALO, :]

    dt = _softplus(dt_ref[...] + dtb_ref[...])
    a = -jnp.exp(alog_ref[...])
    ad = dt * a
    cs = jnp.dot(tril.astype(F32), ad, precision=lax.Precision.HIGHEST, preferred_element_type=F32)
    cs_t = cs.T
    last = cs[CHUNK - 1:CHUNK, :]
    ds = jnp.exp(last - cs)
    ecs = jnp.exp(cs)
    cd = jnp.exp(last)

    def pair_expand(m, p):
        return jnp.where(lane_lo[0:m.shape[0], :], m[:, 2 * p:2 * p + 1], m[:, 2 * p + 1:2 * p + 2])

    pairs_per_group = SSD_HEADS // SSD_GROUPS // 2
    gw = SSD_WIDTH // SSD_GROUPS
    for g in range(SSD_GROUPS):
        bm_g = xa_scr[:, SSD_WIDTH + g * SSD_STATE:SSD_WIDTH + (g + 1) * SSD_STATE]
        cm_g = xa_scr[:, SSD_WIDTH + (SSD_GROUPS + g) * SSD_STATE:SSD_WIDTH + (SSD_GROUPS + g + 1) * SSD_STATE]
        cmb = cm_g.astype(BF16)
        bmb = bm_g.astype(BF16)
        cbm = lax.dot_general(cmb, bmb, (((1,), (1,)), ((), ())), preferred_element_type=F32)
        bm_t = bm_g.T.astype(BF16)
        st_prev = state_scr[:, g * gw:(g + 1) * gw]
        y_off = jnp.dot(cmb, st_prev.astype(BF16), preferred_element_type=F32)
        xdds = []
        cds = []
        for q in range(pairs_per_group):
            p = g * pairs_per_group + q
            sl = slice(p * LANES, (p + 1) * LANES)
            xs_p = xa_scr[:, sl]
            xd = xs_p * pair_expand(dt, p)
            xdb = xd.astype(BF16)
            ys = []
            for hh in (2 * p, 2 * p + 1):
                diff = cs[:, hh:hh + 1] - cs_t[hh:hh + 1, :]
                lm = jnp.where(tril, jnp.exp(jnp.where(tril, diff, 0.0)), 0.0)
                wmat = (cbm * lm).astype(BF16)
                ys.append(jnp.dot(wmat, xdb, preferred_element_type=F32))
            y_diag = jnp.where(lane_lo, ys[0], ys[1])
            y = y_diag + y_off[:, q * LANES:(q + 1) * LANES] * pair_expand(ecs, p)
            ys_scr[:, sl] = y + dsk_ref[:, sl] * xs_p
            xdds.append((xd * pair_expand(ds, p)).astype(BF16))
            cds.append(jnp.where(lane_lo_row, cd[:, 2 * p:2 * p + 1], cd[:, 2 * p + 1:2 * p + 2]))
        st_new = jnp.dot(bm_t, jnp.concatenate(xdds, axis=1), preferred_element_type=F32)
        state_scr[:, g * gw:(g + 1) * gw] = st_prev * jnp.concatenate(cds, axis=1) + st_new

    for g in range(SSD_GROUPS):
        sl = slice(g * gw, (g + 1) * gw)
        gg = ys_scr[:, sl] * _silu(z_ref[:, sl])
        ms = jnp.mean(gg * gg, axis=-1, keepdims=True)
        y_ref[:, GM_WIDTH + g * gw:GM_WIDTH + (g + 1) * gw] = (
            gg * lax.rsqrt(ms + EPS) * nw_ref[:, sl]).astype(y_ref.dtype)


def _mixer(proj, dt_raw, p, batch, seq):
    t = proj.shape[0]
    nc = seq // CHUNK
    rows = lambda b, c: b * nc + c
    full = lambda shape: pl.BlockSpec(shape, lambda b, c: (0,) * len(shape))
    return pl.pallas_call(
        _mixer_kernel,
        grid=(batch, nc),
        in_specs=[
            pl.BlockSpec((CHUNK, GM_WIDTH), lambda b, c: (rows(b, c), 0)),
            pl.BlockSpec((CHUNK, GM_WIDTH), lambda b, c: (rows(b, c), 1)),
            pl.BlockSpec((CHUNK, SSD_WIDTH), lambda b, c: (rows(b, c), 2)),
            pl.BlockSpec((CHUNK, CONV_DIM), lambda b, c: (rows(b, c), 2)),
            pl.BlockSpec((CHUNK, LANES), lambda b, c: (rows(b, c), 0)),
            full((1, GM_WIDTH)), full((1, GM_WIDTH)),
            full((GM_HEADS, CHUNK, CHUNK)), full((CHUNK, GM_HEADS)),
            full((SSD_CONV, CONV_DIM)), full((1, CONV_DIM)),
            full((1, LANES)), full((1, LANES)),
            full((1, SSD_WIDTH)), full((1, SSD_WIDTH)),
        ],
        out_specs=pl.BlockSpec((CHUNK, GM_WIDTH + SSD_WIDTH), lambda b, c: (rows(b, c), 0)),
        out_shape=jax.ShapeDtypeStruct((t, GM_WIDTH + SSD_WIDTH), BF16),
        scratch_shapes=[
            pltpu.VMEM((HALO + CHUNK, CONV_DIM), F32),
            pltpu.VMEM((CHUNK, CONV_DIM), F32),
            pltpu.VMEM((SSD_STATE, SSD_WIDTH), F32),
            pltpu.VMEM((CHUNK, SSD_WIDTH), F32),
        ],
        compiler_params=_cparams(("parallel", "arbitrary")),
        name="mixer",
    )(proj, proj, proj, proj, dt_raw,
      p["lnw"], p["lnb"], p["ws"], p["bst"], p["cw"], p["cb"], p["dtb"], p["alog"], p["dsk"], p["nw"])


def _first_max(vals, axis_iota, n):
    m = jnp.max(vals, axis=0, keepdims=True)
    idx = jnp.min(jnp.where(vals == m, axis_iota, n), axis=0, keepdims=True)
    return m, idx


def _post_kernel(y_ref, wout_ref, x_ref, g1_ref, n2w_ref, s2_ref, sh2_ref, rwt_ref, rb_ref,
                 xo_ref, h2_ref, eidx_ref, gate_ref, rank_ref, cnt_ref, carry_scr):
    @pl.when(pl.program_id(0) == 0)
    def _():
        carry_scr[...] = jnp.zeros(carry_scr.shape, F32)

    tm = x_ref.shape[0]
    mix = jnp.dot(y_ref[...], wout_ref[...], preferred_element_type=F32)
    x = x_ref[...] + g1_ref[...] * mix
    xo_ref[...] = x
    ms = jnp.mean(x * x, axis=-1, keepdims=True)
    h = x * lax.rsqrt(ms + EPS) * n2w_ref[...] * (1.0 + s2_ref[...]) + sh2_ref[...]
    _store_slab_rows(h2_ref, 0, h)

    logits_t = lax.dot_general(rwt_ref[...], h.astype(BF16), _NT, preferred_element_type=F32)
    scores = jax.nn.sigmoid(logits_t)
    biased = scores + rb_ref[...]

    sub = lax.broadcasted_iota(I32, (EXPERTS_PER_GROUP, tm), 0)
    neg = jnp.float32(-jnp.inf)
    best = None
    for g in range(N_EXPERT_GROUPS):
        grp = biased[g * EXPERTS_PER_GROUP:(g + 1) * EXPERTS_PER_GROUP, :]
        m1, i1 = _first_max(grp, sub, EXPERTS_PER_GROUP)
        m2, i2 = _first_max(jnp.where(sub == i1, neg, grp), sub, EXPERTS_PER_GROUP)
        gs = m1 + m2
        if best is None:
            best, bi, l1, l2 = gs, jnp.zeros((1, tm), I32), i1, i2
        else:
            upd = gs > best
            best = jnp.where(upd, gs, best)
            bi = jnp.where(upd, g, bi)
            l1 = jnp.where(upd, i1, l1)
            l2 = jnp.where(upd, i2, l2)
    e0 = bi * EXPERTS_PER_GROUP + l1
    e1 = bi * EXPERTS_PER_GROUP + l2

    eio = lax.broadcasted_iota(I32, (N_EXPERTS, tm), 0)
    oh0 = eio == e0
    oh1 = eio == e1
    s0 = jnp.sum(jnp.where(oh0, scores, 0.0), axis=0, keepdims=True)
    s1 = jnp.sum(jnp.where(oh1, scores, 0.0), axis=0, keepdims=True)
    tot = s0 + s1
    eidx_ref[0:1, :] = e0
    eidx_ref[1:2, :] = e1
    gate_ref[0:1, :] = s0 / tot
    gate_ref[1:2, :] = s1 / tot

    ohs = oh0.astype(F32) + oh1.astype(F32)
    tr = lax.broadcasted_iota(I32, (tm, tm), 0)
    tc = lax.broadcasted_iota(I32, (tm, tm), 1)
    before = (tr < tc).astype(BF16)
    prefix = jnp.dot(ohs.astype(BF16), before, preferred_element_type=F32)
    base = carry_scr[:, 0:1] + prefix
    rank_ref[0:1, :] = jnp.sum(jnp.where(oh0, base, 0.0), axis=0, keepdims=True).astype(I32)
    rank_ref[1:2, :] = jnp.sum(jnp.where(oh1, base, 0.0), axis=0, keepdims=True).astype(I32)
    carry_scr[...] = carry_scr[...] + jnp.sum(ohs, axis=1, keepdims=True)
    cnt_ref[...] = carry_scr[...]


def _post(y_mix, w_out, xf, mod, norm2_w, rw_t, rb, seq):
    t, d = xf.shape
    dm = y_mix.shape[1]
    tm = min(512, seq)
    per_batch = seq // tm
    modspec = lambda k: pl.BlockSpec((None, None, 1, d), lambda i: (i // per_batch, k, 0, 0))
    tok = pl.BlockSpec((2, tm), lambda i: (0, i))
    return pl.pallas_call(
        _post_kernel,
        grid=(t // tm,),
        in_specs=[
            pl.BlockSpec((tm, dm), lambda i: (i, 0)),
            pl.BlockSpec((dm, d), lambda i: (0, 0)),
            pl.BlockSpec((tm, d), lambda i: (i, 0)),
            modspec(2),
            pl.BlockSpec((1, d), lambda i: (0, 0)),
            modspec(4),
            modspec(3),
            pl.BlockSpec((N_EXPERTS, d), lambda i: (0, 0)),
            pl.BlockSpec((N_EXPERTS, 1), lambda i: (0, 0)),
        ],
        out_specs=[
            pl.BlockSpec((tm, d), lambda i: (i, 0)),
            pl.BlockSpec((tm * SLAB, d // SLAB), lambda i: (i, 0)),
            tok, tok, tok,
            pl.BlockSpec((N_EXPERTS, LANES), lambda i: (0, 0)),
        ],
        out_shape=[
            jax.ShapeDtypeStruct((t, d), F32),
            jax.ShapeDtypeStruct((t * SLAB, d // SLAB), F32),
            jax.ShapeDtypeStruct((2, t), I32),
            jax.ShapeDtypeStruct((2, t), F32),
            jax.ShapeDtypeStruct((2, t), I32),
            jax.ShapeDtypeStruct((N_EXPERTS, LANES), F32),
        ],
        scratch_shapes=[pltpu.VMEM((N_EXPERTS, LANES), F32)],
        compiler_params=_cparams(("arbitrary",)),
        name="post",
    )(y_mix, w_out, xf, mod, norm2_w.reshape(1, d), mod, mod, rw_t, rb.reshape(N_EXPERTS, 1))


META_ROWS = 8
META_LANES = 256
ROW_BLK_E, ROW_CNT, ROW_NUSED, ROW_NVALID = 0, 1, 2, 3


def _col_to_row(colv):
    n = colv.shape[0]
    r = lax.broadcasted_iota(I32, (n, n), 0)
    c = lax.broadcasted_iota(I32, (n, n), 1)
    return jnp.sum(jnp.where(r == c, colv, 0.0), axis=0, keepdims=True)


def _meta_kernel(cnt_ref, eidx_ref, rank_ref, dest_ref, meta_ref):
    t = eidx_ref.shape[1]
    cnt = cnt_ref[...]
    nblk = jnp.floor((cnt + (MOE_BLOCK - 1)) * (1.0 / MOE_BLOCK))
    r = lax.broadcasted_iota(I32, (N_EXPERTS, N_EXPERTS), 0)
    c = lax.broadcasted_iota(I32, (N_EXPERTS, N_EXPERTS), 1)
    lower = (c < r).astype(BF16)
    pstart = jnp.dot(lower, nblk.astype(BF16), preferred_element_type=F32)
    pend = pstart + nblk

    chunk = min(1024, t)
    for j in range(t // chunk):
        sl = slice(j * chunk, (j + 1) * chunk)
        eio = lax.broadcasted_iota(I32, (N_EXPERTS, chunk), 0)
        for k in range(2):
            oh = eio == eidx_ref[k:k + 1, sl]
            ps = jnp.sum(jnp.where(oh, pstart[:, 0:1], 0.0), axis=0, keepdims=True)
            dest_ref[k:k + 1, sl] = (ps * MOE_BLOCK).astype(I32) + rank_ref[k:k + 1, sl]

    bl = lax.broadcasted_iota(I32, (N_EXPERTS, META_LANES), 1).astype(F32)
    raw = jnp.sum((pend[:, 0:1] <= bl).astype(F32), axis=0, keepdims=True)
    raw = jnp.minimum(raw, N_EXPERTS - 1.0)
    nused = pend[N_EXPERTS - 1:N_EXPERTS, 0:1]
    used = bl[0:1, :] < nused
    last_e = jnp.max(jnp.where(used, raw, 0.0), axis=1, keepdims=True)
    meta_ref[...] = jnp.zeros(meta_ref.shape, I32)
    meta_ref[ROW_BLK_E:ROW_BLK_E + 1, :] = jnp.where(used, raw, last_e).astype(I32)
    meta_ref[ROW_CNT:ROW_CNT + 1, 0:N_EXPERTS] = _col_to_row(cnt[:, 0:1]).astype(I32)
    meta_ref[ROW_NUSED:ROW_NUSED + 1, :] = jnp.broadcast_to(nused, (1, META_LANES)).astype(I32)
    mine = lax.broadcasted_iota(I32, (N_EXPERTS, META_LANES), 0).astype(F32) == raw
    cnt_b = jnp.sum(jnp.where(mine, cnt[:, 0:1], 0.0), axis=0, keepdims=True)
    first_b = jnp.sum(jnp.where(mine, pstart[:, 0:1], 0.0), axis=0, keepdims=True)
    nvalid = jnp.clip(cnt_b - (bl[0:1, :] - first_b) * MOE_BLOCK, 0.0, float(MOE_BLOCK))
    meta_ref[ROW_NVALID:ROW_NVALID + 1, :] = jnp.where(used, nvalid, 0.0).astype(I32)


def _meta(cnt, eidx, rank):
    t = eidx.shape[1]
    full = lambda shape: pl.BlockSpec(shape, lambda: (0,) * len(shape))
    return pl.pallas_call(
        _meta_kernel,
        in_specs=[full((N_EXPERTS, LANES)), full((2, t)), full((2, t))],
        out_specs=[full((2, t)), full((META_ROWS, META_LANES))],
        out_shape=[jax.ShapeDtypeStruct((2, t), I32), jax.ShapeDtypeStruct((META_ROWS, META_LANES), I32)],
        name="meta",
    )(cnt, eidx, rank)


def _invert_kernel(dest0_ref, dest1_ref, code_ref):
    i = pl.program_id(0)
    tm = dest0_ref.shape[0]
    t = tm * pl.num_programs(0)

    @pl.when(i == 0)
    def _():
        def clear(p, carry):
            code_ref[p] = 0
            return carry
        lax.fori_loop(0, code_ref.shape[0], clear, 0, unroll=16)

    def put(tok, carry):
        code_ref[dest0_ref[tok]] = i * tm + tok
        code_ref[dest1_ref[tok]] = t + i * tm + tok
        return carry
    lax.fori_loop(0, tm, put, 0, unroll=8)


def _invert(dest, n_rows):
    t = dest.shape[1]
    tm = min(1024, t)
    slots = pl.BlockSpec((tm,), lambda i: (i,), memory_space=pltpu.SMEM)
    return pl.pallas_call(
        _invert_kernel,
        grid=(t // tm,),
        in_specs=[slots, slots],
        out_specs=pl.BlockSpec((n_rows,), lambda i: (0,), memory_space=pltpu.SMEM),
        out_shape=jax.ShapeDtypeStruct((n_rows,), I32),
        compiler_params=_cparams(("arbitrary",)),
        name="invert",
    )(dest[0], dest[1])


def _expert_kernel(be_sm, nu_sm, cnt_sm, nv_sm, code_sm, h2_hbm, wg_hbm, wu_hbm, wd_hbm, yk_hbm, xbuf, obuf,
                   land_g, land_u, land_d, wg_scr, wu_scr, wd_scr, sems, gsem, ssem, st_sm, *, layer):
    b = pl.program_id(0)
    nu = nu_sm[0]
    par = b & 1
    t = h2_hbm.shape[0] // SLAB
    n_blocks = pl.num_programs(0)
    blk_rows = MOE_BLOCK * VMEM_PITCH

    def hbm_rows(ref, index, n=1):
        return ref.at[pl.ds(pl.multiple_of(index * SLAB, SUBLANES), n * SLAB)]

    def vmem_row(ref, slot, r):
        return ref.at[pl.ds(pl.multiple_of((slot * MOE_BLOCK + r) * VMEM_PITCH, SUBLANES), SLAB)]

    def vmem_span(ref, slot, n):
        return ref.at[pl.ds(pl.multiple_of(slot * blk_rows, SUBLANES), n * SLAB)]

    def gather_row(ahead, r, slot):
        code = code_sm[(b + ahead) * MOE_BLOCK + r]
        if t & (t - 1) == 0:
            tok = code & (t - 1)
        else:
            tok = jnp.where(code >= t, code - t, code)
        return pltpu.make_async_copy(hbm_rows(h2_hbm, tok), vmem_row(xbuf, slot, r), gsem.at[slot])

    def scatter_row(r, slot):
        return pltpu.make_async_copy(vmem_row(obuf, slot, r), hbm_rows(yk_hbm, code_sm[b * MOE_BLOCK + r]),
                                     ssem.at[slot])

    def gather_sized(slot):
        return lambda n: pltpu.make_async_copy(hbm_rows(h2_hbm, 0, n), vmem_span(xbuf, slot, n), gsem.at[slot])

    def scatter_sized(slot):
        return lambda n: pltpu.make_async_copy(vmem_span(obuf, slot, n), hbm_rows(yk_hbm, 0, n), ssem.at[slot])

    def start_rows(n, row_copy, queues):
        groups = n // SUBLANES

        def eight(g, c):
            for u in range(SUBLANES):
                row_copy(g * SUBLANES + u).start(priority=queues[u % len(queues)])
            return c
        lax.fori_loop(0, groups, eight, 0)
        lax.fori_loop(groups * SUBLANES, n, lambda r, c: (row_copy(r).start(priority=queues[0]), c)[1], 0)

    def wait_rows(n, row_copy, sized_copy):
        @pl.when(n == MOE_BLOCK)
        def _():
            sized_copy(MOE_BLOCK).wait()

        @pl.when(n != MOE_BLOCK)
        def _():
            groups = n // SUBLANES
            lax.fori_loop(0, groups, lambda g, c: (sized_copy(SUBLANES).wait(), c)[1], 0)
            lax.fori_loop(groups * SUBLANES, n, lambda r, c: (row_copy(r).wait(), c)[1], 0)

    def fetch(e, slot):
        return (pltpu.make_async_copy(wg_hbm.at[layer, e], land_g.at[slot], sems.at[slot, 0]),
                pltpu.make_async_copy(wu_hbm.at[layer, e], land_u.at[slot], sems.at[slot, 1]),
                pltpu.make_async_copy(wd_hbm.at[layer, e], land_d.at[slot], sems.at[slot, 2]))

    def next_used(e):
        return lax.while_loop(lambda n: (n < N_EXPERTS) & (cnt_sm[jnp.minimum(n, N_EXPERTS - 1)] == 0),
                              lambda n: n + 1, e + 1)

    def start_next(slot):
        nxt = next_used(st_sm[1])

        @pl.when(nxt < N_EXPERTS)
        def _():
            for cp in fetch(nxt, slot):
                cp.start(priority=WEIGHT_QUEUE)
        st_sm[1] = nxt

    @pl.when(b == 0)
    def _():
        st_sm[0] = 0
        st_sm[1] = -1
        start_next(0)
        start_next(1)
        xbuf[...] = jnp.zeros(xbuf.shape, xbuf.dtype)
        for ahead in range(GATHER_AHEAD):
            @pl.when(ahead < nu)
            def _():
                start_rows(nv_sm[ahead], lambda r: gather_row(ahead, r, ahead), GATHER_QUEUES)

    xslot = lax.rem(b, GATHER_AHEAD + 1)

    @pl.when(b < nu)
    def _():
        @pl.when(b + GATHER_AHEAD < nu)
        def _():
            start_rows(nv_sm[jnp.minimum(b + GATHER_AHEAD, n_blocks - 1)],
                       lambda r: gather_row(GATHER_AHEAD, r, lax.rem(b + GATHER_AHEAD, GATHER_AHEAD + 1)),
                       GATHER_QUEUES)

        prev = be_sm[jnp.maximum(b - 1, 0)]

        @pl.when((b == 0) | (be_sm[b] != prev))
        def _():
            slot = st_sm[0] & 1
            for cp in fetch(be_sm[b], slot):
                cp.wait()
            wg_scr[...] = land_g[slot].astype(BF16)
            wu_scr[...] = land_u[slot].astype(BF16)
            wd_scr[...] = land_d[slot].astype(BF16)
            st_sm[0] = st_sm[0] + 1
            start_next(slot)

        wait_rows(nv_sm[b], lambda r: gather_row(0, r, xslot), gather_sized(xslot))

        xb = jnp.concatenate(_load_slab_rows(xbuf, xslot * blk_rows, MOE_BLOCK, VMEM_PITCH),
                             axis=1).astype(BF16)
        gate = jnp.dot(xb, wg_scr[...], preferred_element_type=F32)
        up = jnp.dot(xb, wu_scr[...], preferred_element_type=F32)
        act = (_silu(gate) * up).astype(BF16)
        out = jnp.dot(act, wd_scr[...], preferred_element_type=F32)

        @pl.when(b >= 2)
        def _():
            wait_rows(nv_sm[jnp.maximum(b - 2, 0)], lambda r: scatter_row(r, par), scatter_sized(par))

        _store_slab_rows(obuf, par * blk_rows, out, VMEM_PITCH)
        start_rows(nv_sm[b], lambda r: scatter_row(r, par), SCATTER_QUEUES)

        @pl.when(b == nu - 1)
        def _():
            @pl.when(b >= 1)
            def _():
                wait_rows(nv_sm[jnp.maximum(b - 1, 0)], lambda r: scatter_row(r, 1 - par),
                          scatter_sized(1 - par))
            wait_rows(nv_sm[b], lambda r: scatter_row(r, par), scatter_sized(par))


def _experts(blk_e, nused, cnt_row, nvalid, code_sorted, h2, w_gate, w_up, w_down, layer):
    sw = h2.shape[1]
    t = h2.shape[0] // SLAB
    d = SLAB * sw
    de = w_gate.shape[3]
    n_blocks = code_sorted.shape[0] // MOE_BLOCK
    hbm = pl.BlockSpec(memory_space=pl.ANY)
    return pl.pallas_call(
        functools.partial(_expert_kernel, layer=layer),
        grid_spec=pltpu.PrefetchScalarGridSpec(
            num_scalar_prefetch=5,
            grid=(n_blocks,),
            in_specs=[hbm, hbm, hbm, hbm],
            out_specs=hbm,
            scratch_shapes=[
                pltpu.VMEM(((GATHER_AHEAD + 1) * MOE_BLOCK * VMEM_PITCH, sw), F32),
                pltpu.VMEM((2 * MOE_BLOCK * VMEM_PITCH, sw), F32),
                pltpu.VMEM((2, d, de), F32),
                pltpu.VMEM((2, d, de), F32),
                pltpu.VMEM((2, de, d), F32),
                pltpu.VMEM((d, de), BF16),
                pltpu.VMEM((d, de), BF16),
                pltpu.VMEM((de, d), BF16),
                pltpu.SemaphoreType.DMA((2, 3)),
                pltpu.SemaphoreType.DMA((GATHER_AHEAD + 1,)),
                pltpu.SemaphoreType.DMA((2,)),
                pltpu.SMEM((2,), I32),
            ],
        ),
        out_shape=jax.ShapeDtypeStruct((2 * t * SLAB, sw), F32),
        compiler_params=_cparams(("arbitrary",)),
        name="experts",
    )(blk_e, nused, cnt_row, nvalid, code_sorted, h2, w_gate, w_up, w_down)


def _combine_kernel(gate_ref, x_ref, g2_ref, fw_ref, y0_ref, y1_ref, out_ref, *, final):
    tm = x_ref.shape[0]
    r = lax.broadcasted_iota(I32, (tm, tm), 0)
    c = lax.broadcasted_iota(I32, (tm, tm), 1)
    eye = r == c
    w0 = jnp.sum(jnp.where(eye, gate_ref[0:1, :], 0.0), axis=1, keepdims=True)
    w1 = jnp.sum(jnp.where(eye, gate_ref[1:2, :], 0.0), axis=1, keepdims=True)
    for q in range(tm // COMBINE_ROWS):
        rs = slice(q * COMBINE_ROWS, (q + 1) * COMBINE_ROWS)
        base = q * COMBINE_ROWS * SLAB
        chunks = [c0 * w0[rs] + c1 * w1[rs] for c0, c1 in zip(_load_slab_rows(y0_ref, base, COMBINE_ROWS),
                                                              _load_slab_rows(y1_ref, base, COMBINE_ROWS))]
        if final:
            x = x_ref[rs, :] + g2_ref[...] * jnp.concatenate(chunks, axis=1)
            ms = jnp.mean(x * x, axis=-1, keepdims=True)
            out_ref[rs, :] = x * lax.rsqrt(ms + EPS) * fw_ref[...]
        else:
            sw = y0_ref.shape[1]
            for s, y in enumerate(chunks):
                sl = slice(s * sw, (s + 1) * sw)
                out_ref[rs, sl] = x_ref[rs, sl] + g2_ref[:, sl] * y


def _combine(gate, xf, mod, final_w, yk, seq, final):
    t, d = xf.shape
    tm = min(512, seq)
    per_batch = seq // tm
    return pl.pallas_call(
        functools.partial(_combine_kernel, final=final),
        grid=(t // tm,),
        in_specs=[
            pl.BlockSpec((2, tm), lambda i: (0, i)),
            pl.BlockSpec((tm, d), lambda i: (i, 0)),
            pl.BlockSpec((None, None, 1, d), lambda i: (i // per_batch, 5, 0, 0)),
            pl.BlockSpec((1, d), lambda i: (0, 0)),
            pl.BlockSpec((tm * SLAB, yk.shape[1]), lambda i: (i, 0)),
            pl.BlockSpec((tm * SLAB, yk.shape[1]), lambda i: (i + t // tm, 0)),
        ],
        out_specs=pl.BlockSpec((tm, d), lambda i: (i, 0)),
        out_shape=jax.ShapeDtypeStruct((t, d), F32),
        compiler_params=_cparams(("parallel",)),
        name="combine",
    )(gate, xf, mod, final_w.reshape(1, d), yk, yk)


def _pad_lanes(v, n=LANES):
    return jnp.pad(v, (0, n - v.shape[0])).reshape(1, n)


def kernel(x, c, ada_w, ada_b, norm1_w, w_in, gm_ln_w, gm_ln_b, gm_ws, gm_bs, conv_w, conv_b, dt_bias, a_log,
           d_skip, ssd_norm_w, w_out, norm2_w, router_w, router_b, exp_w_gate, exp_w_up, exp_w_down,
           final_norm_w):
    batch, seq, d = x.shape
    t = batch * seq
    depth = ada_w.shape[0]
    assert batch <= 8 and seq % CHUNK == 0 and w_in.shape[2] == MAIN_PROJ + SSD_HEADS
    n_rows = (-(-(t * 2) // MOE_BLOCK) + N_EXPERTS) * MOE_BLOCK
    assert n_rows // MOE_BLOCK <= META_LANES

    ada = _ada(jnp.pad(c, ((0, 8 - batch), (0, 0))), ada_w, ada_b)
    rw_t = router_w.T.astype(BF16)
    w_in_t = jnp.swapaxes(w_in, 1, 2)
    xf = x.reshape(t, d)
    for l in range(depth):
        mod = ada[l, :batch].reshape(batch, 6, 1, d)
        proj, dt_raw = _inproj(xf, norm1_w[l], mod, w_in_t, l, seq)
        mixer_params = dict(
            lnw=gm_ln_w[l].reshape(1, GM_WIDTH), lnb=gm_ln_b[l].reshape(1, GM_WIDTH),
            ws=gm_ws[l], bst=gm_bs[l].T,
            cw=conv_w[l], cb=conv_b[l].reshape(1, CONV_DIM),
            dtb=_pad_lanes(dt_bias[l]), alog=_pad_lanes(a_log[l]),
            dsk=jnp.repeat(d_skip[l], SSD_WIDTH // SSD_HEADS).reshape(1, SSD_WIDTH),
            nw=ssd_norm_w[l].reshape(1, SSD_WIDTH))
        y_mix = _mixer(proj, dt_raw, mixer_params, batch, seq)
        xf, h2, eidx, gate, rank, cnt = _post(y_mix, w_out[l].astype(BF16), xf, mod, norm2_w[l], rw_t,
                                              router_b, seq)
        dest, meta = _meta(cnt, eidx, rank)
        n_blocks = n_rows // MOE_BLOCK
        yk = _experts(meta[ROW_BLK_E, :n_blocks], meta[ROW_NUSED, :1], meta[ROW_CNT, :N_EXPERTS],
                      meta[ROW_NVALID, :n_blocks], _invert(dest, n_rows), h2, exp_w_gate, exp_w_up,
                      exp_w_down, l)
        xf = _combine(gate, xf, mod, final_norm_w, yk, seq, final=(l == depth - 1))
    return xf.reshape(batch, seq, d)
```

```python
import functools

import jax
import jax.numpy as jnp
from jax import lax
from jax.experimental import pallas as pl
from jax.experimental.pallas import tpu as pltpu

F32 = jnp.float32
BF16 = jnp.bfloat16
I32 = jnp.int32

EPS = 1e-6
LANES = 128
CHUNK = 128
GM_HEADS = 8
GM_WIDTH = 1024
SSD_WIDTH = 1024
SSD_HEADS = 16
SSD_GROUPS = 2
SSD_STATE = 128
SSD_CONV = 4
CONV_DIM = SSD_WIDTH + 2 * SSD_GROUPS * SSD_STATE
MAIN_PROJ = 2 * GM_WIDTH + SSD_WIDTH + CONV_DIM
N_EXPERTS = 64
EXPERTS_PER_GROUP = 8
N_EXPERT_GROUPS = 8
MOE_BLOCK = 128
HALO = 8
SLAB = 16
VMEM_PITCH = 24
SUBLANES = 8
WEIGHT_QUEUE = 1
GATHER_QUEUES = (0,)
SCATTER_QUEUES = (0, 1)
GATHER_AHEAD = 1
COMBINE_ROWS = 128
VMEM_LIMIT = 56 * 1024 * 1024


def _cparams(sem, vmem=VMEM_LIMIT):
    return pltpu.CompilerParams(dimension_semantics=sem, vmem_limit_bytes=vmem)


def _silu(x):
    half = 0.5 * x
    return half * (1.0 + jnp.tanh(half))


def _gelu(x):
    return 0.5 * x * (1.0 + lax.erf(x * 0.7071067811865476))


def _softplus(x):
    return jnp.maximum(x, 0.0) + jnp.log1p(jnp.exp(-jnp.abs(x)))


def _store_slab_rows(ref, base, x, pitch=SLAB):
    n = x.shape[0]
    w = ref.shape[1]
    for s in range(SLAB):
        ref[pl.ds(base + s, n, stride=pitch), :] = x[:, s * w:(s + 1) * w]


def _load_slab_rows(ref, base, n, pitch=SLAB):
    return [ref[pl.ds(base + s, n, stride=pitch), :] for s in range(SLAB)]


def _ada_kernel(c_ref, w_ref, b_ref, o_ref):
    sc = _silu(c_ref[...])
    o_ref[0] = jnp.dot(sc.astype(BF16), w_ref[0].astype(BF16), preferred_element_type=F32) + b_ref[0]


def _ada(c_pad, ada_w, ada_b):
    n_layers, d, n = ada_w.shape
    tn = 1024
    return pl.pallas_call(
        _ada_kernel,
        grid=(n_layers, n // tn),
        in_specs=[
            pl.BlockSpec((8, d), lambda l, j: (0, 0)),
            pl.BlockSpec((1, d, tn), lambda l, j: (l, 0, j)),
            pl.BlockSpec((1, 1, tn), lambda l, j: (l, 0, j)),
        ],
        out_specs=pl.BlockSpec((1, 8, tn), lambda l, j: (l, 0, j)),
        out_shape=jax.ShapeDtypeStruct((n_layers, 8, n), F32),
        compiler_params=_cparams(("parallel", "parallel")),
        name="ada",
    )(c_pad, ada_w, ada_b.reshape(n_layers, 1, n))


_NT = (((1,), (1,)), ((), ()))


def _inproj_kernel(x_ref, nw_ref, s_ref, sh_ref, wt_ref, wdt_ref, o_ref, dt_ref, h_scr, wdt_scr):
    @pl.when(pl.program_id(1) == 0)
    def _():
        x = x_ref[...]
        ms = jnp.mean(x * x, axis=-1, keepdims=True)
        y = x * lax.rsqrt(ms + EPS) * nw_ref[...]
        h = (y * (1.0 + s_ref[...]) + sh_ref[...]).astype(BF16)
        h_scr[...] = h
        wdt_scr[...] = jnp.zeros(wdt_scr.shape, wdt_scr.dtype)
        wdt_scr[0:SSD_HEADS, :] = wdt_ref[...].astype(BF16)
        dt_ref[...] = lax.dot_general(h, wdt_scr[...], _NT, preferred_element_type=F32)

    o_ref[...] = lax.dot_general(h_scr[...], wt_ref[...].astype(BF16), _NT, preferred_element_type=F32)


def _inproj(xf, norm_w, mod, w_in_t, layer, seq):
    t, d = xf.shape
    tm = min(1024, seq)
    tn = 1152
    per_batch = seq // tm
    return pl.pallas_call(
        _inproj_kernel,
        grid=(t // tm, MAIN_PROJ // tn),
        in_specs=[
            pl.BlockSpec((tm, d), lambda i, j: (i, 0)),
            pl.BlockSpec((1, d), lambda i, j: (0, 0)),
            pl.BlockSpec((None, None, 1, d), lambda i, j: (i // per_batch, 1, 0, 0)),
            pl.BlockSpec((None, None, 1, d), lambda i, j: (i // per_batch, 0, 0, 0)),
            pl.BlockSpec((None, tn, d), lambda i, j: (layer, j, 0)),
            pl.BlockSpec((None, SSD_HEADS, d), lambda i, j: (layer, MAIN_PROJ // SSD_HEADS, 0)),
        ],
        out_specs=[
            pl.BlockSpec((tm, tn), lambda i, j: (i, j)),
            pl.BlockSpec((tm, LANES), lambda i, j: (i, 0)),
        ],
        out_shape=[
            jax.ShapeDtypeStruct((t, MAIN_PROJ), F32),
            jax.ShapeDtypeStruct((t, LANES), F32),
        ],
        scratch_shapes=[
            pltpu.VMEM((tm, d), BF16),
            pltpu.VMEM((LANES, d), BF16),
        ],
        compiler_params=_cparams(("parallel", "arbitrary")),
        name="inproj",
    )(xf, norm_w.reshape(1, d), mod, mod, w_in_t, w_in_t)


def _mixer_kernel(u_ref, v_ref, z_ref, xbc_ref, dt_ref,
                  lnw_ref, lnb_ref, ws_ref, bst_ref, cw_ref, cb_ref, dtb_ref, alog_ref,
                  dsk_ref, nw_ref, y_ref, buf_scr, xa_scr, state_scr, ys_scr):
    @pl.when(pl.program_id(1) == 0)
    def _():
        buf_scr[0:HALO, :] = jnp.zeros((HALO, CONV_DIM), F32)
        state_scr[...] = jnp.zeros(state_scr.shape, F32)

    row = lax.broadcasted_iota(I32, (CHUNK, CHUNK), 0)
    col = lax.broadcasted_iota(I32, (CHUNK, CHUNK), 1)
    tril = row >= col
    lane_lo = col < (LANES // 2)
    lane_lo_row = lane_lo[0:1, :]

    for h in range(GM_HEADS):
        sl = slice(h * LANES, (h + 1) * LANES)
        gu = _gelu(u_ref[:, sl])
        gv = _gelu(v_ref[:, sl])
        mu = jnp.mean(gv, axis=-1, keepdims=True)
        dv = gv - mu
        var = jnp.mean(dv * dv, axis=-1, keepdims=True)
        vn = dv * lax.rsqrt(var + EPS) * lnw_ref[:, sl] + lnb_ref[:, sl]
        w = jnp.where(tril, ws_ref[h], 0.0).astype(BF16)
        s = jnp.dot(w, vn.astype(BF16), preferred_element_type=F32) + bst_ref[:, h:h + 1]
        y_ref[:, sl] = (gu * s).astype(y_ref.dtype)

    buf_scr[HALO:HALO + CHUNK, :] = xbc_ref[...]
    for cb in range(CONV_DIM // 256):
        cs_ = slice(cb * 256, (cb + 1) * 256)
        first = HALO - (SSD_CONV - 1)
        acc = cb_ref[:, cs_] + cw_ref[0:1, cs_] * buf_scr[first:first + CHUNK, cs_]
        for k in range(1, SSD_CONV):
            acc = acc + cw_ref[k:k + 1, cs_] * buf_scr[first + k:first + k + CHUNK, cs_]
        xa_scr[:, cs_] = _silu(acc)
    buf_scr[0:HALO, :] = buf_scr[CHUNK:CHUNK + HALO, :]

    dt = _softplus(dt_ref[...] + dtb_ref[...])
    a = -jnp.exp(alog_ref[...])
    ad = dt * a
    cs = jnp.dot(tril.astype(F32), ad, precision=lax.Precision.HIGHEST, preferred_element_type=F32)
    cs_t = cs.T
    last = cs[CHUNK - 1:CHUNK, :]
    ds = jnp.exp(last - cs)
    ecs = jnp.exp(cs)
    cd = jnp.exp(last)

    pairs_per_group = SSD_HEADS // SSD_GROUPS // 2
    gw = SSD_WIDTH // SSD_GROUPS
    for g in range(SSD_GROUPS):
        bm_g = xa_scr[:, SSD_WIDTH + g * SSD_STATE:SSD_WIDTH + (g + 1) * SSD_STATE]
        cm_g = xa_scr[:, SSD_WIDTH + (SSD_GROUPS + g) * SSD_STATE:SSD_WIDTH + (SSD_GROUPS + g + 1) * SSD_STATE]
        cmb = cm_g.astype(BF16)
        bmb = bm_g.astype(BF16)
        cbm = lax.dot_general(cmb, bmb, (((1,), (1,)), ((), ())), preferred_element_type=F32)
        bm_t = bm_g.T.astype(BF16)
        st_prev = state_scr[:, g * gw:(g + 1) * gw]
        y_off = jnp.dot(cmb, st_prev.astype(BF16), preferred_element_type=F32)
        xdds = []
        cds = []
        spread = (lax.broadcasted_iota(I32, (LANES, gw), 0)
                  == lax.broadcasted_iota(I32, (LANES, gw), 1) // (LANES // 2) + g * 2 * pairs_per_group
                  ).astype(BF16)

        def head_spread(m):
            hi = m.astype(BF16)
            lo = (m - hi.astype(F32)).astype(BF16)
            return (jnp.dot(hi, spread, preferred_element_type=F32)
                    + jnp.dot(lo, spread, preferred_element_type=F32))

        dt_g, ecs_g, ds_g = head_spread(dt), head_spread(ecs), head_spread(ds)
        for q in range(pairs_per_group):
            p = g * pairs_per_group + q
            sl = slice(p * LANES, (p + 1) * LANES)
            ql = slice(q * LANES, (q + 1) * LANES)
            xs_p = xa_scr[:, sl]
            xd = xs_p * dt_g[:, ql]
            xdb = xd.astype(BF16)
            ys = []
            for hh in (2 * p, 2 * p + 1):
                diff = cs[:, hh:hh + 1] - cs_t[hh:hh + 1, :]
                lm = jnp.where(tril, jnp.exp(jnp.where(tril, diff, 0.0)), 0.0)
                wmat = (cbm * lm).astype(BF16)
                ys.append(jnp.dot(wmat, xdb, preferred_element_type=F32))
            y_diag = jnp.where(lane_lo, ys[0], ys[1])
            y = y_diag + y_off[:, ql] * ecs_g[:, ql]
            ys_scr[:, sl] = y + dsk_ref[:, sl] * xs_p
            xdds.append((xd * ds_g[:, ql]).astype(BF16))
            cds.append(jnp.where(lane_lo_row, cd[:, 2 * p:2 * p + 1], cd[:, 2 * p + 1:2 * p + 2]))
        st_new = jnp.dot(bm_t, jnp.concatenate(xdds, axis=1), preferred_element_type=F32)
        state_scr[:, g * gw:(g + 1) * gw] = st_prev * jnp.concatenate(cds, axis=1) + st_new

    for g in range(SSD_GROUPS):
        sl = slice(g * gw, (g + 1) * gw)
        gg = ys_scr[:, sl] * _silu(z_ref[:, sl])
        ms = jnp.mean(gg * gg, axis=-1, keepdims=True)
        y_ref[:, GM_WIDTH + g * gw:GM_WIDTH + (g + 1) * gw] = (
            gg * lax.rsqrt(ms + EPS) * nw_ref[:, sl]).astype(y_ref.dtype)


def _mixer(proj, dt_raw, p, batch, seq):
    t = proj.shape[0]
    nc = seq // CHUNK
    rows = lambda b, c: b * nc + c
    full = lambda shape: pl.BlockSpec(shape, lambda b, c: (0,) * len(shape))
    return pl.pallas_call(
        _mixer_kernel,
        grid=(batch, nc),
        in_specs=[
            pl.BlockSpec((CHUNK, GM_WIDTH), lambda b, c: (rows(b, c), 0)),
            pl.BlockSpec((CHUNK, GM_WIDTH), lambda b, c: (rows(b, c), 1)),
            pl.BlockSpec((CHUNK, SSD_WIDTH), lambda b, c: (rows(b, c), 2)),
            pl.BlockSpec((CHUNK, CONV_DIM), lambda b, c: (rows(b, c), 2)),
            pl.BlockSpec((CHUNK, LANES), lambda b, c: (rows(b, c), 0)),
            full((1, GM_WIDTH)), full((1, GM_WIDTH)),
            full((GM_HEADS, CHUNK, CHUNK)), full((CHUNK, GM_HEADS)),
            full((SSD_CONV, CONV_DIM)), full((1, CONV_DIM)),
            full((1, LANES)), full((1, LANES)),
            full((1, SSD_WIDTH)), full((1, SSD_WIDTH)),
        ],
        out_specs=pl.BlockSpec((CHUNK, GM_WIDTH + SSD_WIDTH), lambda b, c: (rows(b, c), 0)),
        out_shape=jax.ShapeDtypeStruct((t, GM_WIDTH + SSD_WIDTH), BF16),
        scratch_shapes=[
            pltpu.VMEM((HALO + CHUNK, CONV_DIM), F32),
            pltpu.VMEM((CHUNK, CONV_DIM), F32),
            pltpu.VMEM((SSD_STATE, SSD_WIDTH), F32),
            pltpu.VMEM((CHUNK, SSD_WIDTH), F32),
        ],
        compiler_params=_cparams(("parallel", "arbitrary")),
        name="mixer",
    )(proj, proj, proj, proj, dt_raw,
      p["lnw"], p["lnb"], p["ws"], p["bst"], p["cw"], p["cb"], p["dtb"], p["alog"], p["dsk"], p["nw"])


def _first_max(vals, axis_iota, n):
    m = jnp.max(vals, axis=0, keepdims=True)
    idx = jnp.min(jnp.where(vals == m, axis_iota, n), axis=0, keepdims=True)
    return m, idx


def _post_kernel(y_ref, wout_ref, x_ref, g1_ref, n2w_ref, s2_ref, sh2_ref, rwt_ref, rb_ref,
                 xo_ref, h2_ref, eidx_ref, gate_ref, rank_ref, cnt_ref, carry_scr):
    @pl.when(pl.program_id(0) == 0)
    def _():
        carry_scr[...] = jnp.zeros(carry_scr.shape, F32)

    tm = x_ref.shape[0]
    mix = jnp.dot(y_ref[...], wout_ref[...], preferred_element_type=F32)
    x = x_ref[...] + g1_ref[...] * mix
    xo_ref[...] = x
    ms = jnp.mean(x * x, axis=-1, keepdims=True)
    h = x * lax.rsqrt(ms + EPS) * n2w_ref[...] * (1.0 + s2_ref[...]) + sh2_ref[...]
    _store_slab_rows(h2_ref, 0, h)

    logits_t = lax.dot_general(rwt_ref[...], h.astype(BF16), _NT, preferred_element_type=F32)
    scores = jax.nn.sigmoid(logits_t)
    biased = scores + rb_ref[...]

    sub = lax.broadcasted_iota(I32, (EXPERTS_PER_GROUP, tm), 0)
    neg = jnp.float32(-jnp.inf)
    best = None
    for g in range(N_EXPERT_GROUPS):
        grp = biased[g * EXPERTS_PER_GROUP:(g + 1) * EXPERTS_PER_GROUP, :]
        m1, i1 = _first_max(grp, sub, EXPERTS_PER_GROUP)
        m2, i2 = _first_max(jnp.where(sub == i1, neg, grp), sub, EXPERTS_PER_GROUP)
        gs = m1 + m2
        if best is None:
            best, bi, l1, l2 = gs, jnp.zeros((1, tm), I32), i1, i2
        else:
            upd = gs > best
            best = jnp.where(upd, gs, best)
            bi = jnp.where(upd, g, bi)
            l1 = jnp.where(upd, i1, l1)
            l2 = jnp.where(upd, i2, l2)
    e0 = bi * EXPERTS_PER_GROUP + l1
    e1 = bi * EXPERTS_PER_GROUP + l2

    eio = lax.broadcasted_iota(I32, (N_EXPERTS, tm), 0)
    oh0 = eio == e0
    oh1 = eio == e1
    s0 = jnp.sum(jnp.where(oh0, scores, 0.0), axis=0, keepdims=True)
    s1 = jnp.sum(jnp.where(oh1, scores, 0.0), axis=0, keepdims=True)
    tot = s0 + s1
    eidx_ref[0:1, :] = e0
    eidx_ref[1:2, :] = e1
    gate_ref[0:1, :] = s0 / tot
    gate_ref[1:2, :] = s1 / tot

    ohs = oh0.astype(F32) + oh1.astype(F32)
    tr = lax.broadcasted_iota(I32, (tm, tm), 0)
    tc = lax.broadcasted_iota(I32, (tm, tm), 1)
    before = (tr < tc).astype(BF16)
    prefix = jnp.dot(ohs.astype(BF16), before, preferred_element_type=F32)
    base = carry_scr[:, 0:1] + prefix
    rank_ref[0:1, :] = jnp.sum(jnp.where(oh0, base, 0.0), axis=0, keepdims=True).astype(I32)
    rank_ref[1:2, :] = jnp.sum(jnp.where(oh1, base, 0.0), axis=0, keepdims=True).astype(I32)
    carry_scr[...] = carry_scr[...] + jnp.sum(ohs, axis=1, keepdims=True)
    cnt_ref[...] = carry_scr[...]


def _post(y_mix, w_out, xf, mod, norm2_w, rw_t, rb, seq):
    t, d = xf.shape
    dm = y_mix.shape[1]
    tm = min(512, seq)
    per_batch = seq // tm
    modspec = lambda k: pl.BlockSpec((None, None, 1, d), lambda i: (i // per_batch, k, 0, 0))
    tok = pl.BlockSpec((2, tm), lambda i: (0, i))
    return pl.pallas_call(
        _post_kernel,
        grid=(t // tm,),
        in_specs=[
            pl.BlockSpec((tm, dm), lambda i: (i, 0)),
            pl.BlockSpec((dm, d), lambda i: (0, 0)),
            pl.BlockSpec((tm, d), lambda i: (i, 0)),
            modspec(2),
            pl.BlockSpec((1, d), lambda i: (0, 0)),
            modspec(4),
            modspec(3),
            pl.BlockSpec((N_EXPERTS, d), lambda i: (0, 0)),
            pl.BlockSpec((N_EXPERTS, 1), lambda i: (0, 0)),
        ],
        out_specs=[
            pl.BlockSpec((tm, d), lambda i: (i, 0)),
            pl.BlockSpec((tm * SLAB, d // SLAB), lambda i: (i, 0)),
            tok, tok, tok,
            pl.BlockSpec((N_EXPERTS, LANES), lambda i: (0, 0)),
        ],
        out_shape=[
            jax.ShapeDtypeStruct((t, d), F32),
            jax.ShapeDtypeStruct((t * SLAB, d // SLAB), F32),
            jax.ShapeDtypeStruct((2, t), I32),
            jax.ShapeDtypeStruct((2, t), F32),
            jax.ShapeDtypeStruct((2, t), I32),
            jax.ShapeDtypeStruct((N_EXPERTS, LANES), F32),
        ],
        scratch_shapes=[pltpu.VMEM((N_EXPERTS, LANES), F32)],
        compiler_params=_cparams(("arbitrary",)),
        name="post",
    )(y_mix, w_out, xf, mod, norm2_w.reshape(1, d), mod, mod, rw_t, rb.reshape(N_EXPERTS, 1))


META_ROWS = 8
META_LANES = 256
ROW_BLK_E, ROW_CNT, ROW_NUSED, ROW_NVALID = 0, 1, 2, 3


def _col_to_row(colv):
    n = colv.shape[0]
    r = lax.broadcasted_iota(I32, (n, n), 0)
    c = lax.broadcasted_iota(I32, (n, n), 1)
    return jnp.sum(jnp.where(r == c, colv, 0.0), axis=0, keepdims=True)


def _meta_kernel(cnt_ref, eidx_ref, rank_ref, dest_ref, meta_ref):
    t = eidx_ref.shape[1]
    cnt = cnt_ref[...]
    nblk = jnp.floor((cnt + (MOE_BLOCK - 1)) * (1.0 / MOE_BLOCK))
    r = lax.broadcasted_iota(I32, (N_EXPERTS, N_EXPERTS), 0)
    c = lax.broadcasted_iota(I32, (N_EXPERTS, N_EXPERTS), 1)
    lower = (c < r).astype(BF16)
    pstart = jnp.dot(lower, nblk.astype(BF16), preferred_element_type=F32)
    pend = pstart + nblk

    chunk = min(1024, t)
    for j in range(t // chunk):
        sl = slice(j * chunk, (j + 1) * chunk)
        eio = lax.broadcasted_iota(I32, (N_EXPERTS, chunk), 0)
        for k in range(2):
            oh = eio == eidx_ref[k:k + 1, sl]
            ps = jnp.sum(jnp.where(oh, pstart[:, 0:1], 0.0), axis=0, keepdims=True)
            dest_ref[k:k + 1, sl] = (ps * MOE_BLOCK).astype(I32) + rank_ref[k:k + 1, sl]

    bl = lax.broadcasted_iota(I32, (N_EXPERTS, META_LANES), 1).astype(F32)
    raw = jnp.sum((pend[:, 0:1] <= bl).astype(F32), axis=0, keepdims=True)
    raw = jnp.minimum(raw, N_EXPERTS - 1.0)
    nused = pend[N_EXPERTS - 1:N_EXPERTS, 0:1]
    used = bl[0:1, :] < nused
    last_e = jnp.max(jnp.where(used, raw, 0.0), axis=1, keepdims=True)
    meta_ref[...] = jnp.zeros(meta_ref.shape, I32)
    meta_ref[ROW_BLK_E:ROW_BLK_E + 1, :] = jnp.where(used, raw, last_e).astype(I32)
    meta_ref[ROW_CNT:ROW_CNT + 1, 0:N_EXPERTS] = _col_to_row(cnt[:, 0:1]).astype(I32)
    meta_ref[ROW_NUSED:ROW_NUSED + 1, :] = jnp.broadcast_to(nused, (1, META_LANES)).astype(I32)
    mine = lax.broadcasted_iota(I32, (N_EXPERTS, META_LANES), 0).astype(F32) == raw
    cnt_b = jnp.sum(jnp.where(mine, cnt[:, 0:1], 0.0), axis=0, keepdims=True)
    first_b = jnp.sum(jnp.where(mine, pstart[:, 0:1], 0.0), axis=0, keepdims=True)
    nvalid = jnp.clip(cnt_b - (bl[0:1, :] - first_b) * MOE_BLOCK, 0.0, float(MOE_BLOCK))
    meta_ref[ROW_NVALID:ROW_NVALID + 1, :] = jnp.where(used, nvalid, 0.0).astype(I32)


def _meta(cnt, eidx, rank):
    t = eidx.shape[1]
    full = lambda shape: pl.BlockSpec(shape, lambda: (0,) * len(shape))
    return pl.pallas_call(
        _meta_kernel,
        in_specs=[full((N_EXPERTS, LANES)), full((2, t)), full((2, t))],
        out_specs=[full((2, t)), full((META_ROWS, META_LANES))],
        out_shape=[jax.ShapeDtypeStruct((2, t), I32), jax.ShapeDtypeStruct((META_ROWS, META_LANES), I32)],
        name="meta",
    )(cnt, eidx, rank)


def _invert_kernel(dest0_ref, dest1_ref, code_ref):
    i = pl.program_id(0)
    tm = dest0_ref.shape[0]
    t = tm * pl.num_programs(0)

    @pl.when(i == 0)
    def _():
        def clear(p, carry):
            code_ref[p] = 0
            return carry
        lax.fori_loop(0, code_ref.shape[0], clear, 0, unroll=16)

    def put(tok, carry):
        code_ref[dest0_ref[tok]] = i * tm + tok
        code_ref[dest1_ref[tok]] = t + i * tm + tok
        return carry
    lax.fori_loop(0, tm, put, 0, unroll=8)


def _invert(dest, n_rows):
    t = dest.shape[1]
    tm = min(1024, t)
    slots = pl.BlockSpec((tm,), lambda i: (i,), memory_space=pltpu.SMEM)
    return pl.pallas_call(
        _invert_kernel,
        grid=(t // tm,),
        in_specs=[slots, slots],
        out_specs=pl.BlockSpec((n_rows,), lambda i: (0,), memory_space=pltpu.SMEM),
        out_shape=jax.ShapeDtypeStruct((n_rows,), I32),
        compiler_params=_cparams(("arbitrary",)),
        name="invert",
    )(dest[0], dest[1])


def _expert_kernel(be_sm, nu_sm, cnt_sm, nv_sm, code_sm, h2_hbm, wg_hbm, wu_hbm, wd_hbm, yk_hbm, xbuf, obuf,
                   land_g, land_u, land_d, wg_scr, wu_scr, wd_scr, sems, gsem, ssem, st_sm, *, layer):
    b = pl.program_id(0)
    nu = nu_sm[0]
    par = b & 1
    t = h2_hbm.shape[0] // SLAB
    n_blocks = pl.num_programs(0)
    blk_rows = MOE_BLOCK * VMEM_PITCH

    def hbm_rows(ref, index, n=1):
        return ref.at[pl.ds(pl.multiple_of(index * SLAB, SUBLANES), n * SLAB)]

    def vmem_row(ref, slot, r):
        return ref.at[pl.ds(pl.multiple_of((slot * MOE_BLOCK + r) * VMEM_PITCH, SUBLANES), SLAB)]

    def vmem_span(ref, slot, n):
        return ref.at[pl.ds(pl.multiple_of(slot * blk_rows, SUBLANES), n * SLAB)]

    def gather_row(ahead, r, slot):
        code = code_sm[(b + ahead) * MOE_BLOCK + r]
        if t & (t - 1) == 0:
            tok = code & (t - 1)
        else:
            tok = jnp.where(code >= t, code - t, code)
        return pltpu.make_async_copy(hbm_rows(h2_hbm, tok), vmem_row(xbuf, slot, r), gsem.at[slot])

    def scatter_row(r, slot):
        return pltpu.make_async_copy(vmem_row(obuf, slot, r), hbm_rows(yk_hbm, code_sm[b * MOE_BLOCK + r]),
                                     ssem.at[slot])

    def gather_sized(slot):
        return lambda n: pltpu.make_async_copy(hbm_rows(h2_hbm, 0, n), vmem_span(xbuf, slot, n), gsem.at[slot])

    def scatter_sized(slot):
        return lambda n: pltpu.make_async_copy(vmem_span(obuf, slot, n), hbm_rows(yk_hbm, 0, n), ssem.at[slot])

    def start_rows(n, row_copy, queues):
        groups = n // SUBLANES

        def eight(g, c):
            for u in range(SUBLANES):
                row_copy(g * SUBLANES + u).start(priority=queues[u % len(queues)])
            return c
        lax.fori_loop(0, groups, eight, 0)
        lax.fori_loop(groups * SUBLANES, n, lambda r, c: (row_copy(r).start(priority=queues[0]), c)[1], 0)

    def wait_rows(n, row_copy, sized_copy):
        @pl.when(n == MOE_BLOCK)
        def _():
            sized_copy(MOE_BLOCK).wait()

        @pl.when(n != MOE_BLOCK)
        def _():
            groups = n // SUBLANES
            lax.fori_loop(0, groups, lambda g, c: (sized_copy(SUBLANES).wait(), c)[1], 0)
            lax.fori_loop(groups * SUBLANES, n, lambda r, c: (row_copy(r).wait(), c)[1], 0)

    def fetch(e, slot):
        return (pltpu.make_async_copy(wg_hbm.at[layer, e], land_g.at[slot], sems.at[slot, 0]),
                pltpu.make_async_copy(wu_hbm.at[layer, e], land_u.at[slot], sems.at[slot, 1]),
                pltpu.make_async_copy(wd_hbm.at[layer, e], land_d.at[slot], sems.at[slot, 2]))

    def next_used(e):
        return lax.while_loop(lambda n: (n < N_EXPERTS) & (cnt_sm[jnp.minimum(n, N_EXPERTS - 1)] == 0),
                              lambda n: n + 1, e + 1)

    def start_next(slot):
        nxt = next_used(st_sm[1])

        @pl.when(nxt < N_EXPERTS)
        def _():
            for cp in fetch(nxt, slot):
                cp.start(priority=WEIGHT_QUEUE)
        st_sm[1] = nxt

    @pl.when(b == 0)
    def _():
        st_sm[0] = 0
        st_sm[1] = -1
        start_next(0)
        start_next(1)
        xbuf[...] = jnp.zeros(xbuf.shape, xbuf.dtype)
        for ahead in range(GATHER_AHEAD):
            @pl.when(ahead < nu)
            def _():
                start_rows(nv_sm[ahead], lambda r: gather_row(ahead, r, ahead), GATHER_QUEUES)

    xslot = lax.rem(b, GATHER_AHEAD + 1)

    @pl.when(b < nu)
    def _():
        @pl.when(b + GATHER_AHEAD < nu)
        def _():
            start_rows(nv_sm[jnp.minimum(b + GATHER_AHEAD, n_blocks - 1)],
                       lambda r: gather_row(GATHER_AHEAD, r, lax.rem(b + GATHER_AHEAD, GATHER_AHEAD + 1)),
                       GATHER_QUEUES)

        prev = be_sm[jnp.maximum(b - 1, 0)]

        @pl.when((b == 0) | (be_sm[b] != prev))
        def _():
            slot = st_sm[0] & 1
            for cp in fetch(be_sm[b], slot):
                cp.wait()
            wg_scr[...] = land_g[slot].astype(BF16)
            wu_scr[...] = land_u[slot].astype(BF16)
            wd_scr[...] = land_d[slot].astype(BF16)
            st_sm[0] = st_sm[0] + 1
            start_next(slot)

        wait_rows(nv_sm[b], lambda r: gather_row(0, r, xslot), gather_sized(xslot))

        xb = jnp.concatenate(_load_slab_rows(xbuf, xslot * blk_rows, MOE_BLOCK, VMEM_PITCH),
                             axis=1).astype(BF16)
        gate = jnp.dot(xb, wg_scr[...], preferred_element_type=F32)
        up = jnp.dot(xb, wu_scr[...], preferred_element_type=F32)
        act = (_silu(gate) * up).astype(BF16)
        out = jnp.dot(act, wd_scr[...], preferred_element_type=F32)

        @pl.when(b >= 2)
        def _():
            wait_rows(nv_sm[jnp.maximum(b - 2, 0)], lambda r: scatter_row(r, par), scatter_sized(par))

        _store_slab_rows(obuf, par * blk_rows, out, VMEM_PITCH)
        start_rows(nv_sm[b], lambda r: scatter_row(r, par), SCATTER_QUEUES)

        @pl.when(b == nu - 1)
        def _():
            @pl.when(b >= 1)
            def _():
                wait_rows(nv_sm[jnp.maximum(b - 1, 0)], lambda r: scatter_row(r, 1 - par),
                          scatter_sized(1 - par))
            wait_rows(nv_sm[b], lambda r: scatter_row(r, par), scatter_sized(par))


def _experts(blk_e, nused, cnt_row, nvalid, code_sorted, h2, w_gate, w_up, w_down, layer):
    sw = h2.shape[1]
    t = h2.shape[0] // SLAB
    d = SLAB * sw
    de = w_gate.shape[3]
    n_blocks = code_sorted.shape[0] // MOE_BLOCK
    hbm = pl.BlockSpec(memory_space=pl.ANY)
    return pl.pallas_call(
        functools.partial(_expert_kernel, layer=layer),
        grid_spec=pltpu.PrefetchScalarGridSpec(
            num_scalar_prefetch=5,
            grid=(n_blocks,),
            in_specs=[hbm, hbm, hbm, hbm],
            out_specs=hbm,
            scratch_shapes=[
                pltpu.VMEM(((GATHER_AHEAD + 1) * MOE_BLOCK * VMEM_PITCH, sw), F32),
                pltpu.VMEM((2 * MOE_BLOCK * VMEM_PITCH, sw), F32),
                pltpu.VMEM((2, d, de), F32),
                pltpu.VMEM((2, d, de), F32),
                pltpu.VMEM((2, de, d), F32),
                pltpu.VMEM((d, de), BF16),
                pltpu.VMEM((d, de), BF16),
                pltpu.VMEM((de, d), BF16),
                pltpu.SemaphoreType.DMA((2, 3)),
                pltpu.SemaphoreType.DMA((GATHER_AHEAD + 1,)),
                pltpu.SemaphoreType.DMA((2,)),
                pltpu.SMEM((2,), I32),
            ],
        ),
        out_shape=jax.ShapeDtypeStruct((2 * t * SLAB, sw), F32),
        compiler_params=_cparams(("arbitrary",)),
        name="experts",
    )(blk_e, nused, cnt_row, nvalid, code_sorted, h2, w_gate, w_up, w_down)


def _combine_kernel(gate_ref, x_ref, g2_ref, fw_ref, y0_ref, y1_ref, out_ref, *, final):
    tm = x_ref.shape[0]
    r = lax.broadcasted_iota(I32, (tm, tm), 0)
    c = lax.broadcasted_iota(I32, (tm, tm), 1)
    eye = r == c
    w0 = jnp.sum(jnp.where(eye, gate_ref[0:1, :], 0.0), axis=1, keepdims=True)
    w1 = jnp.sum(jnp.where(eye, gate_ref[1:2, :], 0.0), axis=1, keepdims=True)
    for q in range(tm // COMBINE_ROWS):
        rs = slice(q * COMBINE_ROWS, (q + 1) * COMBINE_ROWS)
        base = q * COMBINE_ROWS * SLAB
        chunks = [c0 * w0[rs] + c1 * w1[rs] for c0, c1 in zip(_load_slab_rows(y0_ref, base, COMBINE_ROWS),
                                                              _load_slab_rows(y1_ref, base, COMBINE_ROWS))]
        if final:
            x = x_ref[rs, :] + g2_ref[...] * jnp.concatenate(chunks, axis=1)
            ms = jnp.mean(x * x, axis=-1, keepdims=True)
            out_ref[rs, :] = x * lax.rsqrt(ms + EPS) * fw_ref[...]
        else:
            sw = y0_ref.shape[1]
            for s, y in enumerate(chunks):
                sl = slice(s * sw, (s + 1) * sw)
                out_ref[rs, sl] = x_ref[rs, sl] + g2_ref[:, sl] * y


def _combine(gate, xf, mod, final_w, yk, seq, final):
    t, d = xf.shape
    tm = min(512, seq)
    per_batch = seq // tm
    return pl.pallas_call(
        functools.partial(_combine_kernel, final=final),
        grid=(t // tm,),
        in_specs=[
            pl.BlockSpec((2, tm), lambda i: (0, i)),
            pl.BlockSpec((tm, d), lambda i: (i, 0)),
            pl.BlockSpec((None, None, 1, d), lambda i: (i // per_batch, 5, 0, 0)),
            pl.BlockSpec((1, d), lambda i: (0, 0)),
            pl.BlockSpec((tm * SLAB, yk.shape[1]), lambda i: (i, 0)),
            pl.BlockSpec((tm * SLAB, yk.shape[1]), lambda i: (i + t // tm, 0)),
        ],
        out_specs=pl.BlockSpec((tm, d), lambda i: (i, 0)),
        out_shape=jax.ShapeDtypeStruct((t, d), F32),
        compiler_params=_cparams(("parallel",)),
        name="combine",
    )(gate, xf, mod, final_w.reshape(1, d), yk, yk)


def _pad_lanes(v, n=LANES):
    return jnp.pad(v, (0, n - v.shape[0])).reshape(1, n)


def kernel(x, c, ada_w, ada_b, norm1_w, w_in, gm_ln_w, gm_ln_b, gm_ws, gm_bs, conv_w, conv_b, dt_bias, a_log,
           d_skip, ssd_norm_w, w_out, norm2_w, router_w, router_b, exp_w_gate, exp_w_up, exp_w_down,
           final_norm_w):
    batch, seq, d = x.shape
    t = batch * seq
    depth = ada_w.shape[0]
    assert batch <= 8 and seq % CHUNK == 0 and w_in.shape[2] == MAIN_PROJ + SSD_HEADS
    n_rows = (-(-(t * 2) // MOE_BLOCK) + N_EXPERTS) * MOE_BLOCK
    assert n_rows // MOE_BLOCK <= META_LANES

    ada = _ada(jnp.pad(c, ((0, 8 - batch), (0, 0))), ada_w, ada_b)
    rw_t = router_w.T.astype(BF16)
    w_in_t = jnp.swapaxes(w_in, 1, 2)
    xf = x.reshape(t, d)
    for l in range(depth):
        mod = ada[l, :batch].reshape(batch, 6, 1, d)
        proj, dt_raw = _inproj(xf, norm1_w[l], mod, w_in_t, l, seq)
        mixer_params = dict(
            lnw=gm_ln_w[l].reshape(1, GM_WIDTH), lnb=gm_ln_b[l].reshape(1, GM_WIDTH),
            ws=gm_ws[l], bst=gm_bs[l].T,
            cw=conv_w[l], cb=conv_b[l].reshape(1, CONV_DIM),
            dtb=_pad_lanes(dt_bias[l]), alog=_pad_lanes(a_log[l]),
            dsk=jnp.repeat(d_skip[l], SSD_WIDTH // SSD_HEADS).reshape(1, SSD_WIDTH),
            nw=ssd_norm_w[l].reshape(1, SSD_WIDTH))
        y_mix = _mixer(proj, dt_raw, mixer_params, batch, seq)
        xf, h2, eidx, gate, rank, cnt = _post(y_mix, w_out[l].astype(BF16), xf, mod, norm2_w[l], rw_t,
                                              router_b, seq)
        dest, meta = _meta(cnt, eidx, rank)
        n_blocks = n_rows // MOE_BLOCK
        yk = _experts(meta[ROW_BLK_E, :n_blocks], meta[ROW_NUSED, :1], meta[ROW_CNT, :N_EXPERTS],
                      meta[ROW_NVALID, :n_blocks], _invert(dest, n_rows), h2, exp_w_gate, exp_w_up,
                      exp_w_down, l)
        xf = _combine(gate, xf, mod, final_norm_w, yk, seq, final=(l == depth - 1))
    return xf.reshape(batch, seq, d)
```
